```python
import math
import jax, jax.numpy as jnp
from jax import lax
import numpy as np

D_MODEL = 1024
BATCH = 2
SEQ = 16384
DEPTH = 2

ATT_HEADS = 8
ATT_KV_GROUPS = 2
ATT_HPG = ATT_HEADS // ATT_KV_GROUPS
HEAD_DIM = 64
D_ATT = ATT_HEADS * HEAD_DIM
D_KV = ATT_KV_GROUPS * HEAD_DIM
N_BRANCH = 3
CMP_LEN = 32
CMP_STRIDE = 16
CMP_HIDDEN = 256
SLC_BLOCK = 64
SLC_TOPN = 16
WINDOW = 512
Q_BLOCK = 128
SEL_FORCE = 1.0e4
REL_BUCKETS = 32
REL_MAX_DIST = 128
GMLP_GROUPS = 4
GMLP_GROUP_DIM = 64
D_GMLP = GMLP_GROUPS * GMLP_GROUP_DIM
GMLP_CHUNK = 128
SSM_HEADS = 4
SSM_HEAD_DIM = 64
D_SSM = SSM_HEADS * SSM_HEAD_DIM
SSM_GROUPS = 2
SSM_STATE = 128
SSM_CONV = 4
SSM_CHUNK = 256
D_XBC = D_SSM + 2 * SSM_GROUPS * SSM_STATE
D_MIX = D_ATT + D_GMLP + D_SSM
D_ATT_IN = D_ATT + 6 * D_KV + ATT_HEADS * N_BRANCH
D_GMLP_IN = 2 * D_GMLP
D_SSM_IN = D_SSM + D_XBC + SSM_HEADS
D_IN = D_ATT_IN + D_GMLP_IN + D_SSM_IN
D_FF = 2816
N_SUB = 3
ALPHA = (2 * DEPTH) ** 0.25
BETA = (8 * DEPTH) ** -0.25
LN_EPS = 1e-5

kernel_name = "hybrid_nsa_gmlp_ssd_macaron_deepnorm"


def layer_norm(x, g, b):
    xf = x.astype(jnp.float32)
    mu = xf.mean(-1, keepdims=True)
    var = jnp.square(xf - mu).mean(-1, keepdims=True)
    return ((xf - mu) * lax.rsqrt(var + LN_EPS)).astype(x.dtype) * g + b


def swiglu(h, w1, w3, w2):
    return (jax.nn.silu(h @ w1) * (h @ w3)) @ w2


def masked_softmax(s, mask):
    s = jnp.where(mask, s.astype(jnp.float32), -jnp.inf)
    m = jnp.max(s, axis=-1, keepdims=True)
    m = jnp.where(jnp.isfinite(m), m, 0.0)
    p = jnp.exp(s - m)
    return p / jnp.maximum(p.sum(-1, keepdims=True), 1e-30)


def rel_bucket(dist):
    n = jnp.maximum(dist, 0)
    max_exact = REL_BUCKETS // 2
    nf = jnp.maximum(n, 1).astype(jnp.float32)
    large = max_exact + (jnp.log(nf / max_exact) / math.log(REL_MAX_DIST / max_exact)
                         * (REL_BUCKETS - max_exact)).astype(jnp.int32)
    large = jnp.minimum(large, REL_BUCKETS - 1)
    return jnp.where(n < max_exact, n, large)


def nsa_attention(q, k_cmp, v_cmp, k_slc, v_slc, k_win, v_win, gates, rel_bias,
                  cmp_pe, cmp_w1, cmp_b1, cmp_w2):
    Bn, T = q.shape[:2]
    G, I, Dh = ATT_KV_GROUPS, ATT_HPG, HEAD_DIM
    n_cmp = (T - CMP_LEN) // CMP_STRIDE + 1
    n_slc = T // SLC_BLOCK
    n_qb = T // Q_BLOCK
    top_n = min(SLC_TOPN, n_slc)
    kw_len = Q_BLOCK + WINDOW

    cmp_start = np.arange(n_cmp) * CMP_STRIDE
    blk_idx = cmp_start[:, None] + np.arange(CMP_LEN)[None, :]

    def compress(kv, j):
        blocks = kv[:, blk_idx] + cmp_pe[j][None, None, :, None, :]
        blocks = blocks.transpose(0, 3, 1, 2, 4).reshape(Bn, G, n_cmp, CMP_LEN * Dh)
        return jax.nn.gelu(blocks @ cmp_w1[j] + cmp_b1[j]) @ cmp_w2[j]

    kc, vc = compress(k_cmp, 0), compress(v_cmp, 1)
    cmp_end = jnp.asarray(cmp_start + CMP_LEN - 1, jnp.int32)
    slc_start = np.arange(n_slc) * SLC_BLOCK
    overlap = np.clip(np.minimum(cmp_start[:, None] + CMP_LEN, slc_start[None, :] + SLC_BLOCK)
                      - np.maximum(cmp_start[:, None], slc_start[None, :]), 0, None)
    overlap = jnp.asarray(overlap / CMP_LEN, jnp.float32)

    ks = k_slc.transpose(0, 2, 1, 3).reshape(Bn, G, n_slc, SLC_BLOCK, Dh)
    vs = v_slc.transpose(0, 2, 1, 3).reshape(Bn, G, n_slc, SLC_BLOCK, Dh)
    pad_w = ((0, 0), (0, 0), (WINDOW, 0), (0, 0))
    kw = jnp.pad(k_win.transpose(0, 2, 1, 3), pad_w)
    vw = jnp.pad(v_win.transpose(0, 2, 1, 3), pad_w)

    win_dist = np.arange(Q_BLOCK)[:, None] + WINDOW - np.arange(kw_len)[None, :]
    win_mask = jnp.asarray((win_dist >= 0) & (win_dist < WINDOW))
    win_bias = rel_bias[rel_bucket(jnp.asarray(win_dist, jnp.int32))]
    win_bias = win_bias.reshape(Q_BLOCK, kw_len, G, I).transpose(2, 3, 0, 1)
    rb = rel_bias.reshape(REL_BUCKETS, G, I)
    gather_blocks = jax.vmap(jax.vmap(lambda kb, s: kb[s]))
    bias_by_group = jax.vmap(lambda tb, bk: tb[bk], in_axes=(1, 1), out_axes=1)

    qb_all = q.reshape(Bn, n_qb, Q_BLOCK, G, I, Dh).transpose(1, 0, 3, 4, 2, 5) * (Dh ** -0.5)
    gb_all = gates.reshape(Bn, n_qb, Q_BLOCK, G, I, N_BRANCH).transpose(1, 0, 3, 4, 2, 5)

    def block(args):
        b_idx, qb, gb = args
        start = b_idx * Q_BLOCK
        t = start + jnp.arange(Q_BLOCK, dtype=jnp.int32)
        dist_c = t[:, None] - cmp_end[None, :]
        bias_c = rel_bias[rel_bucket(dist_c)].reshape(Q_BLOCK, n_cmp, G, I).transpose(2, 3, 0, 1)
        p_c = masked_softmax(jnp.einsum('bgiqd,bgkd->bgiqk', qb, kc) + bias_c, dist_c >= 0)
        o_c = jnp.einsum('bgiqk,bgkd->bgiqd', p_c.astype(vc.dtype), vc)
        imp = jnp.einsum('bgiqk,kn->bgqn', p_c, overlap)
        cur = (t // SLC_BLOCK)[:, None]
        blk = jnp.arange(n_slc, dtype=jnp.int32)[None, :]
        forced = (blk == 0) | (blk == cur) | (blk == cur - 1)
        imp = jnp.where(forced, SEL_FORCE, jnp.where(blk <= cur, imp, -SEL_FORCE))
        _, sel = lax.top_k(imp, top_n)
        k_sel = gather_blocks(ks, sel).reshape(Bn, G, Q_BLOCK, top_n * SLC_BLOCK, Dh)
        v_sel = gather_blocks(vs, sel).reshape(Bn, G, Q_BLOCK, top_n * SLC_BLOCK, Dh)
        pos_sel = (sel[..., None] * SLC_BLOCK + jnp.arange(SLC_BLOCK, dtype=jnp.int32)
                   ).reshape(Bn, G, Q_BLOCK, top_n * SLC_BLOCK)
        dist_s = t[None, None, :, None] - pos_sel
        bias_s = bias_by_group(rb, rel_bucket(dist_s)).transpose(0, 1, 4, 2, 3)
        p_s = masked_softmax(jnp.einsum('bgiqd,bgqkd->bgiqk', qb, k_sel) + bias_s,
                             (dist_s >= 0)[:, :, None])
        o_s = jnp.einsum('bgiqk,bgqkd->bgiqd', p_s.astype(v_sel.dtype), v_sel)
        kb = lax.dynamic_slice_in_dim(kw, start, kw_len, axis=2)
        vb = lax.dynamic_slice_in_dim(vw, start, kw_len, axis=2)
        pos_w = start - WINDOW + jnp.arange(kw_len, dtype=jnp.int32)
        p_w = masked_softmax(jnp.einsum('bgiqd,bgkd->bgiqk', qb, kb) + win_bias,
                             win_mask & (pos_w >= 0)[None, :])
        o_w = jnp.einsum('bgiqk,bgkd->bgiqd', p_w.astype(vb.dtype), vb)
        return gb[..., 0:1] * o_c + gb[..., 1:2] * o_s + gb[..., 2:3] * o_w

    o = lax.map(block, (jnp.arange(n_qb, dtype=jnp.int32), qb_all, gb_all))
    return o.transpose(1, 0, 4, 2, 3, 5).reshape(Bn, T, ATT_HEADS * Dh)


def gmlp_spatial_gating(uv, ln_g, ln_b, ws, bs):
    Bn, T, _ = uv.shape
    u, v = jnp.split(jax.nn.gelu(uv), 2, axis=-1)
    v = layer_norm(v, ln_g, ln_b)
    v = v.reshape(Bn, T // GMLP_CHUNK, GMLP_CHUNK, GMLP_GROUPS, GMLP_GROUP_DIM)
    causal = jnp.tril(jnp.ones((GMLP_CHUNK, GMLP_CHUNK), dtype=bool))
    w = jnp.where(causal[None], ws, 0.0)
    sv = jnp.einsum('gts,bcsgd->bctgd', w, v) + bs.T[None, None, :, :, None]
    return u * sv.reshape(Bn, T, D_GMLP)


def segsum(a):
    L = a.shape[-1]
    cs = jnp.cumsum(a, axis=-1)
    mask = jnp.tril(jnp.ones((L, L), dtype=bool))
    return jnp.where(mask, cs[..., :, None] - cs[..., None, :], -jnp.inf)


def ssd_chunked(X, A, Bh, Ch):
    Bn, T, H, P = X.shape
    nc = -(-T // SSM_CHUNK)
    pad = nc * SSM_CHUNK - T
    def chunk(a):
        a = jnp.pad(a, [(0, 0), (0, pad)] + [(0, 0)] * (a.ndim - 2))
        return a.reshape(Bn, nc, SSM_CHUNK, *a.shape[2:])
    X, A, Bh, Ch = chunk(X), chunk(A), chunk(Bh), chunk(Ch)
    A = A.transpose(0, 3, 1, 2)
    A_cs = jnp.cumsum(A, axis=-1)
    Lmat = jnp.exp(segsum(A))
    scores = jnp.einsum('bclhn,bcshn->bhcls', Ch, Bh) * Lmat
    y_diag = jnp.einsum('bhcls,bcshp->bclhp', scores, X)
    decay = jnp.exp(A_cs[..., -1:] - A_cs)
    states = jnp.einsum('bclhn,bhcl,bclhp->bchpn', Bh, decay, X)
    chunk_decay = jnp.exp(A_cs[..., -1])

    def step(h, inp):
        s_c, dec = inp
        return dec[:, :, None, None] * h + s_c, h

    h0 = jnp.zeros((Bn, H, P, Bh.shape[-1]), X.dtype)
    _, prev = lax.scan(step, h0, (states.transpose(1, 0, 2, 3, 4), chunk_decay.transpose(2, 0, 1)))
    prev = prev.transpose(1, 0, 2, 3, 4)
    y_off = jnp.einsum('bclhn,bchpn,bhcl->bclhp', Ch, prev, jnp.exp(A_cs))
    return (y_diag + y_off).reshape(Bn, nc * SSM_CHUNK, H, P)[:, :T]


def mamba2_ssd(zxbcdt, conv_w, conv_b, dt_bias, a_log, d_skip, norm_g):
    Bn, T, _ = zxbcdt.shape
    z, xbc, dt = jnp.split(zxbcdt, [D_SSM, D_SSM + D_XBC], axis=-1)
    xbc = lax.conv_general_dilated(xbc, conv_w[:, None, :], window_strides=(1,),
                                   padding=[(SSM_CONV - 1, 0)],
                                   dimension_numbers=('NWC', 'WIO', 'NWC'),
                                   feature_group_count=D_XBC) + conv_b
    xbc = jax.nn.silu(xbc)
    xs, Bm, Cm = jnp.split(xbc, [D_SSM, D_SSM + SSM_GROUPS * SSM_STATE], axis=-1)
    dt = jax.nn.softplus((dt + dt_bias).astype(jnp.float32))
    A = -jnp.exp(a_log.astype(jnp.float32))
    X = xs.reshape(Bn, T, SSM_HEADS, SSM_HEAD_DIM)
    rep = SSM_HEADS // SSM_GROUPS
    Bh = jnp.repeat(Bm.reshape(Bn, T, SSM_GROUPS, SSM_STATE), rep, axis=2).astype(jnp.float32)
    Ch = jnp.repeat(Cm.reshape(Bn, T, SSM_GROUPS, SSM_STATE), rep, axis=2).astype(jnp.float32)
    y = ssd_chunked(X.astype(jnp.float32) * dt[..., None], A * dt, Bh, Ch)
    y = y + X.astype(jnp.float32) * d_skip.astype(jnp.float32)[:, None]
    g = (y.reshape(Bn, T, D_SSM) * jax.nn.silu(z.astype(jnp.float32))).reshape(Bn, T, SSM_GROUPS, -1)
    g = g * lax.rsqrt(jnp.mean(jnp.square(g), axis=-1, keepdims=True) + LN_EPS)
    return g.reshape(Bn, T, D_SSM).astype(zxbcdt.dtype) * norm_g


def hybrid_mixer(h, rel_bias, w_in, w_out, cmp_pe, cmp_w1, cmp_b1, cmp_w2,
                 gmlp_ln_g, gmlp_ln_b, gmlp_ws, gmlp_bs,
                 conv_w, conv_b, dt_bias, a_log, d_skip, norm_g):
    Bn, T, _ = h.shape
    proj = h @ w_in
    att_in, gmlp_in, ssm_in = jnp.split(proj, [D_ATT_IN, D_ATT_IN + D_GMLP_IN], axis=-1)
    cuts = [D_ATT + i * D_KV for i in range(7)]
    q, kc, vc, ks, vs, kw, vw, g = jnp.split(att_in, cuts, axis=-1)
    kvr = lambda a: a.reshape(Bn, T, ATT_KV_GROUPS, HEAD_DIM)
    gates = jax.nn.sigmoid(g).reshape(Bn, T, ATT_HEADS, N_BRANCH)
    o_att = nsa_attention(q.reshape(Bn, T, ATT_HEADS, HEAD_DIM), kvr(kc), kvr(vc), kvr(ks), kvr(vs),
                          kvr(kw), kvr(vw), gates, rel_bias, cmp_pe, cmp_w1, cmp_b1, cmp_w2)
    o_gmlp = gmlp_spatial_gating(gmlp_in, gmlp_ln_g, gmlp_ln_b, gmlp_ws, gmlp_bs)
    o_ssm = mamba2_ssd(ssm_in, conv_w, conv_b, dt_bias, a_log, d_skip, norm_g)
    return jnp.concatenate([o_att, o_gmlp, o_ssm], axis=-1) @ w_out


def post_norm_residual(x, y, gate, weight, g, b):
    return layer_norm(ALPHA * x + weight * (1.0 + gate) * y, g, b)


def setup_inputs(seed: int = 0) -> dict:
    key = jax.random.key(seed)
    k = jax.random.split(key, 30)
    nrm = lambda kk, shape, s: jax.random.normal(kk, shape, jnp.float32) * s
    L = DEPTH
    dt_init = jnp.exp(jax.random.uniform(k[25], (L, SSM_HEADS), jnp.float32,
                                         math.log(1e-3), math.log(1e-1)))
    return {
        "x": nrm(k[0], (BATCH, SEQ, D_MODEL), 1.0),
        "c": nrm(k[1], (BATCH, D_MODEL), 1.0),
        "rel_bias": nrm(k[2], (REL_BUCKETS, ATT_HEADS), 0.5),
        "ada_w": nrm(k[3], (L, D_MODEL, N_SUB * 3 * D_MODEL), 0.5 * D_MODEL ** -0.5),
        "ada_b": nrm(k[4], (L, N_SUB * 3 * D_MODEL), 0.02),
        "ln_g": 1.0 + nrm(k[5], (L, N_SUB, D_MODEL), 0.02),
        "ln_b": nrm(k[6], (L, N_SUB, D_MODEL), 0.02),
        "ffn_w1": nrm(k[7], (L, 2, D_MODEL, D_FF), BETA * D_MODEL ** -0.5),
        "ffn_w3": nrm(k[8], (L, 2, D_MODEL, D_FF), BETA * D_MODEL ** -0.5),
        "ffn_w2": nrm(k[9], (L, 2, D_FF, D_MODEL), BETA * D_FF ** -0.5),
        "w_in": nrm(k[10], (L, D_MODEL, D_IN), D_MODEL ** -0.5),
        "w_out": nrm(k[11], (L, D_MIX, D_MODEL), BETA * D_MIX ** -0.5),
        "cmp_pe": nrm(k[12], (L, 2, CMP_LEN, HEAD_DIM), 0.1),
        "cmp_w1": nrm(k[13], (L, 2, CMP_LEN * HEAD_DIM, CMP_HIDDEN), (CMP_LEN * HEAD_DIM) ** -0.5),
        "cmp_b1": nrm(k[14], (L, 2, CMP_HIDDEN), 0.02),
        "cmp_w2": nrm(k[15], (L, 2, CMP_HIDDEN, HEAD_DIM), 1.5 * CMP_HIDDEN ** -0.5),
        "gmlp_ln_g": 1.0 + nrm(k[16], (L, D_GMLP), 0.02),
        "gmlp_ln_b": nrm(k[17], (L, D_GMLP), 0.02),
        "gmlp_ws": nrm(k[18], (L, GMLP_GROUPS, GMLP_CHUNK, GMLP_CHUNK), GMLP_CHUNK ** -0.5),
        "gmlp_bs": 1.0 + nrm(k[19], (L, GMLP_GROUPS, GMLP_CHUNK), 0.02),
        "ssm_conv_w": nrm(k[20], (L, SSM_CONV, D_XBC), SSM_CONV ** -0.5),
        "ssm_conv_b": nrm(k[21], (L, D_XBC), 0.02),
        "ssm_dt_bias": dt_init + jnp.log(-jnp.expm1(-dt_init)),
        "ssm_a_log": jnp.log(jax.random.uniform(k[22], (L, SSM_HEADS), jnp.float32, 1.0, 16.0)),
        "ssm_d": 1.0 + nrm(k[23], (L, SSM_HEADS), 0.02),
        "ssm_norm_g": 1.0 + nrm(k[24], (L, D_SSM), 0.02),
    }


def reference(x, c, rel_bias, ada_w, ada_b, ln_g, ln_b, ffn_w1, ffn_w3, ffn_w2, w_in, w_out,
              cmp_pe, cmp_w1, cmp_b1, cmp_w2, gmlp_ln_g, gmlp_ln_b, gmlp_ws, gmlp_bs,
              ssm_conv_w, ssm_conv_b, ssm_dt_bias, ssm_a_log, ssm_d, ssm_norm_g):
    Bn = x.shape[0]
    for l in range(DEPTH):
        mod = (jax.nn.silu(c) @ ada_w[l] + ada_b[l]).reshape(Bn, N_SUB, 3, D_MODEL)
        shift, scale, gate = mod[:, :, 0, None], mod[:, :, 1, None], mod[:, :, 2, None]
        h = x * (1.0 + scale[:, 0]) + shift[:, 0]
        y = swiglu(h, ffn_w1[l, 0], ffn_w3[l, 0], ffn_w2[l, 0])
        x = post_norm_residual(x, y, gate[:, 0], 0.5, ln_g[l, 0], ln_b[l, 0])
        h = x * (1.0 + scale[:, 1]) + shift[:, 1]
        y = hybrid_mixer(h, rel_bias, w_in[l], w_out[l], cmp_pe[l], cmp_w1[l], cmp_b1[l], cmp_w2[l],
                         gmlp_ln_g[l], gmlp_ln_b[l], gmlp_ws[l], gmlp_bs[l],
                         ssm_conv_w[l], ssm_conv_b[l], ssm_dt_bias[l], ssm_a_log[l], ssm_d[l],
                         ssm_norm_g[l])
        x = post_norm_residual(x, y, gate[:, 1], 1.0, ln_g[l, 1], ln_b[l, 1])
        h = x * (1.0 + scale[:, 2]) + shift[:, 2]
        y = swiglu(h, ffn_w1[l, 1], ffn_w3[l, 1], ffn_w2[l, 1])
        x = post_norm_residual(x, y, gate[:, 2], 0.5, ln_g[l, 2], ln_b[l, 2])
    return x
```

```python
import functools
import math

import numpy as np
import jax
import jax.numpy as jnp
from jax import lax
from jax.experimental import pallas as pl
from jax.experimental.pallas import tpu as pltpu

F32 = jnp.float32
BF16 = jnp.bfloat16
HI = lax.Precision.HIGHEST

D_MODEL = 1024
DEPTH = 2
ATT_HEADS = 8
ATT_KV_GROUPS = 2
ATT_HPG = 4
HEAD_DIM = 64
CMP_LEN = 32
CMP_STRIDE = 16
CMP_HIDDEN = 256
SLC_BLOCK = 64
SLC_TOPN = 16
WINDOW = 512
SEL_FORCE = 1.0e4
REL_BUCKETS = 32
REL_MAX_DIST = 128
GMLP_GROUPS = 4
GMLP_CHUNK = 128
D_GMLP = 256
SSM_HEADS = 4
SSM_HEAD_DIM = 64
D_SSM = 256
SSM_STATE = 128
SSM_CONV = 4
SSM_CHUNK = 256
D_XBC = 768
D_FF = 2816
N_SUB = 3
ALPHA = (2 * DEPTH) ** 0.25
LN_EPS = 1e-5

LANES = 128
NEG_BIG = -(2.0 ** 30)
KPAD = 512
CMP_KPAD = 16
TQ_C = 256
TQ = 256
TK = 256
TM = 512
FF_CHUNK = 256
SEL_CHUNK_BLOCKS = 128

COL_Q = 0
COL_KV = 1024
COL_GATE = 2560
COL_GMLP = 2688
COL_Z = 3200
COL_XBC = 3456
COL_DT = 4224
N_COLS = 4352


def _cparams(n_axes, vmem_mb):
    return pltpu.CompilerParams(dimension_semantics=("arbitrary",) * n_axes,
                                vmem_limit_bytes=vmem_mb * 1024 * 1024)


def _sigmoid(x):
    return 1.0 / (1.0 + jnp.exp(-x))


def _gelu_tanh(x):
    return x * (0.5 * (1.0 + jnp.tanh(math.sqrt(2.0 / math.pi) * (x + 0.044715 * (x * x * x)))))


def _layer_norm(z, g, b):
    mu = jnp.mean(z, axis=-1, keepdims=True)
    d = z - mu
    var = jnp.mean(d * d, axis=-1, keepdims=True)
    return d * lax.rsqrt(var + LN_EPS) * g + b


def _dot(a, b):
    return jnp.dot(a, b, preferred_element_type=F32)


def _dot_hi(a, b):
    return jnp.dot(a, b, preferred_element_type=F32, precision=HI)


def _dot_nt(a, b):
    return lax.dot_general(a, b, (((1,), (1,)), ((), ())), preferred_element_type=F32)


def _mod_body(c_ref, w_ref, b_ref, o_ref):
    c = c_ref[...]
    o_ref[0] = _dot_hi(c * _sigmoid(c), w_ref[0]) + b_ref[0]


def _adaln_mod(c, ada_w, ada_b):
    n_layers, d, n = ada_w.shape
    bn = c.shape[0]
    tn = 1536
    cp = jnp.zeros((8, d), F32).at[:bn].set(c)
    out = pl.pallas_call(
        _mod_body,
        grid=(n_layers, n // tn),
        in_specs=[pl.BlockSpec((8, d), lambda l, j: (0, 0)),
                  pl.BlockSpec((1, d, tn), lambda l, j: (l, 0, j)),
                  pl.BlockSpec((1, 1, tn), lambda l, j: (l, 0, j))],
        out_specs=pl.BlockSpec((1, 8, tn), lambda l, j: (l, 0, j)),
        out_shape=jax.ShapeDtypeStruct((n_layers, 8, n), F32),
        compiler_params=_cparams(2, 32),
        name="adaln_mod",
    )(cp, ada_w, ada_b.reshape(n_layers, 1, n))
    return out[:, :bn].reshape(n_layers, bn, N_SUB * 3, d)


def _ffn_body(x_ref, mod_ref, w1_ref, w3_ref, w2_ref, g_ref, b_ref, o_ref, *, sub):
    x = x_ref[0]
    m = mod_ref[0]
    shift, scale, gate = m[3 * sub:3 * sub + 1], m[3 * sub + 1:3 * sub + 2], m[3 * sub + 2:3 * sub + 3]
    h = (x * (1.0 + scale) + shift).astype(BF16)
    acc = jnp.zeros(x.shape, F32)
    for j in range(D_FF // FF_CHUNK):
        sl = slice(j * FF_CHUNK, (j + 1) * FF_CHUNK)
        a = _dot(h, w1_ref[:, sl])
        b = _dot(h, w3_ref[:, sl])
        acc = acc + _dot((a * _sigmoid(a) * b).astype(BF16), w2_ref[sl, :])
    z = ALPHA * x + (0.5 * (1.0 + gate)) * acc
    o_ref[0] = _layer_norm(z, g_ref[...], b_ref[...])


def _ffn_sublayer(x, mod_l, w1, w3, w2, g, b, sub):
    bn, t, d = x.shape
    full = lambda shape: pl.BlockSpec(shape, lambda bi, i: (0,) * len(shape))
    return pl.pallas_call(
        functools.partial(_ffn_body, sub=sub),
        grid=(bn, t // TM),
        in_specs=[pl.BlockSpec((1, TM, d), lambda bi, i: (bi, i, 0)),
                  pl.BlockSpec((1, N_SUB * 3, d), lambda bi, i: (bi, 0, 0)),
                  full((d, D_FF)), full((d, D_FF)), full((D_FF, d)),
                  full((1, d)), full((1, d))],
        out_specs=pl.BlockSpec((1, TM, d), lambda bi, i: (bi, i, 0)),
        out_shape=jax.ShapeDtypeStruct(x.shape, F32),
        compiler_params=_cparams(2, 56),
        name="ffn",
    )(x, mod_l, w1.astype(BF16), w3.astype(BF16), w2.astype(BF16), g[None], b[None])


def _pad_cols(w, width):
    return jnp.pad(w, ((0, 0), (0, width - w.shape[1])))


def _relayout_w_in(w_in):
    cols = [_pad_cols(w_in[:, HEAD_DIM * h:HEAD_DIM * (h + 1)], LANES) for h in range(ATT_HEADS)]
    for i in range(6):
        for g in range(ATT_KV_GROUPS):
            base = 512 + 128 * i + HEAD_DIM * g
            cols.append(_pad_cols(w_in[:, base:base + HEAD_DIM], LANES))
    cols.append(_pad_cols(w_in[:, 1280:1304], LANES))
    cols.append(w_in[:, 1304:1816])
    cols.append(w_in[:, 1816:2072])
    cols.append(w_in[:, 2072:2840])
    cols.append(_pad_cols(w_in[:, 2840:2844], LANES))
    return jnp.concatenate(cols, axis=1).astype(BF16)


def _inproj_body(x_ref, mod_ref, w_ref, qp_ref, kcv_ref, ksa_ref, vsp_ref, kwp_ref, vwp_ref,
                 gt_ref, gm_ref, z_ref, xbc_ref, dt_ref):
    i = pl.program_id(1)
    x = x_ref[0]
    m = mod_ref[0]
    h = (x * (1.0 + m[4:5]) + m[3:4]).astype(BF16)

    def proj(c0, width):
        return _dot(h, w_ref[:, c0:c0 + width])

    lane = lax.broadcasted_iota(jnp.int32, (1, LANES), 1)
    q_flag = jnp.where(lane == HEAD_DIM, NEG_BIG, 0.0)
    for hh in range(ATT_HEADS):
        qp_ref[0, hh] = (proj(COL_Q + LANES * hh, LANES) * (HEAD_DIM ** -0.5) + q_flag).astype(BF16)
    for k in range(4):
        kcv_ref[0, k] = proj(COL_KV + LANES * k, LANES).astype(BF16)
    gt_ref[0] = _sigmoid(proj(COL_GATE, LANES))
    gm_ref[0] = proj(COL_GMLP, 512)
    z_ref[0] = proj(COL_Z, D_SSM)
    xbc_ref[0] = proj(COL_XBC, D_XBC)
    dt_ref[0] = proj(COL_DT, LANES)

    @pl.when(i == 0)
    def _():
        dummy = jnp.broadcast_to(jnp.where(lane == HEAD_DIM, 1.0, 0.0), (TM, LANES)).astype(BF16)
        zeros = jnp.zeros((TM, LANES), BF16)
        for g in range(ATT_KV_GROUPS):
            ksa_ref[0, g, :, 0:LANES] = dummy
            ksa_ref[0, g, :, LANES:2 * LANES] = zeros
            kwp_ref[0, g] = dummy
            vsp_ref[0, g] = zeros
            vwp_ref[0, g] = zeros

    @pl.when(i > 0)
    def _():
        pos = (i - 1) * TM + lax.broadcasted_iota(jnp.int32, (TM, LANES), 0)
        blk_lane = (pos >> 6) & (SEL_CHUNK_BLOCKS - 1)
        onehot = jnp.where(blk_lane == lax.broadcasted_iota(jnp.int32, (TM, LANES), 1), 1.0, 0.0).astype(BF16)
        for g in range(ATT_KV_GROUPS):
            ksa_ref[0, g, :, 0:LANES] = proj(COL_KV + LANES * (4 + g), LANES).astype(BF16)
            ksa_ref[0, g, :, LANES:2 * LANES] = onehot
            vsp_ref[0, g] = proj(COL_KV + LANES * (6 + g), LANES).astype(BF16)
            kwp_ref[0, g] = proj(COL_KV + LANES * (8 + g), LANES).astype(BF16)
            vwp_ref[0, g] = proj(COL_KV + LANES * (10 + g), LANES).astype(BF16)


def _mixer_inproj(x, mod_l, w_in):
    bn, t, d = x.shape
    tp = t + KPAD
    row = lambda width: pl.BlockSpec((1, TM, width), lambda bi, i: (bi, jnp.maximum(i - 1, 0), 0))
    sds = jax.ShapeDtypeStruct
    return pl.pallas_call(
        _inproj_body,
        grid=(bn, t // TM + 1),
        in_specs=[row(d),
                  pl.BlockSpec((1, N_SUB * 3, d), lambda bi, i: (bi, 0, 0)),
                  pl.BlockSpec((d, N_COLS), lambda bi, i: (0, 0))],
        out_specs=[pl.BlockSpec((1, ATT_HEADS, TM, LANES), lambda bi, i: (bi, 0, jnp.maximum(i - 1, 0), 0)),
                   pl.BlockSpec((1, 4, TM, LANES), lambda bi, i: (bi, 0, jnp.maximum(i - 1, 0), 0)),
                   pl.BlockSpec((1, ATT_KV_GROUPS, TM, 2 * LANES), lambda bi, i: (bi, 0, i, 0)),
                   pl.BlockSpec((1, ATT_KV_GROUPS, TM, LANES), lambda bi, i: (bi, 0, i, 0)),
                   pl.BlockSpec((1, ATT_KV_GROUPS, TM, LANES), lambda bi, i: (bi, 0, i, 0)),
                   pl.BlockSpec((1, ATT_KV_GROUPS, TM, LANES), lambda bi, i: (bi, 0, i, 0)),
                   row(LANES), row(512), row(D_SSM), row(D_XBC), row(LANES)],
        out_shape=[sds((bn, ATT_HEADS, t, LANES), BF16),
                   sds((bn, 4, t, LANES), BF16),
                   sds((bn, ATT_KV_GROUPS, tp, 2 * LANES), BF16),
                   sds((bn, ATT_KV_GROUPS, tp, LANES), BF16),
                   sds((bn, ATT_KV_GROUPS, tp, LANES), BF16),
                   sds((bn, ATT_KV_GROUPS, tp, LANES), BF16),
                   sds((bn, t, LANES), F32), sds((bn, t, 512), F32), sds((bn, t, D_SSM), F32),
                   sds((bn, t, D_XBC), F32), sds((bn, t, LANES), F32)],
        compiler_params=_cparams(2, 48),
        name="mixer_inproj",
    )(x, mod_l, w_in)


def _cmp_rows(t):
    return -(-(t // CMP_STRIDE + 120) // LANES) * LANES


def _compress_body(c_ref, w1_ref, pe_ref, b1_ref, w2_ref, o_ref):
    c = c_ref[0, 0]
    n_rows = c.shape[0]
    half = CMP_STRIDE * HEAD_DIM
    w1 = w1_ref[0]
    h1 = _dot(c, w1[:half])
    h2 = _dot(c, w1[half:])
    cvec = _dot(pe_ref[0], w1)[0:1] + b1_ref[0]
    hid = h1 + pltpu.roll(h2, n_rows - 1, 0) + cvec
    out = _dot(_gelu_tanh(hid).astype(BF16), w2_ref[0]).astype(BF16)
    o_ref[0, 0, 0:CMP_KPAD] = jnp.zeros((CMP_KPAD, LANES), BF16)
    o_ref[0, 0, CMP_KPAD:CMP_KPAD + n_rows] = out
    tail = o_ref.shape[2] - CMP_KPAD - n_rows
    o_ref[0, 0, CMP_KPAD + n_rows:] = jnp.zeros((tail, LANES), BF16)


def _compress(kcv, cmp_pe, cmp_w1, cmp_b1, cmp_w2):
    bn, _, t, _ = kcv.shape
    n_rows = t // CMP_STRIDE
    c = kcv[..., :HEAD_DIM].reshape(bn, 4, n_rows, CMP_STRIDE * HEAD_DIM)
    pe = jnp.zeros((2, 8, CMP_LEN * HEAD_DIM), F32).at[:, 0].set(cmp_pe.reshape(2, -1)).astype(BF16)
    w2 = jnp.pad(cmp_w2, ((0, 0), (0, 0), (0, LANES - HEAD_DIM))).astype(BF16)
    rows = _cmp_rows(t)
    return pl.pallas_call(
        _compress_body,
        grid=(bn, 4),
        in_specs=[pl.BlockSpec((1, 1, n_rows, CMP_STRIDE * HEAD_DIM), lambda bi, k: (bi, k, 0, 0)),
                  pl.BlockSpec((1, CMP_LEN * HEAD_DIM, CMP_HIDDEN), lambda bi, k: (k // 2, 0, 0)),
                  pl.BlockSpec((1, 8, CMP_LEN * HEAD_DIM), lambda bi, k: (k // 2, 0, 0)),
                  pl.BlockSpec((1, 1, CMP_HIDDEN), lambda bi, k: (k // 2, 0, 0)),
                  pl.BlockSpec((1, CMP_HIDDEN, LANES), lambda bi, k: (k // 2, 0, 0))],
        out_specs=pl.BlockSpec((1, 1, rows, LANES), lambda bi, k: (bi, k, 0, 0)),
        out_shape=jax.ShapeDtypeStruct((bn, 4, rows, LANES), BF16),
        compiler_params=_cparams(2, 40),
        name="nsa_compress",
    )(c, cmp_w1.astype(BF16), pe, cmp_b1[:, None, :], w2)


def _rel_bucket(dist):
    n = jnp.maximum(dist, 0)
    max_exact = REL_BUCKETS // 2
    nf = jnp.maximum(n, 1).astype(F32)
    large = max_exact + (jnp.log(nf / max_exact) / math.log(REL_MAX_DIST / max_exact)
                         * (REL_BUCKETS - max_exact)).astype(jnp.int32)
    large = jnp.minimum(large, REL_BUCKETS - 1)
    return jnp.where(n < max_exact, n, large)


def _bias_tables(rel_bias):
    far = rel_bias[REL_BUCKETS - 1]

    def table(dist, valid, shift):
        b = rel_bias[_rel_bucket(dist)] - (far if shift else 0.0)
        return jnp.where(valid[..., None], b, -jnp.inf).transpose(2, 0, 1)

    qi = np.arange(TQ_C)[:, None]
    d_c = jnp.asarray(qi - CMP_STRIDE * np.arange(LANES)[None, :] + (CMP_STRIDE * CMP_KPAD - CMP_LEN + 1))
    nb_c = table(d_c, d_c >= 0, True)
    qi = np.arange(TQ)[:, None]
    d_s = jnp.asarray(qi - np.arange(2 * TK)[None, :] + TK)
    nb_s = table(d_s, d_s >= 0, True)
    d_w = jnp.asarray(qi - np.arange(TQ + WINDOW)[None, :] + WINDOW)
    nb_w = table(d_w, (d_w >= 0) & (d_w < WINDOW), False)
    return nb_c, nb_s, nb_w


def _overlap_matrix(t):
    n_cmp = (t - CMP_LEN) // CMP_STRIDE + 1
    n_slc = t // SLC_BLOCK
    cs = np.arange(n_cmp) * CMP_STRIDE
    ss = np.arange(n_slc) * SLC_BLOCK
    ov = np.clip(np.minimum(cs[:, None] + CMP_LEN, ss[None, :] + SLC_BLOCK)
                 - np.maximum(cs[:, None], ss[None, :]), 0, None) / CMP_LEN
    full = np.zeros((_cmp_rows(t), n_slc), np.float32)
    full[CMP_KPAD:CMP_KPAD + n_cmp] = ov
    return jnp.asarray(full)


def _heads_to_lanes(o, rows):
    return jnp.concatenate([o[i * rows:(i + 1) * rows, :HEAD_DIM] for i in range(ATT_HPG)], axis=1)


def _cmp_attn_body(q_ref, kc_ref, vc_ref, nb_ref, ov_ref, o_ref, imp_ref):
    m = pl.program_id(2)
    q = q_ref[0].reshape(ATT_HPG * TQ_C, LANES)
    kf = kc_ref[0, 0]
    vf = vc_ref[0, 0]
    n_keys = kf.shape[0]
    near0 = pl.multiple_of(m * (TQ_C // CMP_STRIDE), 16)
    col = lax.broadcasted_iota(jnp.int32, (1, n_keys), 1)
    sf = jnp.where((col >= CMP_KPAD) & (col < near0), _dot_nt(q, kf), -jnp.inf)
    kn = kc_ref[0, 0, pl.ds(near0, LANES), :]
    vn = vc_ref[0, 0, pl.ds(near0, LANES), :]
    coln = lax.broadcasted_iota(jnp.int32, (1, LANES), 1) + near0
    sn = jnp.where(coln >= CMP_KPAD, _dot_nt(q, kn) + nb_ref[...].reshape(ATT_HPG * TQ_C, LANES), -jnp.inf)
    mx = jnp.maximum(jnp.max(sf, axis=1, keepdims=True), jnp.max(sn, axis=1, keepdims=True))
    mx = jnp.where(mx == -jnp.inf, 0.0, mx)
    pf = jnp.exp(sf - mx)
    pn = jnp.exp(sn - mx)
    den = jnp.sum(pf, axis=1, keepdims=True) + jnp.sum(pn, axis=1, keepdims=True)
    inv = 1.0 / jnp.maximum(den, 1e-30)
    pf = pf * inv
    pn = pn * inv
    o = _dot(pf.astype(BF16), vf) + _dot(pn.astype(BF16), vn)
    o_ref[0] = _heads_to_lanes(o, TQ_C)
    pf_g = pf[0:TQ_C] + pf[TQ_C:2 * TQ_C] + pf[2 * TQ_C:3 * TQ_C] + pf[3 * TQ_C:4 * TQ_C]
    pn_g = pn[0:TQ_C] + pn[TQ_C:2 * TQ_C] + pn[2 * TQ_C:3 * TQ_C] + pn[3 * TQ_C:4 * TQ_C]
    imp_ref[0, 0] = _dot_hi(pf_g, ov_ref[...]) + _dot_hi(pn_g, ov_ref[pl.ds(near0, LANES), :])


def _cmp_attention(qp, kcp, nb_c, ov):
    bn, _, t, _ = qp.shape
    rows = kcp.shape[2]
    n_slc = ov.shape[1]
    return pl.pallas_call(
        _cmp_attn_body,
        grid=(bn, ATT_KV_GROUPS, t // TQ_C),
        in_specs=[pl.BlockSpec((1, ATT_HPG, TQ_C, LANES), lambda bi, g, m: (bi, g, m, 0)),
                  pl.BlockSpec((1, 1, rows, LANES), lambda bi, g, m: (bi, g, 0, 0)),
                  pl.BlockSpec((1, 1, rows, LANES), lambda bi, g, m: (bi, 2 + g, 0, 0)),
                  pl.BlockSpec((ATT_HPG, TQ_C, LANES), lambda bi, g, m: (g, 0, 0)),
                  pl.BlockSpec((rows, n_slc), lambda bi, g, m: (0, 0))],
        out_specs=[pl.BlockSpec((1, TQ_C, ATT_HPG * HEAD_DIM), lambda bi, g, m: (bi, m, g)),
                   pl.BlockSpec((1, 1, TQ_C, n_slc), lambda bi, g, m: (bi, g, m, 0))],
        out_shape=[jax.ShapeDtypeStruct((bn, t, ATT_HEADS * HEAD_DIM), F32),
                   jax.ShapeDtypeStruct((bn, ATT_KV_GROUPS, t, n_slc), F32)],
        compiler_params=_cparams(3, 40),
        name="nsa_cmp_attn",
    )(qp, kcp, kcp, nb_c, ov)


TOPK_ROWS = 1024


def _topk_body(imp_ref, o_ref):
    i = pl.program_id(2)
    imp = imp_ref[0, 0]
    rows, n_slc = imp.shape
    t = i * rows + lax.broadcasted_iota(jnp.int32, (rows, 1), 0)
    cur = t >> 6
    blk = lax.broadcasted_iota(jnp.int32, (rows, n_slc), 1)
    forced = (blk == 0) | (blk == cur) | (blk == cur - 1)
    vals = jnp.where(forced, SEL_FORCE, jnp.where(blk <= cur, imp, -SEL_FORCE))
    blkf = blk.astype(F32)
    mask = jnp.full((rows, n_slc), NEG_BIG, F32)
    for _ in range(min(SLC_TOPN, n_slc)):
        mx = jnp.max(vals, axis=1, keepdims=True)
        first = jnp.min(jnp.where(vals == mx, blkf, float(n_slc)), axis=1, keepdims=True)
        hit = blkf == first
        mask = jnp.where(hit, 0.0, mask)
        vals = jnp.where(hit, -jnp.inf, vals)
    o_ref[0, 0] = mask.astype(BF16)


def _select_blocks(imp):
    bn, g, t, n_slc = imp.shape
    rows = min(TOPK_ROWS, t)
    spec = pl.BlockSpec((1, 1, rows, n_slc), lambda bi, gi, i: (bi, gi, i, 0))
    return pl.pallas_call(
        _topk_body,
        grid=(bn, g, t // rows),
        in_specs=[spec],
        out_specs=spec,
        out_shape=jax.ShapeDtypeStruct(imp.shape, BF16),
        compiler_params=_cparams(3, 40),
        name="nsa_topk",
    )(imp)


def _sel_attn_body(q_ref, sel_ref, k_ref, v_ref, nb_ref, o_ref, qa_ref, m_ref, l_ref, acc_ref):
    m = pl.program_id(2)
    rows = ATT_HPG * TQ
    q = q_ref[0].reshape(rows, LANES)
    n_chunks = qa_ref.shape[0]
    for c in range(n_chunks):
        sel_c = sel_ref[0, 0, :, c * LANES:(c + 1) * LANES]
        qa_ref[c] = jnp.concatenate([q, jnp.concatenate([sel_c] * ATT_HPG, axis=0)], axis=1)
    m_ref[...] = jnp.full(m_ref.shape, -jnp.inf, F32)
    l_ref[...] = jnp.zeros(l_ref.shape, F32)
    acc_ref[...] = jnp.zeros(acc_ref.shape, F32)

    def tile(j, bias):
        row0 = pl.multiple_of(KPAD + TK * j, TK)
        kt = k_ref[0, 0, pl.ds(row0, TK), :]
        vt = v_ref[0, 0, pl.ds(row0, TK), :]
        c = jnp.maximum(j, 0) // (SEL_CHUNK_BLOCKS * SLC_BLOCK // TK)
        for r in range(ATT_HPG):
            rs = slice(r * TQ, (r + 1) * TQ)
            s = _dot_nt(qa_ref[c, rs, :], kt)
            if bias is not None:
                s = s + bias(r)
            m_prev = m_ref[rs, :]
            m_new = jnp.maximum(m_prev, jnp.max(s, axis=1, keepdims=True))
            alpha = jnp.exp(m_prev - m_new)
            p = jnp.exp(s - jnp.concatenate([m_new] * (TK // LANES), axis=1))
            l_ref[rs, :] = alpha * l_ref[rs, :] + jnp.sum(p, axis=1, keepdims=True)
            acc_ref[rs, :] = alpha * acc_ref[rs, :] + _dot(p.astype(BF16), vt)
            m_ref[rs, :] = m_new

    tile(m, lambda r: nb_ref[r, :, TK:2 * TK])
    tile(m - 1, lambda r: nb_ref[r, :, 0:TK])

    def far(j, carry):
        tile(j, None)
        return carry

    lax.fori_loop(0, m - 1, far, 0)
    o_ref[0] = _heads_to_lanes(acc_ref[...] / l_ref[...], TQ)


def _sel_attention(qp, selneg, ksa, vsp, nb_s):
    bn, _, t, _ = qp.shape
    tp = ksa.shape[2]
    n_slc = selneg.shape[3]
    n_chunks = max(n_slc // SEL_CHUNK_BLOCKS, 1)
    rows = ATT_HPG * TQ
    return pl.pallas_call(
        _sel_attn_body,
        grid=(bn, ATT_KV_GROUPS, t // TQ),
        in_specs=[pl.BlockSpec((1, ATT_HPG, TQ, LANES), lambda bi, g, m: (bi, g, m, 0)),
                  pl.BlockSpec((1, 1, TQ, n_slc), lambda bi, g, m: (bi, g, m, 0)),
                  pl.BlockSpec((1, 1, tp, 2 * LANES), lambda bi, g, m: (bi, g, 0, 0)),
                  pl.BlockSpec((1, 1, tp, LANES), lambda bi, g, m: (bi, g, 0, 0)),
                  pl.BlockSpec((ATT_HPG, TQ, 2 * TK), lambda bi, g, m: (g, 0, 0))],
        out_specs=pl.BlockSpec((1, TQ, ATT_HPG * HEAD_DIM), lambda bi, g, m: (bi, m, g)),
        out_shape=jax.ShapeDtypeStruct((bn, t, ATT_HEADS * HEAD_DIM), F32),
        scratch_shapes=[pltpu.VMEM((n_chunks, rows, 2 * LANES), BF16),
                        pltpu.VMEM((rows, LANES), F32),
                        pltpu.VMEM((rows, LANES), F32),
                        pltpu.VMEM((rows, LANES), F32)],
        compiler_params=_cparams(3, 56),
        name="nsa_sel_attn",
    )(qp, selneg, ksa, vsp, nb_s)


def _win_attn_body(q_ref, k_ref, v_ref, nb_ref, o_ref):
    m = pl.program_id(2)
    row0 = pl.multiple_of(m * TQ, TQ)
    kt = k_ref[0, 0, pl.ds(row0, TQ + WINDOW), :]
    vt = v_ref[0, 0, pl.ds(row0, TQ + WINDOW), :]
    outs = []
    for r in range(ATT_HPG):
        s = _dot_nt(q_ref[0, r], kt) + nb_ref[r]
        p = jnp.exp(s - jnp.max(s, axis=1, keepdims=True))
        den = jnp.sum(p, axis=1, keepdims=True)
        outs.append((_dot(p.astype(BF16), vt) / den)[:, :HEAD_DIM])
    o_ref[0] = jnp.concatenate(outs, axis=1)


def _win_attention(qp, kwp, vwp, nb_w):
    bn, _, t, _ = qp.shape
    tp = kwp.shape[2]
    return pl.pallas_call(
        _win_attn_body,
        grid=(bn, ATT_KV_GROUPS, t // TQ),
        in_specs=[pl.BlockSpec((1, ATT_HPG, TQ, LANES), lambda bi, g, m: (bi, g, m, 0)),
                  pl.BlockSpec((1, 1, tp, LANES), lambda bi, g, m: (bi, g, 0, 0)),
                  pl.BlockSpec((1, 1, tp, LANES), lambda bi, g, m: (bi, g, 0, 0)),
                  pl.BlockSpec((ATT_HPG, TQ, TQ + WINDOW), lambda bi, g, m: (g, 0, 0))],
        out_specs=pl.BlockSpec((1, TQ, ATT_HPG * HEAD_DIM), lambda bi, g, m: (bi, m, g)),
        out_shape=jax.ShapeDtypeStruct((bn, t, ATT_HEADS * HEAD_DIM), F32),
        compiler_params=_cparams(3, 48),
        name="nsa_win_attn",
    )(qp, kwp, vwp, nb_w)


GMLP_ROWS = 1024


def _gmlp_body(uv_ref, g_ref, b_ref, ws_ref, bs_ref, o_ref):
    act = _gelu_tanh(uv_ref[0])
    u = act[:, :D_GMLP]
    v = _layer_norm(act[:, D_GMLP:], g_ref[...], b_ref[...])
    ti = lax.broadcasted_iota(jnp.int32, (GMLP_CHUNK, GMLP_CHUNK), 0)
    si = lax.broadcasted_iota(jnp.int32, (GMLP_CHUNK, GMLP_CHUNK), 1)
    w = [jnp.where(ti >= si, ws_ref[g], 0.0).astype(BF16) for g in range(GMLP_GROUPS)]
    lane = lax.broadcasted_iota(jnp.int32, (1, LANES), 1)
    lo = lane < HEAD_DIM
    for c in range(uv_ref.shape[1] // GMLP_CHUNK):
        rs = slice(c * GMLP_CHUNK, (c + 1) * GMLP_CHUNK)
        pairs = []
        for pr in range(GMLP_GROUPS // 2):
            vp = v[rs, pr * LANES:(pr + 1) * LANES]
            pairs.append(_dot(w[2 * pr], jnp.where(lo, vp, 0.0).astype(BF16))
                         + _dot(w[2 * pr + 1], jnp.where(lo, 0.0, vp).astype(BF16)))
        o_ref[0, rs, :] = u[rs] * (jnp.concatenate(pairs, axis=1) + bs_ref[...])


def _gmlp(uv, ln_g, ln_b, ws, bs):
    bn, t, _ = uv.shape
    rows = min(GMLP_ROWS, t)
    bs_lanes = jnp.repeat(bs.T, D_GMLP // GMLP_GROUPS, axis=1)
    full = lambda shape: pl.BlockSpec(shape, lambda bi, i: (0,) * len(shape))
    return pl.pallas_call(
        _gmlp_body,
        grid=(bn, t // rows),
        in_specs=[pl.BlockSpec((1, rows, 2 * D_GMLP), lambda bi, i: (bi, i, 0)),
                  full((1, D_GMLP)), full((1, D_GMLP)),
                  full((GMLP_GROUPS, GMLP_CHUNK, GMLP_CHUNK)), full((GMLP_CHUNK, D_GMLP))],
        out_specs=pl.BlockSpec((1, rows, D_GMLP), lambda bi, i: (bi, i, 0)),
        out_shape=jax.ShapeDtypeStruct((bn, t, D_GMLP), F32),
        compiler_params=_cparams(2, 40),
        name="gmlp",
    )(uv, ln_g[None], ln_b[None], ws, bs_lanes)


def _ssd_body(z_ref, xbc_ref, prev_ref, dt_ref, cw_ref, cb_ref, dtb_ref, alog_ref, dsk_ref, ng_ref, e4_ref,
              o_ref, xe_ref, state_ref):
    c = pl.program_id(1)
    L = SSM_CHUNK

    @pl.when(c == 0)
    def _():
        state_ref[...] = jnp.zeros(state_ref.shape, F32)

    xe_ref[0:8] = jnp.where(c > 0, prev_ref[0], 0.0)
    xe_ref[8:8 + L] = xbc_ref[0]
    conv = cb_ref[...]
    for k in range(SSM_CONV):
        conv = conv + cw_ref[k:k + 1, :] * xe_ref[pl.ds(8 - (SSM_CONV - 1) + k, L), :]
    xc = conv * _sigmoid(conv)
    xs, bm, cm = xc[:, :D_SSM], xc[:, D_SSM:D_SSM + 2 * SSM_STATE], xc[:, D_SSM + 2 * SSM_STATE:]

    dtr = dt_ref[0] + dtb_ref[...]
    dt = jnp.maximum(dtr, 0.0) + jnp.log1p(jnp.exp(-jnp.abs(dtr)))
    a = -jnp.exp(alog_ref[...]) * dt
    ti = lax.broadcasted_iota(jnp.int32, (L, L), 0)
    si = lax.broadcasted_iota(jnp.int32, (L, L), 1)
    causal = ti >= si
    cs = _dot_hi(jnp.where(causal, 1.0, 0.0), a)
    cs_t = cs.T
    e4 = e4_ref[...]
    dt_e = _dot_hi(dt, e4)
    cs_e = _dot_hi(cs, e4)
    last_e = cs_e[L - 1:L, :]
    x_dt = xs * dt_e
    x_dec = (x_dt * jnp.exp(last_e - cs_e)).astype(BF16)
    x_dt = x_dt.astype(BF16)
    grow = jnp.exp(cs_e)
    lane = lax.broadcasted_iota(jnp.int32, (1, LANES), 1)
    lo = lane < SSM_HEAD_DIM
    zero = jnp.zeros((), BF16)

    ys = []
    for g in range(2):
        gs = slice(g * LANES, (g + 1) * LANES)
        cg = cm[:, gs].astype(BF16)
        bg = bm[:, gs]
        scores = _dot_nt(cg, bg.astype(BF16))
        xp = x_dt[:, gs]
        y = None
        for hh in range(2):
            h = 2 * g + hh
            seg = jnp.broadcast_to(cs[:, h:h + 1], (L, L)) - jnp.broadcast_to(cs_t[h:h + 1, :], (L, L))
            w = (scores * jnp.where(causal, jnp.exp(seg), 0.0)).astype(BF16)
            xh = jnp.where(lo, xp, zero) if hh == 0 else jnp.where(lo, zero, xp)
            yh = _dot(w, xh)
            y = yh if y is None else y + yh
        state = state_ref[g]
        y = y + _dot(cg, state.astype(BF16)) * grow[:, gs]
        state_ref[g] = jnp.exp(last_e[:, gs]) * state + _dot(bg.T.astype(BF16), x_dec[:, gs])
        ys.append(y)
    y = jnp.concatenate(ys, axis=1) + xs * dsk_ref[...]
    zz = z_ref[0]
    gg = y * (zz * _sigmoid(zz))
    outs = []
    for g in range(2):
        gp = gg[:, g * LANES:(g + 1) * LANES]
        outs.append(gp * lax.rsqrt(jnp.mean(gp * gp, axis=-1, keepdims=True) + LN_EPS))
    o_ref[0] = jnp.concatenate(outs, axis=1) * ng_ref[...]


def _head_lanes(v):
    return jnp.repeat(v, SSM_HEAD_DIM)[None]


def _ssd(z, xbc, dt, conv_w, conv_b, dt_bias, a_log, d_skip, norm_g):
    bn, t, _ = z.shape
    L = SSM_CHUNK
    first_lanes = lambda v: jnp.zeros((1, LANES), F32).at[0, :SSM_HEADS].set(v)
    e4 = np.zeros((LANES, D_SSM), np.float32)
    for h in range(SSM_HEADS):
        e4[h, h * SSM_HEAD_DIM:(h + 1) * SSM_HEAD_DIM] = 1.0
    full = lambda shape: pl.BlockSpec(shape, lambda bi, c: (0,) * len(shape))
    return pl.pallas_call(
        _ssd_body,
        grid=(bn, t // L),
        in_specs=[pl.BlockSpec((1, L, D_SSM), lambda bi, c: (bi, c, 0)),
                  pl.BlockSpec((1, L, D_XBC), lambda bi, c: (bi, c, 0)),
                  pl.BlockSpec((1, 8, D_XBC), lambda bi, c: (bi, jnp.maximum(c * (L // 8) - 1, 0), 0)),
                  pl.BlockSpec((1, L, LANES), lambda bi, c: (bi, c, 0)),
                  full((SSM_CONV, D_XBC)), full((1, D_XBC)), full((1, LANES)), full((1, LANES)),
                  full((1, D_SSM)), full((1, D_SSM)), full((LANES, D_SSM))],
        out_specs=pl.BlockSpec((1, L, D_SSM), lambda bi, c: (bi, c, 0)),
        out_shape=jax.ShapeDtypeStruct((bn, t, D_SSM), F32),
        scratch_shapes=[pltpu.VMEM((L + 8, D_XBC), F32),
                        pltpu.VMEM((2, SSM_STATE, LANES), F32)],
        compiler_params=_cparams(2, 40),
        name="ssd",
    )(z, xbc, xbc, dt, conv_w, conv_b[None], first_lanes(dt_bias), first_lanes(a_log),
      _head_lanes(d_skip), norm_g[None], jnp.asarray(e4))


def _outproj_body(x_ref, mod_ref, oc_ref, os_ref, ow_ref, gt_ref, e_ref, gm_ref, ssm_ref, w_ref, g_ref, b_ref,
                  o_ref):
    x = x_ref[0]
    m = mod_ref[0]
    gate = m[5:6]
    d_att = ATT_HEADS * HEAD_DIM
    ge = _dot_hi(gt_ref[0], e_ref[...])
    o_att = (ge[:, 0:d_att] * oc_ref[0] + ge[:, d_att:2 * d_att] * os_ref[0]
             + ge[:, 2 * d_att:3 * d_att] * ow_ref[0])
    y = (_dot(o_att.astype(BF16), w_ref[0:d_att, :])
         + _dot(gm_ref[0].astype(BF16), w_ref[d_att:d_att + D_GMLP, :])
         + _dot(ssm_ref[0].astype(BF16), w_ref[d_att + D_GMLP:, :]))
    z = ALPHA * x + (1.0 + gate) * y
    o_ref[0] = _layer_norm(z, g_ref[...], b_ref[...])


def _mixer_outproj(x, mod_l, o_c, o_s, o_w, gates, o_gmlp, o_ssm, w_out, g, b):
    bn, t, d = x.shape
    d_att = ATT_HEADS * HEAD_DIM
    e = np.zeros((LANES, 3 * d_att), np.float32)
    for h in range(ATT_HEADS):
        for br in range(3):
            e[3 * h + br, br * d_att + h * HEAD_DIM:br * d_att + (h + 1) * HEAD_DIM] = 1.0
    row = lambda width: pl.BlockSpec((1, TM, width), lambda bi, i: (bi, i, 0))
    full = lambda shape: pl.BlockSpec(shape, lambda bi, i: (0,) * len(shape))
    return pl.pallas_call(
        _outproj_body,
        grid=(bn, t // TM),
        in_specs=[row(d), pl.BlockSpec((1, N_SUB * 3, d), lambda bi, i: (bi, 0, 0)),
                  row(d_att), row(d_att), row(d_att), row(LANES), full((LANES, 3 * d_att)),
                  row(D_GMLP), row(D_SSM), full((d, d)), full((1, d)), full((1, d))],
        out_specs=row(d),
        out_shape=jax.ShapeDtypeStruct(x.shape, F32),
        compiler_params=_cparams(2, 48),
        name="mixer_outproj",
    )(x, mod_l, o_c, o_s, o_w, gates, jnp.asarray(e), o_gmlp, o_ssm, w_out.astype(BF16), g[None], b[None])


def _nsa(qp, kcv, ksa, vsp, kwp, vwp, tables, ov, cmp_pe, cmp_w1, cmp_b1, cmp_w2):
    nb_c, nb_s, nb_w = tables
    kcp = _compress(kcv, cmp_pe, cmp_w1, cmp_b1, cmp_w2)
    o_c, imp = _cmp_attention(qp, kcp, nb_c, ov)
    selneg = _select_blocks(imp)
    o_s = _sel_attention(qp, selneg, ksa, vsp, nb_s)
    o_w = _win_attention(qp, kwp, vwp, nb_w)
    return o_c, o_s, o_w


def kernel(x, c, rel_bias, ada_w, ada_b, ln_g, ln_b, ffn_w1, ffn_w3, ffn_w2, w_in, w_out, cmp_pe, cmp_w1, cmp_b1,
           cmp_w2, gmlp_ln_g, gmlp_ln_b, gmlp_ws, gmlp_bs, ssm_conv_w, ssm_conv_b, ssm_dt_bias, ssm_a_log, ssm_d,
           ssm_norm_g):
    t = x.shape[1]
    mod = _adaln_mod(c, ada_w, ada_b)
    tables = _bias_tables(rel_bias)
    ov = _overlap_matrix(t)
    for l in range(ada_w.shape[0]):
        x = _ffn_sublayer(x, mod[l], ffn_w1[l, 0], ffn_w3[l, 0], ffn_w2[l, 0], ln_g[l, 0], ln_b[l, 0], 0)
        qp, kcv, ksa, vsp, kwp, vwp, gates, gm_in, z, xbc, dt = _mixer_inproj(x, mod[l], _relayout_w_in(w_in[l]))
        o_c, o_s, o_w = _nsa(qp, kcv, ksa, vsp, kwp, vwp, tables, ov, cmp_pe[l], cmp_w1[l], cmp_b1[l], cmp_w2[l])
        o_gmlp = _gmlp(gm_in, gmlp_ln_g[l], gmlp_ln_b[l], gmlp_ws[l], gmlp_bs[l])
        o_ssm = _ssd(z, xbc, dt, ssm_conv_w[l], ssm_conv_b[l], ssm_dt_bias[l], ssm_a_log[l], ssm_d[l],
                     ssm_norm_g[l])
        x = _mixer_outproj(x, mod[l], o_c, o_s, o_w, gates, o_gmlp, o_ssm, w_out[l], ln_g[l, 1], ln_b[l, 1])
        x = _ffn_sublayer(x, mod[l], ffn_w1[l, 1], ffn_w3[l, 1], ffn_w2[l, 1], ln_g[l, 2], ln_b[l, 2], 2)
    return x
```

```python
import functools
import math

import numpy as np
import jax
import jax.numpy as jnp
from jax import lax
from jax.experimental import pallas as pl
from jax.experimental.pallas import tpu as pltpu

F32 = jnp.float32
BF16 = jnp.bfloat16
HI = lax.Precision.HIGHEST

D_MODEL = 1024
DEPTH = 2
ATT_HEADS = 8
ATT_KV_GROUPS = 2
ATT_HPG = 4
HEAD_DIM = 64
CMP_LEN = 32
CMP_STRIDE = 16
CMP_HIDDEN = 256
SLC_BLOCK = 64
SLC_TOPN = 16
WINDOW = 512
SEL_FORCE = 1.0e4
REL_BUCKETS = 32
REL_MAX_DIST = 128
GMLP_GROUPS = 4
GMLP_CHUNK = 128
D_GMLP = 256
SSM_HEADS = 4
SSM_HEAD_DIM = 64
D_SSM = 256
SSM_STATE = 128
SSM_CONV = 4
SSM_CHUNK = 256
D_XBC = 768
D_FF = 2816
N_SUB = 3
ALPHA = (2 * DEPTH) ** 0.25
LN_EPS = 1e-5

LANES = 128
LOG2E = math.log2(math.e)
NEG_BIG = -(2.0 ** 30)
KPAD = 512
CMP_KPAD = 16
TQ_C = 256
TQ = 256
TK = 256
SEL_FAR_TILES = 4
TM = 512
FF_CHUNK = 256
SEL_CHUNK_BLOCKS = 128

COL_Q = 0
COL_KV = 1024
COL_GATE = 2560
COL_GMLP = 2688
COL_Z = 3200
COL_XBC = 3456
COL_DT = 4224
N_COLS = 4352


def _cparams(n_axes, vmem_mb):
    return pltpu.CompilerParams(dimension_semantics=("arbitrary",) * n_axes,
                                vmem_limit_bytes=vmem_mb * 1024 * 1024)


def _sigmoid(x):
    return 1.0 / (1.0 + jnp.exp(-x))


def _gelu_tanh(x):
    return x * (0.5 * (1.0 + jnp.tanh(math.sqrt(2.0 / math.pi) * (x + 0.044715 * (x * x * x)))))


def _layer_norm(z, g, b):
    mu = jnp.mean(z, axis=-1, keepdims=True)
    d = z - mu
    var = jnp.mean(d * d, axis=-1, keepdims=True)
    return d * lax.rsqrt(var + LN_EPS) * g + b


def _dot(a, b):
    return jnp.dot(a, b, preferred_element_type=F32)


def _dot_hi(a, b):
    return jnp.dot(a, b, preferred_element_type=F32, precision=HI)


def _dot_nt(a, b):
    return lax.dot_general(a, b, (((1,), (1,)), ((), ())), preferred_element_type=F32)


def _mod_body(c_ref, w_ref, b_ref, o_ref):
    c = c_ref[...]
    o_ref[0] = _dot_hi(c * _sigmoid(c), w_ref[0]) + b_ref[0]


def _adaln_mod(c, ada_w, ada_b):
    n_layers, d, n = ada_w.shape
    bn = c.shape[0]
    tn = 1536
    cp = jnp.zeros((8, d), F32).at[:bn].set(c)
    out = pl.pallas_call(
        _mod_body,
        grid=(n_layers, n // tn),
        in_specs=[pl.BlockSpec((8, d), lambda l, j: (0, 0)),
                  pl.BlockSpec((1, d, tn), lambda l, j: (l, 0, j)),
                  pl.BlockSpec((1, 1, tn), lambda l, j: (l, 0, j))],
        out_specs=pl.BlockSpec((1, 8, tn), lambda l, j: (l, 0, j)),
        out_shape=jax.ShapeDtypeStruct((n_layers, 8, n), F32),
        compiler_params=_cparams(2, 32),
        name="adaln_mod",
    )(cp, ada_w, ada_b.reshape(n_layers, 1, n))
    return out[:, :bn].reshape(n_layers, bn, N_SUB * 3, d)


def _ffn_body(x_ref, mod_ref, w1_ref, w3_ref, w2_ref, g_ref, b_ref, o_ref, *, sub):
    x = x_ref[0]
    m = mod_ref[0]
    shift, scale, gate = m[3 * sub:3 * sub + 1], m[3 * sub + 1:3 * sub + 2], m[3 * sub + 2:3 * sub + 3]
    h = (x * (1.0 + scale) + shift).astype(BF16)
    acc = jnp.zeros(x.shape, F32)
    for j in range(D_FF // FF_CHUNK):
        sl = slice(j * FF_CHUNK, (j + 1) * FF_CHUNK)
        a = _dot(h, w1_ref[:, sl])
        b = _dot(h, w3_ref[:, sl])
        acc = acc + _dot((a * _sigmoid(a) * b).astype(BF16), w2_ref[sl, :])
    z = ALPHA * x + (0.5 * (1.0 + gate)) * acc
    o_ref[0] = _layer_norm(z, g_ref[...], b_ref[...])


def _ffn_sublayer(x, mod_l, w1, w3, w2, g, b, sub):
    bn, t, d = x.shape
    full = lambda shape: pl.BlockSpec(shape, lambda bi, i: (0,) * len(shape))
    return pl.pallas_call(
        functools.partial(_ffn_body, sub=sub),
        grid=(bn, t // TM),
        in_specs=[pl.BlockSpec((1, TM, d), lambda bi, i: (bi, i, 0)),
                  pl.BlockSpec((1, N_SUB * 3, d), lambda bi, i: (bi, 0, 0)),
                  full((d, D_FF)), full((d, D_FF)), full((D_FF, d)),
                  full((1, d)), full((1, d))],
        out_specs=pl.BlockSpec((1, TM, d), lambda bi, i: (bi, i, 0)),
        out_shape=jax.ShapeDtypeStruct(x.shape, F32),
        compiler_params=_cparams(2, 56),
        name="ffn",
    )(x, mod_l, w1.astype(BF16), w3.astype(BF16), w2.astype(BF16), g[None], b[None])


def _pad_cols(w, width):
    return jnp.pad(w, ((0, 0), (0, width - w.shape[1])))


def _relayout_w_in(w_in):
    cols = [_pad_cols(w_in[:, HEAD_DIM * h:HEAD_DIM * (h + 1)], LANES) for h in range(ATT_HEADS)]
    for i in range(6):
        for g in range(ATT_KV_GROUPS):
            base = 512 + 128 * i + HEAD_DIM * g
            cols.append(_pad_cols(w_in[:, base:base + HEAD_DIM], LANES))
    cols.append(_pad_cols(w_in[:, 1280:1304], LANES))
    cols.append(w_in[:, 1304:1816])
    cols.append(w_in[:, 1816:2072])
    cols.append(w_in[:, 2072:2840])
    cols.append(_pad_cols(w_in[:, 2840:2844], LANES))
    return jnp.concatenate(cols, axis=1).astype(BF16)


def _inproj_body(x_ref, mod_ref, w_ref, qp_ref, kcv_ref, ksa_ref, vst_ref, kwp_ref, vwp_ref,
                 gt_ref, gm_ref, z_ref, xbc_ref, dt_ref):
    i = pl.program_id(1)
    x = x_ref[0]
    m = mod_ref[0]
    h = (x * (1.0 + m[4:5]) + m[3:4]).astype(BF16)

    def proj(c0, width):
        return _dot(h, w_ref[:, c0:c0 + width])

    lane = lax.broadcasted_iota(jnp.int32, (1, LANES), 1)
    q_flag = jnp.where(lane == HEAD_DIM, NEG_BIG, 0.0)
    for hh in range(ATT_HEADS):
        qp_ref[0, hh] = (proj(COL_Q + LANES * hh, LANES) * (HEAD_DIM ** -0.5 * LOG2E) + q_flag).astype(BF16)
    for k in range(4):
        kcv_ref[0, k] = proj(COL_KV + LANES * k, LANES).astype(BF16)
    gt_ref[0] = _sigmoid(proj(COL_GATE, LANES))
    gm_ref[0] = proj(COL_GMLP, 512)
    z_ref[0] = proj(COL_Z, D_SSM)
    xbc_ref[0] = proj(COL_XBC, D_XBC)
    dt_ref[0] = proj(COL_DT, LANES)

    @pl.when(i == 0)
    def _():
        dummy = jnp.broadcast_to(jnp.where(lane == HEAD_DIM, 1.0, 0.0), (TM, LANES)).astype(BF16)
        zeros = jnp.zeros((TM, LANES), BF16)
        for g in range(ATT_KV_GROUPS):
            ksa_ref[0, g, :, 0:LANES] = dummy
            ksa_ref[0, g, :, LANES:2 * LANES] = zeros
            kwp_ref[0, g] = dummy
            vst_ref[0, g] = jnp.zeros((LANES, TM), BF16)
            vwp_ref[0, g] = zeros

    @pl.when(i > 0)
    def _():
        pos = (i - 1) * TM + lax.broadcasted_iota(jnp.int32, (TM, LANES), 0)
        blk_lane = (pos >> 6) & (SEL_CHUNK_BLOCKS - 1)
        onehot = jnp.where(blk_lane == lax.broadcasted_iota(jnp.int32, (TM, LANES), 1), 1.0, 0.0).astype(BF16)
        ones_row = lax.broadcasted_iota(jnp.int32, (LANES, 1), 0) == HEAD_DIM
        for g in range(ATT_KV_GROUPS):
            ksa_ref[0, g, :, 0:LANES] = proj(COL_KV + LANES * (4 + g), LANES).astype(BF16)
            ksa_ref[0, g, :, LANES:2 * LANES] = onehot
            vst_ref[0, g] = jnp.where(ones_row, 1.0, proj(COL_KV + LANES * (6 + g), LANES).T).astype(BF16)
            kwp_ref[0, g] = proj(COL_KV + LANES * (8 + g), LANES).astype(BF16)
            vwp_ref[0, g] = proj(COL_KV + LANES * (10 + g), LANES).astype(BF16)


def _mixer_inproj(x, mod_l, w_in):
    bn, t, d = x.shape
    tp = t + KPAD
    row = lambda width: pl.BlockSpec((1, TM, width), lambda bi, i: (bi, jnp.maximum(i - 1, 0), 0))
    sds = jax.ShapeDtypeStruct
    return pl.pallas_call(
        _inproj_body,
        grid=(bn, t // TM + 1),
        in_specs=[row(d),
                  pl.BlockSpec((1, N_SUB * 3, d), lambda bi, i: (bi, 0, 0)),
                  pl.BlockSpec((d, N_COLS), lambda bi, i: (0, 0))],
        out_specs=[pl.BlockSpec((1, ATT_HEADS, TM, LANES), lambda bi, i: (bi, 0, jnp.maximum(i - 1, 0), 0)),
                   pl.BlockSpec((1, 4, TM, LANES), lambda bi, i: (bi, 0, jnp.maximum(i - 1, 0), 0)),
                   pl.BlockSpec((1, ATT_KV_GROUPS, TM, 2 * LANES), lambda bi, i: (bi, 0, i, 0)),
                   pl.BlockSpec((1, ATT_KV_GROUPS, LANES, TM), lambda bi, i: (bi, 0, 0, i)),
                   pl.BlockSpec((1, ATT_KV_GROUPS, TM, LANES), lambda bi, i: (bi, 0, i, 0)),
                   pl.BlockSpec((1, ATT_KV_GROUPS, TM, LANES), lambda bi, i: (bi, 0, i, 0)),
                   row(LANES), row(512), row(D_SSM), row(D_XBC), row(LANES)],
        out_shape=[sds((bn, ATT_HEADS, t, LANES), BF16),
                   sds((bn, 4, t, LANES), BF16),
                   sds((bn, ATT_KV_GROUPS, tp, 2 * LANES), BF16),
                   sds((bn, ATT_KV_GROUPS, LANES, tp), BF16),
                   sds((bn, ATT_KV_GROUPS, tp, LANES), BF16),
                   sds((bn, ATT_KV_GROUPS, tp, LANES), BF16),
                   sds((bn, t, LANES), F32), sds((bn, t, 512), F32), sds((bn, t, D_SSM), F32),
                   sds((bn, t, D_XBC), F32), sds((bn, t, LANES), F32)],
        compiler_params=_cparams(2, 48),
        name="mixer_inproj",
    )(x, mod_l, w_in)


def _cmp_rows(t):
    return -(-(t // CMP_STRIDE + 120) // LANES) * LANES


def _compress_body(c_ref, w1_ref, pe_ref, b1_ref, w2_ref, o_ref):
    c = c_ref[0, 0]
    n_rows = c.shape[0]
    half = CMP_STRIDE * HEAD_DIM
    w1 = w1_ref[0]
    h1 = _dot(c, w1[:half])
    h2 = _dot(c, w1[half:])
    cvec = _dot(pe_ref[0], w1)[0:1] + b1_ref[0]
    hid = h1 + pltpu.roll(h2, n_rows - 1, 0) + cvec
    out = _dot(_gelu_tanh(hid).astype(BF16), w2_ref[0]).astype(BF16)
    o_ref[0, 0, 0:CMP_KPAD] = jnp.zeros((CMP_KPAD, LANES), BF16)
    o_ref[0, 0, CMP_KPAD:CMP_KPAD + n_rows] = out
    tail = o_ref.shape[2] - CMP_KPAD - n_rows
    o_ref[0, 0, CMP_KPAD + n_rows:] = jnp.zeros((tail, LANES), BF16)


def _compress(kcv, cmp_pe, cmp_w1, cmp_b1, cmp_w2):
    bn, _, t, _ = kcv.shape
    n_rows = t // CMP_STRIDE
    c = kcv[..., :HEAD_DIM].reshape(bn, 4, n_rows, CMP_STRIDE * HEAD_DIM)
    pe = jnp.zeros((2, 8, CMP_LEN * HEAD_DIM), F32).at[:, 0].set(cmp_pe.reshape(2, -1)).astype(BF16)
    w2 = jnp.pad(cmp_w2, ((0, 0), (0, 0), (0, LANES - HEAD_DIM))).astype(BF16)
    rows = _cmp_rows(t)
    return pl.pallas_call(
        _compress_body,
        grid=(bn, 4),
        in_specs=[pl.BlockSpec((1, 1, n_rows, CMP_STRIDE * HEAD_DIM), lambda bi, k: (bi, k, 0, 0)),
                  pl.BlockSpec((1, CMP_LEN * HEAD_DIM, CMP_HIDDEN), lambda bi, k: (k // 2, 0, 0)),
                  pl.BlockSpec((1, 8, CMP_LEN * HEAD_DIM), lambda bi, k: (k // 2, 0, 0)),
                  pl.BlockSpec((1, 1, CMP_HIDDEN), lambda bi, k: (k // 2, 0, 0)),
                  pl.BlockSpec((1, CMP_HIDDEN, LANES), lambda bi, k: (k // 2, 0, 0))],
        out_specs=pl.BlockSpec((1, 1, rows, LANES), lambda bi, k: (bi, k, 0, 0)),
        out_shape=jax.ShapeDtypeStruct((bn, 4, rows, LANES), BF16),
        compiler_params=_cparams(2, 40),
        name="nsa_compress",
    )(c, cmp_w1.astype(BF16), pe, cmp_b1[:, None, :], w2)


def _rel_bucket(dist):
    n = jnp.maximum(dist, 0)
    max_exact = REL_BUCKETS // 2
    nf = jnp.maximum(n, 1).astype(F32)
    large = max_exact + (jnp.log(nf / max_exact) / math.log(REL_MAX_DIST / max_exact)
                         * (REL_BUCKETS - max_exact)).astype(jnp.int32)
    large = jnp.minimum(large, REL_BUCKETS - 1)
    return jnp.where(n < max_exact, n, large)


def _bias_tables(rel_bias):
    far = rel_bias[REL_BUCKETS - 1]

    def table(dist, valid, shift):
        onehot = (_rel_bucket(dist)[..., None] == jnp.arange(REL_BUCKETS)).astype(F32)
        b = jnp.einsum("qkb,bh->hqk", onehot, rel_bias, precision=HI)
        if shift:
            b = b - far[:, None, None]
        return jnp.where(valid[None], b * LOG2E, -jnp.inf)

    qi = np.arange(TQ_C)[:, None]
    d_c = jnp.asarray(qi - CMP_STRIDE * np.arange(LANES)[None, :] + (CMP_STRIDE * CMP_KPAD - CMP_LEN + 1))
    nb_c = table(d_c, d_c >= 0, True)
    ki = np.arange(2 * TK)[:, None]
    d_s = jnp.asarray(np.arange(TQ)[None, :] - ki + TK)
    nb_s = table(d_s, d_s >= 0, True)
    qi = np.arange(TQ)[:, None]
    d_w = jnp.asarray(qi - np.arange(TQ + WINDOW)[None, :] + WINDOW)
    nb_w = table(d_w, (d_w >= 0) & (d_w < WINDOW), False)
    return nb_c, nb_s, nb_w


def _overlap_matrix(t):
    n_cmp = (t - CMP_LEN) // CMP_STRIDE + 1
    n_slc = t // SLC_BLOCK
    cs = np.arange(n_cmp) * CMP_STRIDE
    ss = np.arange(n_slc) * SLC_BLOCK
    ov = np.clip(np.minimum(cs[:, None] + CMP_LEN, ss[None, :] + SLC_BLOCK)
                 - np.maximum(cs[:, None], ss[None, :]), 0, None) / CMP_LEN
    full = np.zeros((_cmp_rows(t), n_slc), np.float32)
    full[CMP_KPAD:CMP_KPAD + n_cmp] = ov
    return jnp.asarray(full)


def _heads_to_lanes(o, rows):
    return jnp.concatenate([o[i * rows:(i + 1) * rows, :HEAD_DIM] for i in range(ATT_HPG)], axis=1)


def _cmp_attn_body(q_ref, kc_ref, vc_ref, nb_ref, ov_ref, o_ref, imp_ref):
    m = pl.program_id(2)
    q = q_ref[0].reshape(ATT_HPG * TQ_C, LANES)
    kf = kc_ref[0, 0]
    vf = vc_ref[0, 0]
    n_keys = kf.shape[0]
    near0 = pl.multiple_of(m * (TQ_C // CMP_STRIDE), 16)
    col = lax.broadcasted_iota(jnp.int32, (1, n_keys), 1)
    sf = jnp.where((col >= CMP_KPAD) & (col < near0), _dot_nt(q, kf), -jnp.inf)
    kn = kc_ref[0, 0, pl.ds(near0, LANES), :]
    vn = vc_ref[0, 0, pl.ds(near0, LANES), :]
    coln = lax.broadcasted_iota(jnp.int32, (1, LANES), 1) + near0
    sn = jnp.where(coln >= CMP_KPAD, _dot_nt(q, kn) + nb_ref[...].reshape(ATT_HPG * TQ_C, LANES), -jnp.inf)
    mx = jnp.maximum(jnp.max(sf, axis=1, keepdims=True), jnp.max(sn, axis=1, keepdims=True))
    mx = jnp.where(mx == -jnp.inf, 0.0, mx)
    pf = jnp.exp2(sf - mx)
    pn = jnp.exp2(sn - mx)
    den = jnp.sum(pf, axis=1, keepdims=True) + jnp.sum(pn, axis=1, keepdims=True)
    inv = 1.0 / jnp.maximum(den, 1e-30)
    pf = pf * inv
    pn = pn * inv
    o = _dot(pf.astype(BF16), vf) + _dot(pn.astype(BF16), vn)
    o_ref[0] = _heads_to_lanes(o, TQ_C)
    pf_g = pf[0:TQ_C] + pf[TQ_C:2 * TQ_C] + pf[2 * TQ_C:3 * TQ_C] + pf[3 * TQ_C:4 * TQ_C]
    pn_g = pn[0:TQ_C] + pn[TQ_C:2 * TQ_C] + pn[2 * TQ_C:3 * TQ_C] + pn[3 * TQ_C:4 * TQ_C]
    imp_ref[0, 0] = _dot_hi(pf_g, ov_ref[...]) + _dot_hi(pn_g, ov_ref[pl.ds(near0, LANES), :])


def _cmp_attention(qp, kcp, nb_c, ov):
    bn, _, t, _ = qp.shape
    rows = kcp.shape[2]
    n_slc = ov.shape[1]
    return pl.pallas_call(
        _cmp_attn_body,
        grid=(bn, ATT_KV_GROUPS, t // TQ_C),
        in_specs=[pl.BlockSpec((1, ATT_HPG, TQ_C, LANES), lambda bi, g, m: (bi, g, m, 0)),
                  pl.BlockSpec((1, 1, rows, LANES), lambda bi, g, m: (bi, g, 0, 0)),
                  pl.BlockSpec((1, 1, rows, LANES), lambda bi, g, m: (bi, 2 + g, 0, 0)),
                  pl.BlockSpec((ATT_HPG, TQ_C, LANES), lambda bi, g, m: (g, 0, 0)),
                  pl.BlockSpec((rows, n_slc), lambda bi, g, m: (0, 0))],
        out_specs=[pl.BlockSpec((1, TQ_C, ATT_HPG * HEAD_DIM), lambda bi, g, m: (bi, m, g)),
                   pl.BlockSpec((1, 1, TQ_C, n_slc), lambda bi, g, m: (bi, g, m, 0))],
        out_shape=[jax.ShapeDtypeStruct((bn, t, ATT_HEADS * HEAD_DIM), F32),
                   jax.ShapeDtypeStruct((bn, ATT_KV_GROUPS, t, n_slc), F32)],
        compiler_params=_cparams(3, 40),
        name="nsa_cmp_attn",
    )(qp, kcp, kcp, nb_c, ov)


TOPK_ROWS = 1024


def _topk_body(imp_ref, o_ref):
    i = pl.program_id(2)
    imp = imp_ref[0, 0]
    rows, n_slc = imp.shape
    t = i * rows + lax.broadcasted_iota(jnp.int32, (rows, 1), 0)
    cur = t >> 6
    blk = lax.broadcasted_iota(jnp.int32, (rows, n_slc), 1)
    forced = (blk == 0) | (blk == cur) | (blk == cur - 1)
    vals = jnp.where(forced, SEL_FORCE, jnp.where(blk <= cur, imp, -SEL_FORCE))
    blkf = blk.astype(F32)
    mask = jnp.full((rows, n_slc), NEG_BIG, F32)
    for _ in range(min(SLC_TOPN, n_slc)):
        mx = jnp.max(vals, axis=1, keepdims=True)
        first = jnp.min(jnp.where(vals == mx, blkf, float(n_slc)), axis=1, keepdims=True)
        hit = blkf == first
        mask = jnp.where(hit, 0.0, mask)
        vals = jnp.where(hit, -jnp.inf, vals)
    o_ref[0, 0] = mask.astype(BF16)


def _select_blocks(imp):
    bn, g, t, n_slc = imp.shape
    rows = min(TOPK_ROWS, t)
    spec = pl.BlockSpec((1, 1, rows, n_slc), lambda bi, gi, i: (bi, gi, i, 0))
    return pl.pallas_call(
        _topk_body,
        grid=(bn, g, t // rows),
        in_specs=[spec],
        out_specs=spec,
        out_shape=jax.ShapeDtypeStruct(imp.shape, BF16),
        compiler_params=_cparams(3, 40),
        name="nsa_topk",
    )(imp)


def _sel_attn_body(q_ref, sel_ref, k_ref, vt_ref, nb_ref, o_ref, qat_ref, m_ref, acc_ref):
    m = pl.program_id(2)
    n_chunks = qat_ref.shape[0]
    for r in range(ATT_HPG):
        q_t = q_ref[0, r].astype(F32).T.astype(BF16)
        for c in range(n_chunks):
            qat_ref[c, r, 0:LANES, :] = q_t
    for c in range(n_chunks):
        sel_t = sel_ref[0, 0, :, c * LANES:(c + 1) * LANES].astype(F32).T.astype(BF16)
        for r in range(ATT_HPG):
            qat_ref[c, r, LANES:2 * LANES, :] = sel_t
    m_ref[...] = jnp.full(m_ref.shape, -jnp.inf, F32)
    acc_ref[...] = jnp.zeros(acc_ref.shape, F32)

    tiles_per_chunk = SEL_CHUNK_BLOCKS * SLC_BLOCK // TK

    def tile(j, c, bias, n_tiles=1):
        row0 = pl.multiple_of(KPAD + TK * j, TK)
        kt = k_ref[0, 0, pl.ds(row0, n_tiles * TK), :]
        vt = vt_ref[0, 0, :, pl.ds(row0, n_tiles * TK)]
        ahead = 2
        scores = [_dot(kt, qat_ref[c, r]) for r in range(ahead)]
        for r in range(ATT_HPG):
            if r + ahead < ATT_HPG:
                scores.append(_dot(kt, qat_ref[c, r + ahead]))
            s = scores[r]
            if bias is not None:
                s = s + bias(r)
            m_prev = m_ref[r]
            m_new = jnp.maximum(m_prev, jnp.max(s, axis=0, keepdims=True))
            alpha = jnp.exp2(m_prev - m_new)
            p = jnp.exp2(s - m_new).astype(BF16)
            acc_ref[r] = alpha * acc_ref[r] + _dot(vt, p)
            m_ref[r] = m_new

    def near_tile(j, bias):
        for c in range(n_chunks):
            @pl.when(jnp.maximum(j, 0) // tiles_per_chunk == c)
            def _():
                tile(j, c, bias)

    near_tile(m, lambda r: nb_ref[r, TK:2 * TK, :])
    near_tile(m - 1, lambda r: nb_ref[r, 0:TK, :])

    for c in range(n_chunks):
        first = c * tiles_per_chunk
        n_far = jnp.clip(m - 1 - first, 0, tiles_per_chunk)
        n_big = n_far // SEL_FAR_TILES

        def far_big(i, carry, c=c, first=first):
            tile(first + SEL_FAR_TILES * i, c, None, SEL_FAR_TILES)
            return carry

        def far_one(j, carry, c=c):
            tile(j, c, None)
            return carry

        lax.fori_loop(0, n_big, far_big, 0)
        lax.fori_loop(first + SEL_FAR_TILES * n_big, first + n_far, far_one, 0)
    outs = []
    for r in range(ATT_HPG):
        acc = acc_ref[r]
        outs.append((acc / acc[HEAD_DIM:HEAD_DIM + 1, :]).T[:, :HEAD_DIM])
    o_ref[0] = jnp.concatenate(outs, axis=1)


def _sel_attention(qp, selneg, ksa, vst, nb_s):
    bn, _, t, _ = qp.shape
    tp = ksa.shape[2]
    n_slc = selneg.shape[3]
    n_chunks = max(n_slc // SEL_CHUNK_BLOCKS, 1)
    return pl.pallas_call(
        _sel_attn_body,
        grid=(bn, ATT_KV_GROUPS, t // TQ),
        in_specs=[pl.BlockSpec((1, ATT_HPG, TQ, LANES), lambda bi, g, m: (bi, g, m, 0)),
                  pl.BlockSpec((1, 1, TQ, n_slc), lambda bi, g, m: (bi, g, m, 0)),
                  pl.BlockSpec((1, 1, tp, 2 * LANES), lambda bi, g, m: (bi, g, 0, 0)),
                  pl.BlockSpec((1, 1, LANES, tp), lambda bi, g, m: (bi, g, 0, 0)),
                  pl.BlockSpec((ATT_HPG, 2 * TK, TQ), lambda bi, g, m: (g, 0, 0))],
        out_specs=pl.BlockSpec((1, TQ, ATT_HPG * HEAD_DIM), lambda bi, g, m: (bi, m, g)),
        out_shape=jax.ShapeDtypeStruct((bn, t, ATT_HEADS * HEAD_DIM), F32),
        scratch_shapes=[pltpu.VMEM((n_chunks, ATT_HPG, 2 * LANES, TQ), BF16),
                        pltpu.VMEM((ATT_HPG, 1, TQ), F32),
                        pltpu.VMEM((ATT_HPG, LANES, TQ), F32)],
        compiler_params=_cparams(3, 56),
        name="nsa_sel_attn",
    )(qp, selneg, ksa, vst, nb_s)


def _win_attn_body(q_ref, k_ref, v_ref, nb_ref, o_ref):
    m = pl.program_id(2)
    row0 = pl.multiple_of(m * TQ, TQ)
    kt = k_ref[0, 0, pl.ds(row0, TQ + WINDOW), :]
    vt = v_ref[0, 0, pl.ds(row0, TQ + WINDOW), :]
    outs = []
    for r in range(ATT_HPG):
        s = _dot_nt(q_ref[0, r], kt) + nb_ref[r]
        p = jnp.exp2(s - jnp.max(s, axis=1, keepdims=True))
        den = jnp.sum(p, axis=1, keepdims=True)
        outs.append((_dot(p.astype(BF16), vt) / den)[:, :HEAD_DIM])
    o_ref[0] = jnp.concatenate(outs, axis=1)


def _win_attention(qp, kwp, vwp, nb_w):
    bn, _, t, _ = qp.shape
    tp = kwp.shape[2]
    return pl.pallas_call(
        _win_attn_body,
        grid=(bn, ATT_KV_GROUPS, t // TQ),
        in_specs=[pl.BlockSpec((1, ATT_HPG, TQ, LANES), lambda bi, g, m: (bi, g, m, 0)),
                  pl.BlockSpec((1, 1, tp, LANES), lambda bi, g, m: (bi, g, 0, 0)),
                  pl.BlockSpec((1, 1, tp, LANES), lambda bi, g, m: (bi, g, 0, 0)),
                  pl.BlockSpec((ATT_HPG, TQ, TQ + WINDOW), lambda bi, g, m: (g, 0, 0))],
        out_specs=pl.BlockSpec((1, TQ, ATT_HPG * HEAD_DIM), lambda bi, g, m: (bi, m, g)),
        out_shape=jax.ShapeDtypeStruct((bn, t, ATT_HEADS * HEAD_DIM), F32),
        compiler_params=_cparams(3, 48),
        name="nsa_win_attn",
    )(qp, kwp, vwp, nb_w)


GMLP_ROWS = 1024


def _gmlp_body(uv_ref, g_ref, b_ref, ws_ref, bs_ref, o_ref):
    act = _gelu_tanh(uv_ref[0])
    u = act[:, :D_GMLP]
    v = _layer_norm(act[:, D_GMLP:], g_ref[...], b_ref[...])
    ti = lax.broadcasted_iota(jnp.int32, (GMLP_CHUNK, GMLP_CHUNK), 0)
    si = lax.broadcasted_iota(jnp.int32, (GMLP_CHUNK, GMLP_CHUNK), 1)
    w = [jnp.where(ti >= si, ws_ref[g], 0.0).astype(BF16) for g in range(GMLP_GROUPS)]
    lane = lax.broadcasted_iota(jnp.int32, (1, LANES), 1)
    lo = lane < HEAD_DIM
    for c in range(uv_ref.shape[1] // GMLP_CHUNK):
        rs = slice(c * GMLP_CHUNK, (c + 1) * GMLP_CHUNK)
        pairs = []
        for pr in range(GMLP_GROUPS // 2):
            vp = v[rs, pr * LANES:(pr + 1) * LANES]
            pairs.append(_dot(w[2 * pr], jnp.where(lo, vp, 0.0).astype(BF16))
                         + _dot(w[2 * pr + 1], jnp.where(lo, 0.0, vp).astype(BF16)))
        o_ref[0, rs, :] = u[rs] * (jnp.concatenate(pairs, axis=1) + bs_ref[...])


def _gmlp(uv, ln_g, ln_b, ws, bs):
    bn, t, _ = uv.shape
    rows = min(GMLP_ROWS, t)
    bs_lanes = jnp.repeat(bs.T, D_GMLP // GMLP_GROUPS, axis=1)
    full = lambda shape: pl.BlockSpec(shape, lambda bi, i: (0,) * len(shape))
    return pl.pallas_call(
        _gmlp_body,
        grid=(bn, t // rows),
        in_specs=[pl.BlockSpec((1, rows, 2 * D_GMLP), lambda bi, i: (bi, i, 0)),
                  full((1, D_GMLP)), full((1, D_GMLP)),
                  full((GMLP_GROUPS, GMLP_CHUNK, GMLP_CHUNK)), full((GMLP_CHUNK, D_GMLP))],
        out_specs=pl.BlockSpec((1, rows, D_GMLP), lambda bi, i: (bi, i, 0)),
        out_shape=jax.ShapeDtypeStruct((bn, t, D_GMLP), F32),
        compiler_params=_cparams(2, 40),
        name="gmlp",
    )(uv, ln_g[None], ln_b[None], ws, bs_lanes)


def _ssd_body(z_ref, xbc_ref, prev_ref, dt_ref, cw_ref, cb_ref, dtb_ref, alog_ref, dsk_ref, ng_ref, e4_ref,
              o_ref, xe_ref, state_ref):
    c = pl.program_id(1)
    L = SSM_CHUNK

    @pl.when(c == 0)
    def _():
        state_ref[...] = jnp.zeros(state_ref.shape, F32)

    xe_ref[0:8] = jnp.where(c > 0, prev_ref[0], 0.0)
    xe_ref[8:8 + L] = xbc_ref[0]
    conv = cb_ref[...]
    for k in range(SSM_CONV):
        conv = conv + cw_ref[k:k + 1, :] * xe_ref[pl.ds(8 - (SSM_CONV - 1) + k, L), :]
    xc = conv * _sigmoid(conv)
    xs, bm, cm = xc[:, :D_SSM], xc[:, D_SSM:D_SSM + 2 * SSM_STATE], xc[:, D_SSM + 2 * SSM_STATE:]

    dtr = dt_ref[0] + dtb_ref[...]
    dt = jnp.maximum(dtr, 0.0) + jnp.log1p(jnp.exp(-jnp.abs(dtr)))
    a = -jnp.exp(alog_ref[...]) * dt
    ti = lax.broadcasted_iota(jnp.int32, (L, L), 0)
    si = lax.broadcasted_iota(jnp.int32, (L, L), 1)
    causal = ti >= si
    cs = _dot_hi(jnp.where(causal, 1.0, 0.0), a)
    cs_t = cs.T
    e4 = e4_ref[...]
    dt_e = _dot_hi(dt, e4)
    cs_e = _dot_hi(cs, e4)
    last_e = cs_e[L - 1:L, :]
    x_dt = xs * dt_e
    x_dec = (x_dt * jnp.exp(last_e - cs_e)).astype(BF16)
    x_dt = x_dt.astype(BF16)
    grow = jnp.exp(cs_e)
    lane = lax.broadcasted_iota(jnp.int32, (1, LANES), 1)
    lo = lane < SSM_HEAD_DIM
    zero = jnp.zeros((), BF16)

    ys = []
    for g in range(2):
        gs = slice(g * LANES, (g + 1) * LANES)
        cg = cm[:, gs].astype(BF16)
        bg = bm[:, gs]
        scores = _dot_nt(cg, bg.astype(BF16))
        xp = x_dt[:, gs]
        y = None
        for hh in range(2):
            h = 2 * g + hh
            seg = jnp.broadcast_to(cs[:, h:h + 1], (L, L)) - jnp.broadcast_to(cs_t[h:h + 1, :], (L, L))
            w = (scores * jnp.where(causal, jnp.exp(seg), 0.0)).astype(BF16)
            xh = jnp.where(lo, xp, zero) if hh == 0 else jnp.where(lo, zero, xp)
            yh = _dot(w, xh)
            y = yh if y is None else y + yh
        state = state_ref[g]
        y = y + _dot(cg, state.astype(BF16)) * grow[:, gs]
        state_ref[g] = jnp.exp(last_e[:, gs]) * state + _dot(bg.T.astype(BF16), x_dec[:, gs])
        ys.append(y)
    y = jnp.concatenate(ys, axis=1) + xs * dsk_ref[...]
    zz = z_ref[0]
    gg = y * (zz * _sigmoid(zz))
    outs = []
    for g in range(2):
        gp = gg[:, g * LANES:(g + 1) * LANES]
        outs.append(gp * lax.rsqrt(jnp.mean(gp * gp, axis=-1, keepdims=True) + LN_EPS))
    o_ref[0] = jnp.concatenate(outs, axis=1) * ng_ref[...]


def _head_lanes(v):
    return jnp.repeat(v, SSM_HEAD_DIM)[None]


def _ssd(z, xbc, dt, conv_w, conv_b, dt_bias, a_log, d_skip, norm_g):
    bn, t, _ = z.shape
    L = SSM_CHUNK
    first_lanes = lambda v: jnp.zeros((1, LANES), F32).at[0, :SSM_HEADS].set(v)
    e4 = np.zeros((LANES, D_SSM), np.float32)
    for h in range(SSM_HEADS):
        e4[h, h * SSM_HEAD_DIM:(h + 1) * SSM_HEAD_DIM] = 1.0
    full = lambda shape: pl.BlockSpec(shape, lambda bi, c: (0,) * len(shape))
    return pl.pallas_call(
        _ssd_body,
        grid=(bn, t // L),
        in_specs=[pl.BlockSpec((1, L, D_SSM), lambda bi, c: (bi, c, 0)),
                  pl.BlockSpec((1, L, D_XBC), lambda bi, c: (bi, c, 0)),
                  pl.BlockSpec((1, 8, D_XBC), lambda bi, c: (bi, jnp.maximum(c * (L // 8) - 1, 0), 0)),
                  pl.BlockSpec((1, L, LANES), lambda bi, c: (bi, c, 0)),
                  full((SSM_CONV, D_XBC)), full((1, D_XBC)), full((1, LANES)), full((1, LANES)),
                  full((1, D_SSM)), full((1, D_SSM)), full((LANES, D_SSM))],
        out_specs=pl.BlockSpec((1, L, D_SSM), lambda bi, c: (bi, c, 0)),
        out_shape=jax.ShapeDtypeStruct((bn, t, D_SSM), F32),
        scratch_shapes=[pltpu.VMEM((L + 8, D_XBC), F32),
                        pltpu.VMEM((2, SSM_STATE, LANES), F32)],
        compiler_params=_cparams(2, 40),
        name="ssd",
    )(z, xbc, xbc, dt, conv_w, conv_b[None], first_lanes(dt_bias), first_lanes(a_log),
      _head_lanes(d_skip), norm_g[None], jnp.asarray(e4))


def _outproj_body(x_ref, mod_ref, oc_ref, os_ref, ow_ref, gt_ref, e_ref, gm_ref, ssm_ref, w_ref, g_ref, b_ref,
                  o_ref):
    x = x_ref[0]
    m = mod_ref[0]
    gate = m[5:6]
    d_att = ATT_HEADS * HEAD_DIM
    ge = _dot_hi(gt_ref[0], e_ref[...])
    o_att = (ge[:, 0:d_att] * oc_ref[0] + ge[:, d_att:2 * d_att] * os_ref[0]
             + ge[:, 2 * d_att:3 * d_att] * ow_ref[0])
    y = (_dot(o_att.astype(BF16), w_ref[0:d_att, :])
         + _dot(gm_ref[0].astype(BF16), w_ref[d_att:d_att + D_GMLP, :])
         + _dot(ssm_ref[0].astype(BF16), w_ref[d_att + D_GMLP:, :]))
    z = ALPHA * x + (1.0 + gate) * y
    o_ref[0] = _layer_norm(z, g_ref[...], b_ref[...])


def _mixer_outproj(x, mod_l, o_c, o_s, o_w, gates, o_gmlp, o_ssm, w_out, g, b):
    bn, t, d = x.shape
    d_att = ATT_HEADS * HEAD_DIM
    e = np.zeros((LANES, 3 * d_att), np.float32)
    for h in range(ATT_HEADS):
        for br in range(3):
            e[3 * h + br, br * d_att + h * HEAD_DIM:br * d_att + (h + 1) * HEAD_DIM] = 1.0
    row = lambda width: pl.BlockSpec((1, TM, width), lambda bi, i: (bi, i, 0))
    full = lambda shape: pl.BlockSpec(shape, lambda bi, i: (0,) * len(shape))
    return pl.pallas_call(
        _outproj_body,
        grid=(bn, t // TM),
        in_specs=[row(d), pl.BlockSpec((1, N_SUB * 3, d), lambda bi, i: (bi, 0, 0)),
                  row(d_att), row(d_att), row(d_att), row(LANES), full((LANES, 3 * d_att)),
                  row(D_GMLP), row(D_SSM), full((d, d)), full((1, d)), full((1, d))],
        out_specs=row(d),
        out_shape=jax.ShapeDtypeStruct(x.shape, F32),
        compiler_params=_cparams(2, 48),
        name="mixer_outproj",
    )(x, mod_l, o_c, o_s, o_w, gates, jnp.asarray(e), o_gmlp, o_ssm, w_out.astype(BF16), g[None], b[None])


def _nsa(qp, kcv, ksa, vst,kwp, vwp, tables, ov, cmp_pe, cmp_w1, cmp_b1, cmp_w2):
    nb_c, nb_s, nb_w = tables
    kcp = _compress(kcv, cmp_pe, cmp_w1, cmp_b1, cmp_w2)
    o_c, imp = _cmp_attention(qp, kcp, nb_c, ov)
    selneg = _select_blocks(imp)
    o_s = _sel_attention(qp, selneg, ksa, vst,nb_s)
    o_w = _win_attention(qp, kwp, vwp, nb_w)
    return o_c, o_s, o_w


def kernel(x, c, rel_bias, ada_w, ada_b, ln_g, ln_b, ffn_w1, ffn_w3, ffn_w2, w_in, w_out, cmp_pe, cmp_w1, cmp_b1,
           cmp_w2, gmlp_ln_g, gmlp_ln_b, gmlp_ws, gmlp_bs, ssm_conv_w, ssm_conv_b, ssm_dt_bias, ssm_a_log, ssm_d,
           ssm_norm_g):
    t = x.shape[1]
    mod = _adaln_mod(c, ada_w, ada_b)
    tables = _bias_tables(rel_bias)
    ov = _overlap_matrix(t)
    for l in range(ada_w.shape[0]):
        x = _ffn_sublayer(x, mod[l], ffn_w1[l, 0], ffn_w3[l, 0], ffn_w2[l, 0], ln_g[l, 0], ln_b[l, 0], 0)
        qp, kcv, ksa, vst,kwp, vwp, gates, gm_in, z, xbc, dt = _mixer_inproj(x, mod[l], _relayout_w_in(w_in[l]))
        o_c, o_s, o_w = _nsa(qp, kcv, ksa, vst,kwp, vwp, tables, ov, cmp_pe[l], cmp_w1[l], cmp_b1[l], cmp_w2[l])
        o_gmlp = _gmlp(gm_in, gmlp_ln_g[l], gmlp_ln_b[l], gmlp_ws[l], gmlp_bs[l])
        o_ssm = _ssd(z, xbc, dt, ssm_conv_w[l], ssm_conv_b[l], ssm_dt_bias[l], ssm_a_log[l], ssm_d[l],
                     ssm_norm_g[l])
        x = _mixer_outproj(x, mod[l], o_c, o_s, o_w, gates, o_gmlp, o_ssm, w_out[l], ln_g[l, 1], ln_b[l, 1])
        x = _ffn_sublayer(x, mod[l], ffn_w1[l, 1], ffn_w3[l, 1], ffn_w2[l, 1], ln_g[l, 2], ln_b[l, 2], 2)
    return x
```

```python
import functools
import math

import numpy as np
import jax
import jax.numpy as jnp
from jax import lax
from jax.experimental import pallas as pl
from jax.experimental.pallas import tpu as pltpu

F32 = jnp.float32
BF16 = jnp.bfloat16
HI = lax.Precision.HIGHEST

D_MODEL = 1024
DEPTH = 2
ATT_HEADS = 8
ATT_KV_GROUPS = 2
ATT_HPG = 4
HEAD_DIM = 64
CMP_LEN = 32
CMP_STRIDE = 16
CMP_HIDDEN = 256
SLC_BLOCK = 64
SLC_TOPN = 16
WINDOW = 512
SEL_FORCE = 1.0e4
REL_BUCKETS = 32
REL_MAX_DIST = 128
GMLP_GROUPS = 4
GMLP_CHUNK = 128
D_GMLP = 256
SSM_HEADS = 4
SSM_HEAD_DIM = 64
D_SSM = 256
SSM_STATE = 128
SSM_CONV = 4
SSM_CHUNK = 256
D_XBC = 768
D_FF = 2816
N_SUB = 3
ALPHA = (2 * DEPTH) ** 0.25
LN_EPS = 1e-5

LANES = 128
LOG2E = math.log2(math.e)
NEG_BIG = -(2.0 ** 100)
KPAD = 512
CMP_KPAD = 16
CMP_CHUNK_LOG2 = 7
CMP_NEAR = 256
CMP_FLAG0 = HEAD_DIM + 1
TQ_C = 256
TQ = 256
TK = 256
SEL_FAR_TILES = 2
SEL_SUBTILES = 2
TM = 512
FF_CHUNK = 256
SEL_CHUNK_BLOCKS = 128

COL_Q = 0
COL_KV = 1024
COL_GATE = 2560
COL_GMLP = 2688
COL_Z = 3200
COL_XBC = 3456
COL_DT = 4224
N_COLS = 4352


def _cparams(n_axes, vmem_mb):
    return pltpu.CompilerParams(dimension_semantics=("arbitrary",) * n_axes,
                                vmem_limit_bytes=vmem_mb * 1024 * 1024)


def _sigmoid(x):
    return 1.0 / (1.0 + jnp.exp(-x))


def _gelu_tanh(x):
    return x * (0.5 * (1.0 + jnp.tanh(math.sqrt(2.0 / math.pi) * (x + 0.044715 * (x * x * x)))))


def _layer_norm(z, g, b):
    mu = jnp.mean(z, axis=-1, keepdims=True)
    d = z - mu
    var = jnp.mean(d * d, axis=-1, keepdims=True)
    return d * lax.rsqrt(var + LN_EPS) * g + b


def _dot(a, b):
    return jnp.dot(a, b, preferred_element_type=F32)


def _dot_hi(a, b):
    return jnp.dot(a, b, preferred_element_type=F32, precision=HI)


def _dot_nt(a, b):
    return lax.dot_general(a, b, (((1,), (1,)), ((), ())), preferred_element_type=F32)


def _mod_body(c_ref, w_ref, b_ref, o_ref):
    c = c_ref[...]
    o_ref[0] = _dot_hi(c * _sigmoid(c), w_ref[0]) + b_ref[0]


def _adaln_mod(c, ada_w, ada_b):
    n_layers, d, n = ada_w.shape
    bn = c.shape[0]
    tn = 1536
    cp = jnp.zeros((8, d), F32).at[:bn].set(c)
    out = pl.pallas_call(
        _mod_body,
        grid=(n_layers, n // tn),
        in_specs=[pl.BlockSpec((8, d), lambda l, j: (0, 0)),
                  pl.BlockSpec((1, d, tn), lambda l, j: (l, 0, j)),
                  pl.BlockSpec((1, 1, tn), lambda l, j: (l, 0, j))],
        out_specs=pl.BlockSpec((1, 8, tn), lambda l, j: (l, 0, j)),
        out_shape=jax.ShapeDtypeStruct((n_layers, 8, n), F32),
        compiler_params=_cparams(2, 32),
        name="adaln_mod",
    )(cp, ada_w, ada_b.reshape(n_layers, 1, n))
    return out[:, :bn].reshape(n_layers, bn, N_SUB * 3, d)


def _ffn_body(x_ref, mod_ref, w1_ref, w3_ref, w2_ref, g_ref, b_ref, o_ref, *, sub):
    x = x_ref[0]
    m = mod_ref[0]
    shift, scale, gate = m[3 * sub:3 * sub + 1], m[3 * sub + 1:3 * sub + 2], m[3 * sub + 2:3 * sub + 3]
    h = (x * (1.0 + scale) + shift).astype(BF16)
    acc = jnp.zeros(x.shape, F32)
    for j in range(D_FF // FF_CHUNK):
        sl = slice(j * FF_CHUNK, (j + 1) * FF_CHUNK)
        a = _dot(h, w1_ref[:, sl])
        b = _dot(h, w3_ref[:, sl])
        acc = acc + _dot((a * _sigmoid(a) * b).astype(BF16), w2_ref[sl, :])
    z = ALPHA * x + (0.5 * (1.0 + gate)) * acc
    o_ref[0] = _layer_norm(z, g_ref[...], b_ref[...])


def _ffn_sublayer(x, mod_l, w1, w3, w2, g, b, sub):
    bn, t, d = x.shape
    full = lambda shape: pl.BlockSpec(shape, lambda bi, i: (0,) * len(shape))
    return pl.pallas_call(
        functools.partial(_ffn_body, sub=sub),
        grid=(bn, t // TM),
        in_specs=[pl.BlockSpec((1, TM, d), lambda bi, i: (bi, i, 0)),
                  pl.BlockSpec((1, N_SUB * 3, d), lambda bi, i: (bi, 0, 0)),
                  full((d, D_FF)), full((d, D_FF)), full((D_FF, d)),
                  full((1, d)), full((1, d))],
        out_specs=pl.BlockSpec((1, TM, d), lambda bi, i: (bi, i, 0)),
        out_shape=jax.ShapeDtypeStruct(x.shape, F32),
        compiler_params=_cparams(2, 56),
        name="ffn",
    )(x, mod_l, w1.astype(BF16), w3.astype(BF16), w2.astype(BF16), g[None], b[None])


def _pad_cols(w, width):
    return jnp.pad(w, ((0, 0), (0, width - w.shape[1])))


def _relayout_w_in(w_in):
    cols = [_pad_cols(w_in[:, HEAD_DIM * h:HEAD_DIM * (h + 1)], LANES) for h in range(ATT_HEADS)]
    for i in range(6):
        for g in range(ATT_KV_GROUPS):
            base = 512 + 128 * i + HEAD_DIM * g
            cols.append(_pad_cols(w_in[:, base:base + HEAD_DIM], LANES))
    cols.append(_pad_cols(w_in[:, 1280:1304], LANES))
    cols.append(w_in[:, 1304:1816])
    cols.append(w_in[:, 1816:2072])
    cols.append(w_in[:, 2072:2840])
    cols.append(_pad_cols(w_in[:, 2840:2844], LANES))
    return jnp.concatenate(cols, axis=1).astype(BF16)


def _inproj_body(x_ref, mod_ref, w_ref, qp_ref, kcv_ref, ksa_ref, vst_ref, kwp_ref, vwp_ref,
                 gt_ref, gm_ref, z_ref, xbc_ref, dt_ref):
    i = pl.program_id(1)
    x = x_ref[0]
    m = mod_ref[0]
    h = (x * (1.0 + m[4:5]) + m[3:4]).astype(BF16)

    def proj(c0, width):
        return _dot(h, w_ref[:, c0:c0 + width])

    lane = lax.broadcasted_iota(jnp.int32, (1, LANES), 1)
    q_flag = jnp.where(lane == HEAD_DIM, NEG_BIG, 0.0)
    for hh in range(ATT_HEADS):
        qp_ref[0, hh] = (proj(COL_Q + LANES * hh, LANES) * (HEAD_DIM ** -0.5 * LOG2E) + q_flag).astype(BF16)
    for k in range(4):
        kcv_ref[0, k] = proj(COL_KV + LANES * k, LANES).astype(BF16)
    gt_ref[0] = _sigmoid(proj(COL_GATE, LANES))
    gm_ref[0] = proj(COL_GMLP, 512)
    z_ref[0] = proj(COL_Z, D_SSM)
    xbc_ref[0] = proj(COL_XBC, D_XBC)
    dt_ref[0] = proj(COL_DT, LANES)

    @pl.when(i == 0)
    def _():
        dummy = jnp.broadcast_to(jnp.where(lane == HEAD_DIM, 1.0, 0.0), (TM, LANES)).astype(BF16)
        zeros = jnp.zeros((TM, LANES), BF16)
        for g in range(ATT_KV_GROUPS):
            ksa_ref[0, g, :, 0:LANES] = dummy
            ksa_ref[0, g, :, LANES:2 * LANES] = zeros
            kwp_ref[0, g] = dummy
            vst_ref[0, g] = jnp.zeros((LANES, TM), BF16)
            vwp_ref[0, g] = zeros

    @pl.when(i > 0)
    def _():
        pos = (i - 1) * TM + lax.broadcasted_iota(jnp.int32, (TM, LANES), 0)
        blk_lane = (pos >> 6) & (SEL_CHUNK_BLOCKS - 1)
        onehot = jnp.where(blk_lane == lax.broadcasted_iota(jnp.int32, (TM, LANES), 1), 1.0, 0.0).astype(BF16)
        ones_row = lax.broadcasted_iota(jnp.int32, (LANES, 1), 0) == HEAD_DIM
        for g in range(ATT_KV_GROUPS):
            ksa_ref[0, g, :, 0:LANES] = proj(COL_KV + LANES * (4 + g), LANES).astype(BF16)
            ksa_ref[0, g, :, LANES:2 * LANES] = onehot
            vst_ref[0, g] = jnp.where(ones_row, 1.0, proj(COL_KV + LANES * (6 + g), LANES).T).astype(BF16)
            kwp_ref[0, g] = proj(COL_KV + LANES * (8 + g), LANES).astype(BF16)
            vwp_ref[0, g] = proj(COL_KV + LANES * (10 + g), LANES).astype(BF16)


def _mixer_inproj(x, mod_l, w_in):
    bn, t, d = x.shape
    tp = t + KPAD
    row = lambda width: pl.BlockSpec((1, TM, width), lambda bi, i: (bi, jnp.maximum(i - 1, 0), 0))
    sds = jax.ShapeDtypeStruct
    return pl.pallas_call(
        _inproj_body,
        grid=(bn, t // TM + 1),
        in_specs=[row(d),
                  pl.BlockSpec((1, N_SUB * 3, d), lambda bi, i: (bi, 0, 0)),
                  pl.BlockSpec((d, N_COLS), lambda bi, i: (0, 0))],
        out_specs=[pl.BlockSpec((1, ATT_HEADS, TM, LANES), lambda bi, i: (bi, 0, jnp.maximum(i - 1, 0), 0)),
                   pl.BlockSpec((1, 4, TM, LANES), lambda bi, i: (bi, 0, jnp.maximum(i - 1, 0), 0)),
                   pl.BlockSpec((1, ATT_KV_GROUPS, TM, 2 * LANES), lambda bi, i: (bi, 0, i, 0)),
                   pl.BlockSpec((1, ATT_KV_GROUPS, LANES, TM), lambda bi, i: (bi, 0, 0, i)),
                   pl.BlockSpec((1, ATT_KV_GROUPS, TM, LANES), lambda bi, i: (bi, 0, i, 0)),
                   pl.BlockSpec((1, ATT_KV_GROUPS, TM, LANES), lambda bi, i: (bi, 0, i, 0)),
                   row(LANES), row(512), row(D_SSM), row(D_XBC), row(LANES)],
        out_shape=[sds((bn, ATT_HEADS, t, LANES), BF16),
                   sds((bn, 4, t, LANES), BF16),
                   sds((bn, ATT_KV_GROUPS, tp, 2 * LANES), BF16),
                   sds((bn, ATT_KV_GROUPS, LANES, tp), BF16),
                   sds((bn, ATT_KV_GROUPS, tp, LANES), BF16),
                   sds((bn, ATT_KV_GROUPS, tp, LANES), BF16),
                   sds((bn, t, LANES), F32), sds((bn, t, 512), F32), sds((bn, t, D_SSM), F32),
                   sds((bn, t, D_XBC), F32), sds((bn, t, LANES), F32)],
        compiler_params=_cparams(2, 48),
        name="mixer_inproj",
    )(x, mod_l, w_in)


def _cmp_rows(t):
    return -(-(t // CMP_STRIDE + 120) // LANES) * LANES


def _compress_body(ck_ref, cv_ref, w1_ref, pe_ref, b1_ref, w2_ref, ovt_ref, k_ref, lhs_ref):
    n_rows = ck_ref.shape[2]
    rows = k_ref.shape[2]
    n_cmp = n_rows - CMP_LEN // CMP_STRIDE + 1
    half = CMP_STRIDE * HEAD_DIM
    row = lax.broadcasted_iota(jnp.int32, (rows, LANES), 0)
    lane = lax.broadcasted_iota(jnp.int32, (rows, LANES), 1)
    real = (row >= CMP_KPAD) & (row < CMP_KPAD + n_cmp)

    def mlp(c, j):
        w1 = w1_ref[j]
        h1 = _dot(c, w1[:half])
        h2 = _dot(c, w1[half:])
        cvec = _dot(pe_ref[j], w1)[0:1] + b1_ref[j]
        hid = h1 + pltpu.roll(h2, n_rows - 1, 0) + cvec
        out = _dot(_gelu_tanh(hid).astype(BF16), w2_ref[j])
        tail = rows - CMP_KPAD - n_rows
        full = jnp.concatenate([jnp.zeros((CMP_KPAD, LANES), F32), out, jnp.zeros((tail, LANES), F32)], axis=0)
        return jnp.where(real, full, 0.0)

    flags = jnp.where(lane == HEAD_DIM, jnp.where(real, 0.0, 1.0),
                      jnp.where(lane - CMP_FLAG0 == (row >> CMP_CHUNK_LOG2), 1.0, 0.0))
    k_ref[0, 0] = (mlp(ck_ref[0, 0], 0) + flags).astype(BF16)
    ones_row = lax.broadcasted_iota(jnp.int32, (LANES, 1), 0) == HEAD_DIM
    lhs_ref[0, 0, 0:LANES, :] = jnp.where(ones_row, 1.0, mlp(cv_ref[0, 0], 1).T).astype(BF16)
    lhs_ref[0, 0, LANES:, :] = ovt_ref[...].astype(BF16)


def _compress(kcv, cmp_pe, cmp_w1, cmp_b1, cmp_w2, ovt):
    bn, _, t, _ = kcv.shape
    n_rows = t // CMP_STRIDE
    width = CMP_STRIDE * HEAD_DIM
    c = kcv[..., :HEAD_DIM].reshape(bn, 4, n_rows, width)
    pe = jnp.zeros((2, 8, CMP_LEN * HEAD_DIM), F32).at[:, 0].set(cmp_pe.reshape(2, -1)).astype(BF16)
    w2 = jnp.pad(cmp_w2, ((0, 0), (0, 0), (0, LANES - HEAD_DIM))).astype(BF16)
    n_slc, rows = ovt.shape
    full = lambda shape: pl.BlockSpec(shape, lambda bi, g: (0,) * len(shape))
    return pl.pallas_call(
        _compress_body,
        grid=(bn, ATT_KV_GROUPS),
        in_specs=[pl.BlockSpec((1, 1, n_rows, width), lambda bi, g: (bi, g, 0, 0)),
                  pl.BlockSpec((1, 1, n_rows, width), lambda bi, g: (bi, ATT_KV_GROUPS + g, 0, 0)),
                  full((2, CMP_LEN * HEAD_DIM, CMP_HIDDEN)), full((2, 8, CMP_LEN * HEAD_DIM)),
                  full((2, 1, CMP_HIDDEN)), full((2, CMP_HIDDEN, LANES)), full((n_slc, rows))],
        out_specs=[pl.BlockSpec((1, 1, rows, LANES), lambda bi, g: (bi, g, 0, 0)),
                   pl.BlockSpec((1, 1, LANES + n_slc, rows), lambda bi, g: (bi, g, 0, 0))],
        out_shape=[jax.ShapeDtypeStruct((bn, ATT_KV_GROUPS, rows, LANES), BF16),
                   jax.ShapeDtypeStruct((bn, ATT_KV_GROUPS, LANES + n_slc, rows), BF16)],
        compiler_params=_cparams(2, 40),
        name="nsa_compress",
    )(c, c, cmp_w1.astype(BF16), pe, cmp_b1[:, None, :], w2, ovt)


def _rel_bucket(dist):
    n = jnp.maximum(dist, 0)
    max_exact = REL_BUCKETS // 2
    nf = jnp.maximum(n, 1).astype(F32)
    large = max_exact + (jnp.log(nf / max_exact) / math.log(REL_MAX_DIST / max_exact)
                         * (REL_BUCKETS - max_exact)).astype(jnp.int32)
    large = jnp.minimum(large, REL_BUCKETS - 1)
    return jnp.where(n < max_exact, n, large)


def _bias_tables(rel_bias):
    far = rel_bias[REL_BUCKETS - 1]

    def table(dist, valid, shift):
        onehot = (_rel_bucket(dist)[..., None] == jnp.arange(REL_BUCKETS)).astype(F32)
        b = jnp.einsum("qkb,bh->hqk", onehot, rel_bias, precision=HI)
        if shift:
            b = b - far[:, None, None]
        return jnp.where(valid[None], b * LOG2E, -jnp.inf)

    n_var = (1 << CMP_CHUNK_LOG2) * CMP_STRIDE // TQ_C
    d_c = jnp.asarray(TQ_C * np.arange(n_var)[:, None, None] + np.arange(TQ_C)[None, None, :]
                      - CMP_STRIDE * np.arange(CMP_NEAR)[None, :, None]
                      + (CMP_STRIDE * CMP_KPAD - CMP_LEN + 1))
    nb_c = jnp.stack([table(d_c[v], d_c[v] >= 0, True) for v in range(n_var)])
    ki = np.arange(2 * TK)[:, None]
    d_s = jnp.asarray(np.arange(TQ)[None, :] - ki + TK)
    nb_s = table(d_s, d_s >= 0, True)
    qi = np.arange(TQ)[:, None]
    d_w = jnp.asarray(qi - np.arange(TQ + WINDOW)[None, :] + WINDOW)
    nb_w = table(d_w, (d_w >= 0) & (d_w < WINDOW), False)
    return nb_c, nb_s, nb_w


def _overlap_matrix(t):
    n_cmp = (t - CMP_LEN) // CMP_STRIDE + 1
    n_slc = t // SLC_BLOCK
    cs = np.arange(n_cmp) * CMP_STRIDE
    ss = np.arange(n_slc) * SLC_BLOCK
    ov = np.clip(np.minimum(cs[:, None] + CMP_LEN, ss[None, :] + SLC_BLOCK)
                 - np.maximum(cs[:, None], ss[None, :]), 0, None) / CMP_LEN
    full = np.zeros((n_slc, _cmp_rows(t)), np.float32)
    full[:, CMP_KPAD:CMP_KPAD + n_cmp] = ov.T
    return jnp.asarray(full)


def _cmp_attn_body(q_ref, k_ref, lhs_ref, nb_ref, o_ref, imp_ref, s_ref):
    m = pl.program_id(2)
    near0 = m * (TQ_C // CMP_STRIDE)
    near_chunk = near0 >> CMP_CHUNK_LOG2
    a0 = pl.multiple_of(near_chunk << CMP_CHUNK_LOG2, 1 << CMP_CHUNK_LOG2)
    kf = k_ref[0, 0]
    rowi = lax.broadcasted_iota(jnp.int32, (LANES, TQ_C), 0)
    chunk_flag = jnp.where((rowi >= CMP_FLAG0) & (rowi - CMP_FLAG0 > near_chunk + 1), NEG_BIG, 0.0)
    for r in range(ATT_HPG):
        q_t = (q_ref[0, r].astype(F32).T + chunk_flag).astype(BF16)
        s_ref[r] = _dot(kf, q_t)
    for r in range(ATT_HPG):
        s_ref[r, pl.ds(a0, CMP_NEAR), :] += nb_ref[0, r]
    lhs = lhs_ref[0, 0]
    imp = jnp.zeros(imp_ref.shape[2:], F32)
    outs = []
    for r in range(ATT_HPG):
        s = s_ref[r]
        mx = jnp.max(s, axis=0, keepdims=True)
        mx = jnp.where(mx < 0.5 * NEG_BIG, 0.0, mx)
        p = jnp.exp2(s - mx).astype(BF16)
        res = _dot(lhs, p)
        inv = 1.0 / jnp.maximum(res[HEAD_DIM:HEAD_DIM + 1, :], 1e-30)
        outs.append((res[0:LANES, :] * inv).T[:, :HEAD_DIM])
        imp = imp + res[LANES:, :] * inv
    o_ref[0] = jnp.concatenate(outs, axis=1)
    imp_ref[0, 0] = imp


def _cmp_attention(qp, kcp, lhs, nb_c):
    bn, _, t, _ = qp.shape
    rows = kcp.shape[2]
    n_slc = lhs.shape[2] - LANES
    n_var = nb_c.shape[0]
    return pl.pallas_call(
        _cmp_attn_body,
        grid=(bn, ATT_KV_GROUPS, t // TQ_C),
        in_specs=[pl.BlockSpec((1, ATT_HPG, TQ_C, LANES), lambda bi, g, m: (bi, g, m, 0)),
                  pl.BlockSpec((1, 1, rows, LANES), lambda bi, g, m: (bi, g, 0, 0)),
                  pl.BlockSpec((1, 1, LANES + n_slc, rows), lambda bi, g, m: (bi, g, 0, 0)),
                  pl.BlockSpec((1, ATT_HPG, CMP_NEAR, TQ_C), lambda bi, g, m: (m % n_var, g, 0, 0))],
        out_specs=[pl.BlockSpec((1, TQ_C, ATT_HPG * HEAD_DIM), lambda bi, g, m: (bi, m, g)),
                   pl.BlockSpec((1, 1, n_slc, TQ_C), lambda bi, g, m: (bi, g, 0, m))],
        out_shape=[jax.ShapeDtypeStruct((bn, t, ATT_HEADS * HEAD_DIM), F32),
                   jax.ShapeDtypeStruct((bn, ATT_KV_GROUPS, n_slc, t), F32)],
        scratch_shapes=[pltpu.VMEM((ATT_HPG, rows, TQ_C), F32)],
        compiler_params=_cparams(3, 40),
        name="nsa_cmp_attn",
    )(qp, kcp, lhs, nb_c)


TOPK_COLS = 1024


def _topk_body(imp_ref, o_ref):
    i = pl.program_id(2)
    imp = imp_ref[0, 0]
    n_slc, cols = imp.shape
    t = i * cols + lax.broadcasted_iota(jnp.int32, (1, cols), 1)
    cur = t >> 6
    blk = lax.broadcasted_iota(jnp.int32, (n_slc, cols), 0)
    forced = (blk == 0) | (blk == cur) | (blk == cur - 1)
    vals = jnp.where(forced, SEL_FORCE, jnp.where(blk <= cur, imp, -SEL_FORCE))
    blkf = blk.astype(F32)
    mask = jnp.full((n_slc, cols), NEG_BIG, F32)
    for _ in range(min(SLC_TOPN, n_slc)):
        mx = jnp.max(vals, axis=0, keepdims=True)
        first = jnp.min(jnp.where(vals == mx, blkf, float(n_slc)), axis=0, keepdims=True)
        hit = blkf == first
        mask = jnp.where(hit, 0.0, mask)
        vals = jnp.where(hit, -jnp.inf, vals)
    o_ref[0, 0] = mask.astype(BF16)


def _select_blocks(imp_t):
    bn, g, n_slc, t = imp_t.shape
    cols = min(TOPK_COLS, t)
    spec = pl.BlockSpec((1, 1, n_slc, cols), lambda bi, gi, i: (bi, gi, 0, i))
    return pl.pallas_call(
        _topk_body,
        grid=(bn, g, t // cols),
        in_specs=[spec],
        out_specs=spec,
        out_shape=jax.ShapeDtypeStruct(imp_t.shape, BF16),
        compiler_params=_cparams(3, 40),
        name="nsa_topk",
    )(imp_t)


def _sel_attn_body(q_ref, sel_ref, k_ref, vt_ref, nb_ref, o_ref, qat_ref, m_ref, acc_ref):
    pair = pl.program_id(2)
    n_chunks = qat_ref.shape[0]
    m_a = SEL_SUBTILES * pair
    for sub in range(SEL_SUBTILES):
        qs = slice(sub * TQ, (sub + 1) * TQ)
        for r in range(ATT_HPG):
            q_t = q_ref[0, r, qs, :].astype(F32).T.astype(BF16)
            for c in range(n_chunks):
                qat_ref[c, ATT_HPG * sub + r, 0:LANES, :] = q_t
        for c in range(n_chunks):
            sel_t = sel_ref[0, 0, c * LANES:(c + 1) * LANES, qs]
            for r in range(ATT_HPG):
                qat_ref[c, ATT_HPG * sub + r, LANES:2 * LANES, :] = sel_t
    m_ref[...] = jnp.full(m_ref.shape, -jnp.inf, F32)
    acc_ref[...] = jnp.zeros(acc_ref.shape, F32)

    tiles_per_chunk = SEL_CHUNK_BLOCKS * SLC_BLOCK // TK
    units_a = tuple(range(ATT_HPG))
    units_b = tuple(range(ATT_HPG, 2 * ATT_HPG))
    bias_hi = lambda r: nb_ref[r, TK:2 * TK, :]
    bias_lo = lambda r: nb_ref[r, 0:TK, :]

    def step(work):
        todo = []
        for j, n_tiles, c, units, bias in work:
            row0 = pl.multiple_of(KPAD + TK * j, TK)
            kt = k_ref[0, 0, pl.ds(row0, n_tiles * TK), :]
            vt = vt_ref[0, 0, :, pl.ds(row0, n_tiles * TK)]
            todo += [(kt, vt, c, u, bias) for u in units]
        ahead = 6
        scores = [_dot(kt, qat_ref[c, u]) for kt, _, c, u, _ in todo[:ahead]]
        for i, (_, vt, _, u, bias) in enumerate(todo):
            if i + ahead < len(todo):
                kt_n, _, c_n, u_n, _ = todo[i + ahead]
                scores.append(_dot(kt_n, qat_ref[c_n, u_n]))
            s = scores[i]
            scores[i] = None
            if bias is not None:
                s = s + bias(u % ATT_HPG)
            m_prev = m_ref[u]
            m_new = jnp.maximum(m_prev, jnp.max(s, axis=0, keepdims=True))
            alpha = jnp.exp2(m_prev - m_new)
            p = jnp.exp2(s - m_new).astype(BF16)
            acc_ref[u] = alpha * acc_ref[u] + _dot(vt, p)
            m_ref[u] = m_new

    def chunk_of(j):
        return jnp.maximum(j, 0) // tiles_per_chunk

    for c in range(n_chunks):
        @pl.when(chunk_of(m_a) == c)
        def _(c=c):
            step([(m_a, 1, c, units_a, bias_hi), (m_a + 1, 1, c, units_b, bias_hi)])
    for c_b in range(n_chunks):
        for c_a in sorted({max(c_b - 1, 0), c_b}):
            @pl.when((chunk_of(m_a) == c_b) & (chunk_of(m_a - 1) == c_a))
            def _(c_a=c_a, c_b=c_b):
                step([(m_a - 1, 1, c_a, units_a, bias_lo), (m_a, 1, c_b, units_b, bias_lo)])
    for c in range(n_chunks):
        @pl.when(chunk_of(m_a - 1) == c)
        def _(c=c):
            step([(m_a - 1, 1, c, units_b, None)])

    for c in range(n_chunks):
        first = c * tiles_per_chunk
        n_far = jnp.clip(m_a - 1 - first, 0, tiles_per_chunk)
        n_big = n_far // SEL_FAR_TILES

        def far_big(i, carry, c=c, first=first):
            step([(first + SEL_FAR_TILES * i, SEL_FAR_TILES, c, units_a + units_b, None)])
            return carry

        def far_one(j, carry, c=c):
            step([(j, 1, c, units_a + units_b, None)])
            return carry

        lax.fori_loop(0, n_big, far_big, 0)
        lax.fori_loop(first + SEL_FAR_TILES * n_big, first + n_far, far_one, 0)
    for sub in range(SEL_SUBTILES):
        outs = []
        for r in range(ATT_HPG):
            acc = acc_ref[ATT_HPG * sub + r]
            outs.append((acc / acc[HEAD_DIM:HEAD_DIM + 1, :]).T[:, :HEAD_DIM])
        o_ref[0, sub * TQ:(sub + 1) * TQ, :] = jnp.concatenate(outs, axis=1)


def _sel_attention(qp, selneg, ksa, vst, nb_s):
    bn, _, t, _ = qp.shape
    tp = ksa.shape[2]
    n_slc = selneg.shape[2]
    n_chunks = max(n_slc // SEL_CHUNK_BLOCKS, 1)
    tq = SEL_SUBTILES * TQ
    n_units = SEL_SUBTILES * ATT_HPG
    return pl.pallas_call(
        _sel_attn_body,
        grid=(bn, ATT_KV_GROUPS, t // tq),
        in_specs=[pl.BlockSpec((1, ATT_HPG, tq, LANES), lambda bi, g, m: (bi, g, m, 0)),
                  pl.BlockSpec((1, 1, n_slc, tq), lambda bi, g, m: (bi, g, 0, m)),
                  pl.BlockSpec((1, 1, tp, 2 * LANES), lambda bi, g, m: (bi, g, 0, 0)),
                  pl.BlockSpec((1, 1, LANES, tp), lambda bi, g, m: (bi, g, 0, 0)),
                  pl.BlockSpec((ATT_HPG, 2 * TK, TQ), lambda bi, g, m: (g, 0, 0))],
        out_specs=pl.BlockSpec((1, tq, ATT_HPG * HEAD_DIM), lambda bi, g, m: (bi, m, g)),
        out_shape=jax.ShapeDtypeStruct((bn, t, ATT_HEADS * HEAD_DIM), F32),
        scratch_shapes=[pltpu.VMEM((n_chunks, n_units, 2 * LANES, TQ), BF16),
                        pltpu.VMEM((n_units, 1, TQ), F32),
                        pltpu.VMEM((n_units, LANES, TQ), F32)],
        compiler_params=_cparams(3, 56),
        name="nsa_sel_attn",
    )(qp, selneg, ksa, vst, nb_s)


def _win_attn_body(q_ref, k_ref, v_ref, nb_ref, o_ref):
    m = pl.program_id(2)
    row0 = pl.multiple_of(m * TQ, TQ)
    kt = k_ref[0, 0, pl.ds(row0, TQ + WINDOW), :]
    vt = v_ref[0, 0, pl.ds(row0, TQ + WINDOW), :]
    outs = []
    for r in range(ATT_HPG):
        s = _dot_nt(q_ref[0, r], kt) + nb_ref[r]
        p = jnp.exp2(s - jnp.max(s, axis=1, keepdims=True))
        den = jnp.sum(p, axis=1, keepdims=True)
        outs.append((_dot(p.astype(BF16), vt) / den)[:, :HEAD_DIM])
    o_ref[0] = jnp.concatenate(outs, axis=1)


def _win_attention(qp, kwp, vwp, nb_w):
    bn, _, t, _ = qp.shape
    tp = kwp.shape[2]
    return pl.pallas_call(
        _win_attn_body,
        grid=(bn, ATT_KV_GROUPS, t // TQ),
        in_specs=[pl.BlockSpec((1, ATT_HPG, TQ, LANES), lambda bi, g, m: (bi, g, m, 0)),
                  pl.BlockSpec((1, 1, tp, LANES), lambda bi, g, m: (bi, g, 0, 0)),
                  pl.BlockSpec((1, 1, tp, LANES), lambda bi, g, m: (bi, g, 0, 0)),
                  pl.BlockSpec((ATT_HPG, TQ, TQ + WINDOW), lambda bi, g, m: (g, 0, 0))],
        out_specs=pl.BlockSpec((1, TQ, ATT_HPG * HEAD_DIM), lambda bi, g, m: (bi, m, g)),
        out_shape=jax.ShapeDtypeStruct((bn, t, ATT_HEADS * HEAD_DIM), F32),
        compiler_params=_cparams(3, 48),
        name="nsa_win_attn",
    )(qp, kwp, vwp, nb_w)


GMLP_ROWS = 1024


def _gmlp_body(uv_ref, g_ref, b_ref, ws_ref, bs_ref, o_ref):
    act = _gelu_tanh(uv_ref[0])
    u = act[:, :D_GMLP]
    v = _layer_norm(act[:, D_GMLP:], g_ref[...], b_ref[...])
    ti = lax.broadcasted_iota(jnp.int32, (GMLP_CHUNK, GMLP_CHUNK), 0)
    si = lax.broadcasted_iota(jnp.int32, (GMLP_CHUNK, GMLP_CHUNK), 1)
    w = [jnp.where(ti >= si, ws_ref[g], 0.0).astype(BF16) for g in range(GMLP_GROUPS)]
    lane = lax.broadcasted_iota(jnp.int32, (1, LANES), 1)
    lo = lane < HEAD_DIM
    for c in range(uv_ref.shape[1] // GMLP_CHUNK):
        rs = slice(c * GMLP_CHUNK, (c + 1) * GMLP_CHUNK)
        pairs = []
        for pr in range(GMLP_GROUPS // 2):
            vp = v[rs, pr * LANES:(pr + 1) * LANES]
            pairs.append(_dot(w[2 * pr], jnp.where(lo, vp, 0.0).astype(BF16))
                         + _dot(w[2 * pr + 1], jnp.where(lo, 0.0, vp).astype(BF16)))
        o_ref[0, rs, :] = u[rs] * (jnp.concatenate(pairs, axis=1) + bs_ref[...])


def _gmlp(uv, ln_g, ln_b, ws, bs):
    bn, t, _ = uv.shape
    rows = min(GMLP_ROWS, t)
    bs_lanes = jnp.repeat(bs.T, D_GMLP // GMLP_GROUPS, axis=1)
    full = lambda shape: pl.BlockSpec(shape, lambda bi, i: (0,) * len(shape))
    return pl.pallas_call(
        _gmlp_body,
        grid=(bn, t // rows),
        in_specs=[pl.BlockSpec((1, rows, 2 * D_GMLP), lambda bi, i: (bi, i, 0)),
                  full((1, D_GMLP)), full((1, D_GMLP)),
                  full((GMLP_GROUPS, GMLP_CHUNK, GMLP_CHUNK)), full((GMLP_CHUNK, D_GMLP))],
        out_specs=pl.BlockSpec((1, rows, D_GMLP), lambda bi, i: (bi, i, 0)),
        out_shape=jax.ShapeDtypeStruct((bn, t, D_GMLP), F32),
        compiler_params=_cparams(2, 40),
        name="gmlp",
    )(uv, ln_g[None], ln_b[None], ws, bs_lanes)


def _ssd_body(z_ref, xbc_ref, prev_ref, dt_ref, cw_ref, cb_ref, dtb_ref, alog_ref, dsk_ref, ng_ref, e4_ref,
              o_ref, xe_ref, state_ref):
    c = pl.program_id(1)
    L = SSM_CHUNK

    @pl.when(c == 0)
    def _():
        state_ref[...] = jnp.zeros(state_ref.shape, F32)

    xe_ref[0:8] = jnp.where(c > 0, prev_ref[0], 0.0)
    xe_ref[8:8 + L] = xbc_ref[0]
    conv = cb_ref[...]
    for k in range(SSM_CONV):
        conv = conv + cw_ref[k:k + 1, :] * xe_ref[pl.ds(8 - (SSM_CONV - 1) + k, L), :]
    xc = conv * _sigmoid(conv)
    xs, bm, cm = xc[:, :D_SSM], xc[:, D_SSM:D_SSM + 2 * SSM_STATE], xc[:, D_SSM + 2 * SSM_STATE:]

    dtr = dt_ref[0] + dtb_ref[...]
    dt = jnp.maximum(dtr, 0.0) + jnp.log1p(jnp.exp(-jnp.abs(dtr)))
    a = -jnp.exp(alog_ref[...]) * dt
    ti = lax.broadcasted_iota(jnp.int32, (L, L), 0)
    si = lax.broadcasted_iota(jnp.int32, (L, L), 1)
    causal = ti >= si
    cs = _dot_hi(jnp.where(causal, 1.0, 0.0), a)
    cs_t = cs.T
    e4 = e4_ref[...]
    dt_e = _dot_hi(dt, e4)
    cs_e = _dot_hi(cs, e4)
    last_e = cs_e[L - 1:L, :]
    x_dt = xs * dt_e
    x_dec = (x_dt * jnp.exp(last_e - cs_e)).astype(BF16)
    x_dt = x_dt.astype(BF16)
    grow = jnp.exp(cs_e)
    lane = lax.broadcasted_iota(jnp.int32, (1, LANES), 1)
    lo = lane < SSM_HEAD_DIM
    zero = jnp.zeros((), BF16)

    ys = []
    for g in range(2):
        gs = slice(g * LANES, (g + 1) * LANES)
        cg = cm[:, gs].astype(BF16)
        bg = bm[:, gs]
        scores = _dot_nt(cg, bg.astype(BF16))
        xp = x_dt[:, gs]
        y = None
        for hh in range(2):
            h = 2 * g + hh
            seg = jnp.broadcast_to(cs[:, h:h + 1], (L, L)) - jnp.broadcast_to(cs_t[h:h + 1, :], (L, L))
            w = (scores * jnp.where(causal, jnp.exp(seg), 0.0)).astype(BF16)
            xh = jnp.where(lo, xp, zero) if hh == 0 else jnp.where(lo, zero, xp)
            yh = _dot(w, xh)
            y = yh if y is None else y + yh
        state = state_ref[g]
        y = y + _dot(cg, state.astype(BF16)) * grow[:, gs]
        state_ref[g] = jnp.exp(last_e[:, gs]) * state + _dot(bg.T.astype(BF16), x_dec[:, gs])
        ys.append(y)
    y = jnp.concatenate(ys, axis=1) + xs * dsk_ref[...]
    zz = z_ref[0]
    gg = y * (zz * _sigmoid(zz))
    outs = []
    for g in range(2):
        gp = gg[:, g * LANES:(g + 1) * LANES]
        outs.append(gp * lax.rsqrt(jnp.mean(gp * gp, axis=-1, keepdims=True) + LN_EPS))
    o_ref[0] = jnp.concatenate(outs, axis=1) * ng_ref[...]


def _head_lanes(v):
    return jnp.repeat(v, SSM_HEAD_DIM)[None]


def _ssd(z, xbc, dt, conv_w, conv_b, dt_bias, a_log, d_skip, norm_g):
    bn, t, _ = z.shape
    L = SSM_CHUNK
    first_lanes = lambda v: jnp.zeros((1, LANES), F32).at[0, :SSM_HEADS].set(v)
    e4 = np.zeros((LANES, D_SSM), np.float32)
    for h in range(SSM_HEADS):
        e4[h, h * SSM_HEAD_DIM:(h + 1) * SSM_HEAD_DIM] = 1.0
    full = lambda shape: pl.BlockSpec(shape, lambda bi, c: (0,) * len(shape))
    return pl.pallas_call(
        _ssd_body,
        grid=(bn, t // L),
        in_specs=[pl.BlockSpec((1, L, D_SSM), lambda bi, c: (bi, c, 0)),
                  pl.BlockSpec((1, L, D_XBC), lambda bi, c: (bi, c, 0)),
                  pl.BlockSpec((1, 8, D_XBC), lambda bi, c: (bi, jnp.maximum(c * (L // 8) - 1, 0), 0)),
                  pl.BlockSpec((1, L, LANES), lambda bi, c: (bi, c, 0)),
                  full((SSM_CONV, D_XBC)), full((1, D_XBC)), full((1, LANES)), full((1, LANES)),
                  full((1, D_SSM)), full((1, D_SSM)), full((LANES, D_SSM))],
        out_specs=pl.BlockSpec((1, L, D_SSM), lambda bi, c: (bi, c, 0)),
        out_shape=jax.ShapeDtypeStruct((bn, t, D_SSM), F32),
        scratch_shapes=[pltpu.VMEM((L + 8, D_XBC), F32),
                        pltpu.VMEM((2, SSM_STATE, LANES), F32)],
        compiler_params=_cparams(2, 40),
        name="ssd",
    )(z, xbc, xbc, dt, conv_w, conv_b[None], first_lanes(dt_bias), first_lanes(a_log),
      _head_lanes(d_skip), norm_g[None], jnp.asarray(e4))


def _outproj_body(x_ref, mod_ref, oc_ref, os_ref, ow_ref, gt_ref, e_ref, gm_ref, ssm_ref, w_ref, g_ref, b_ref,
                  o_ref):
    x = x_ref[0]
    m = mod_ref[0]
    gate = m[5:6]
    d_att = ATT_HEADS * HEAD_DIM
    ge = _dot_hi(gt_ref[0], e_ref[...])
    o_att = (ge[:, 0:d_att] * oc_ref[0] + ge[:, d_att:2 * d_att] * os_ref[0]
             + ge[:, 2 * d_att:3 * d_att] * ow_ref[0])
    y = (_dot(o_att.astype(BF16), w_ref[0:d_att, :])
         + _dot(gm_ref[0].astype(BF16), w_ref[d_att:d_att + D_GMLP, :])
         + _dot(ssm_ref[0].astype(BF16), w_ref[d_att + D_GMLP:, :]))
    z = ALPHA * x + (1.0 + gate) * y
    o_ref[0] = _layer_norm(z, g_ref[...], b_ref[...])


def _mixer_outproj(x, mod_l, o_c, o_s, o_w, gates, o_gmlp, o_ssm, w_out, g, b):
    bn, t, d = x.shape
    d_att = ATT_HEADS * HEAD_DIM
    e = np.zeros((LANES, 3 * d_att), np.float32)
    for h in range(ATT_HEADS):
        for br in range(3):
            e[3 * h + br, br * d_att + h * HEAD_DIM:br * d_att + (h + 1) * HEAD_DIM] = 1.0
    row = lambda width: pl.BlockSpec((1, TM, width), lambda bi, i: (bi, i, 0))
    full = lambda shape: pl.BlockSpec(shape, lambda bi, i: (0,) * len(shape))
    return pl.pallas_call(
        _outproj_body,
        grid=(bn, t // TM),
        in_specs=[row(d), pl.BlockSpec((1, N_SUB * 3, d), lambda bi, i: (bi, 0, 0)),
                  row(d_att), row(d_att), row(d_att), row(LANES), full((LANES, 3 * d_att)),
                  row(D_GMLP), row(D_SSM), full((d, d)), full((1, d)), full((1, d))],
        out_specs=row(d),
        out_shape=jax.ShapeDtypeStruct(x.shape, F32),
        compiler_params=_cparams(2, 48),
        name="mixer_outproj",
    )(x, mod_l, o_c, o_s, o_w, gates, jnp.asarray(e), o_gmlp, o_ssm, w_out.astype(BF16), g[None], b[None])


def _nsa(qp, kcv, ksa, vst, kwp, vwp, tables, ovt, cmp_pe, cmp_w1, cmp_b1, cmp_w2):
    nb_c, nb_s, nb_w = tables
    kcp, lhs = _compress(kcv, cmp_pe, cmp_w1, cmp_b1, cmp_w2, ovt)
    o_c, imp_t = _cmp_attention(qp, kcp, lhs, nb_c)
    selneg_t = _select_blocks(imp_t)
    o_s = _sel_attention(qp, selneg_t, ksa, vst, nb_s)
    o_w = _win_attention(qp, kwp, vwp, nb_w)
    return o_c, o_s, o_w


def kernel(x, c, rel_bias, ada_w, ada_b, ln_g, ln_b, ffn_w1, ffn_w3, ffn_w2, w_in, w_out, cmp_pe, cmp_w1, cmp_b1,
           cmp_w2, gmlp_ln_g, gmlp_ln_b, gmlp_ws, gmlp_bs, ssm_conv_w, ssm_conv_b, ssm_dt_bias, ssm_a_log, ssm_d,
           ssm_norm_g):
    t = x.shape[1]
    mod = _adaln_mod(c, ada_w, ada_b)
    tables = _bias_tables(rel_bias)
    ovt = _overlap_matrix(t)
    for l in range(ada_w.shape[0]):
        x = _ffn_sublayer(x, mod[l], ffn_w1[l, 0], ffn_w3[l, 0], ffn_w2[l, 0], ln_g[l, 0], ln_b[l, 0], 0)
        qp, kcv, ksa, vst, kwp, vwp, gates, gm_in, z, xbc, dt = _mixer_inproj(x, mod[l], _relayout_w_in(w_in[l]))
        o_c, o_s, o_w = _nsa(qp, kcv, ksa, vst, kwp, vwp, tables, ovt, cmp_pe[l], cmp_w1[l], cmp_b1[l], cmp_w2[l])
        o_gmlp = _gmlp(gm_in, gmlp_ln_g[l], gmlp_ln_b[l], gmlp_ws[l], gmlp_bs[l])
        o_ssm = _ssd(z, xbc, dt, ssm_conv_w[l], ssm_conv_b[l], ssm_dt_bias[l], ssm_a_log[l], ssm_d[l],
                     ssm_norm_g[l])
        x = _mixer_outproj(x, mod[l], o_c, o_s, o_w, gates, o_gmlp, o_ssm, w_out[l], ln_g[l, 1], ln_b[l, 1])
        x = _ffn_sublayer(x, mod[l], ffn_w1[l, 1], ffn_w3[l, 1], ffn_w2[l, 1], ln_g[l, 2], ln_b[l, 2], 2)
    return x
```

```python
import functools
import math

import numpy as np
import jax
import jax.numpy as jnp
from jax import lax
from jax.experimental import pallas as pl
from jax.experimental.pallas import tpu as pltpu

F32 = jnp.float32
BF16 = jnp.bfloat16
HI = lax.Precision.HIGHEST

D_MODEL = 1024
DEPTH = 2
ATT_HEADS = 8
ATT_KV_GROUPS = 2
ATT_HPG = 4
HEAD_DIM = 64
CMP_LEN = 32
CMP_STRIDE = 16
CMP_HIDDEN = 256
SLC_BLOCK = 64
SLC_TOPN = 16
WINDOW = 512
SEL_FORCE = 1.0e4
REL_BUCKETS = 32
REL_MAX_DIST = 128
GMLP_GROUPS = 4
GMLP_CHUNK = 128
D_GMLP = 256
SSM_HEADS = 4
SSM_HEAD_DIM = 64
D_SSM = 256
SSM_STATE = 128
SSM_CONV = 4
SSM_CHUNK = 256
D_XBC = 768
D_FF = 2816
N_SUB = 3
ALPHA = (2 * DEPTH) ** 0.25
LN_EPS = 1e-5

LANES = 128
LOG2E = math.log2(math.e)
NEG_BIG = -(2.0 ** 100)
KPAD = 512
CMP_KPAD = 16
CMP_CHUNK_LOG2 = 7
CMP_NEAR = 256
CMP_FLAG0 = HEAD_DIM + 1
CMP_SUBTILES = 2
CMP_KEY_SIZES = (512, 768)
TQ_C = 256
TQ = 256
TK = 256
SEL_FAR_TILES = 2
SEL_SUBTILES = 2
SEL_AHEAD = 6
WIN_SUBTILES = 2
TM = 512
FF_CHUNK = 256
SEL_CHUNK_BLOCKS = 128

COL_Q = 0
COL_KV = 1024
COL_GATE = 2560
COL_GMLP = 2688
COL_Z = 3200
COL_XBC = 3456
COL_DT = 4224
N_COLS = 4352


def _cparams(n_axes, vmem_mb):
    return pltpu.CompilerParams(dimension_semantics=("arbitrary",) * n_axes,
                                vmem_limit_bytes=vmem_mb * 1024 * 1024)


def _sigmoid(x):
    return 1.0 / (1.0 + jnp.exp(-x))


def _gelu_tanh(x):
    return x * (0.5 * (1.0 + jnp.tanh(math.sqrt(2.0 / math.pi) * (x + 0.044715 * (x * x * x)))))


def _layer_norm(z, g, b):
    mu = jnp.mean(z, axis=-1, keepdims=True)
    d = z - mu
    var = jnp.mean(d * d, axis=-1, keepdims=True)
    return d * lax.rsqrt(var + LN_EPS) * g + b


def _dot(a, b):
    return jnp.dot(a, b, preferred_element_type=F32)


def _dot_hi(a, b):
    return jnp.dot(a, b, preferred_element_type=F32, precision=HI)


def _dot_nt(a, b):
    return lax.dot_general(a, b, (((1,), (1,)), ((), ())), preferred_element_type=F32)


def _mod_body(c_ref, w_ref, b_ref, o_ref):
    c = c_ref[...]
    o_ref[0] = _dot_hi(c * _sigmoid(c), w_ref[0]) + b_ref[0]


def _adaln_mod(c, ada_w, ada_b):
    n_layers, d, n = ada_w.shape
    bn = c.shape[0]
    tn = 1536
    cp = jnp.zeros((8, d), F32).at[:bn].set(c)
    out = pl.pallas_call(
        _mod_body,
        grid=(n_layers, n // tn),
        in_specs=[pl.BlockSpec((8, d), lambda l, j: (0, 0)),
                  pl.BlockSpec((1, d, tn), lambda l, j: (l, 0, j)),
                  pl.BlockSpec((1, 1, tn), lambda l, j: (l, 0, j))],
        out_specs=pl.BlockSpec((1, 8, tn), lambda l, j: (l, 0, j)),
        out_shape=jax.ShapeDtypeStruct((n_layers, 8, n), F32),
        compiler_params=_cparams(2, 32),
        name="adaln_mod",
    )(cp, ada_w, ada_b.reshape(n_layers, 1, n))
    return out[:, :bn].reshape(n_layers, bn, N_SUB * 3, d)


def _ffn_body(x_ref, mod_ref, w1_ref, w3_ref, w2_ref, g_ref, b_ref, o_ref, *, sub):
    x = x_ref[0]
    m = mod_ref[0]
    shift, scale, gate = m[3 * sub:3 * sub + 1], m[3 * sub + 1:3 * sub + 2], m[3 * sub + 2:3 * sub + 3]
    h = (x * (1.0 + scale) + shift).astype(BF16)
    acc = jnp.zeros(x.shape, F32)
    for j in range(D_FF // FF_CHUNK):
        sl = slice(j * FF_CHUNK, (j + 1) * FF_CHUNK)
        a = _dot(h, w1_ref[:, sl])
        b = _dot(h, w3_ref[:, sl])
        acc = acc + _dot((a * _sigmoid(a) * b).astype(BF16), w2_ref[sl, :])
    z = ALPHA * x + (0.5 * (1.0 + gate)) * acc
    o_ref[0] = _layer_norm(z, g_ref[...], b_ref[...])


def _ffn_sublayer(x, mod_l, w1, w3, w2, g, b, sub):
    bn, t, d = x.shape
    full = lambda shape: pl.BlockSpec(shape, lambda bi, i: (0,) * len(shape))
    return pl.pallas_call(
        functools.partial(_ffn_body, sub=sub),
        grid=(bn, t // TM),
        in_specs=[pl.BlockSpec((1, TM, d), lambda bi, i: (bi, i, 0)),
                  pl.BlockSpec((1, N_SUB * 3, d), lambda bi, i: (bi, 0, 0)),
                  full((d, D_FF)), full((d, D_FF)), full((D_FF, d)),
                  full((1, d)), full((1, d))],
        out_specs=pl.BlockSpec((1, TM, d), lambda bi, i: (bi, i, 0)),
        out_shape=jax.ShapeDtypeStruct(x.shape, F32),
        compiler_params=_cparams(2, 56),
        name="ffn",
    )(x, mod_l, w1.astype(BF16), w3.astype(BF16), w2.astype(BF16), g[None], b[None])


def _pad_cols(w, width):
    return jnp.pad(w, ((0, 0), (0, width - w.shape[1])))


def _relayout_w_in(w_in):
    cols = [_pad_cols(w_in[:, HEAD_DIM * h:HEAD_DIM * (h + 1)], LANES) for h in range(ATT_HEADS)]
    for i in range(6):
        for g in range(ATT_KV_GROUPS):
            base = 512 + 128 * i + HEAD_DIM * g
            cols.append(_pad_cols(w_in[:, base:base + HEAD_DIM], LANES))
    cols.append(_pad_cols(w_in[:, 1280:1304], LANES))
    cols.append(w_in[:, 1304:1816])
    cols.append(w_in[:, 1816:2072])
    cols.append(w_in[:, 2072:2840])
    cols.append(_pad_cols(w_in[:, 2840:2844], LANES))
    return jnp.concatenate(cols, axis=1).astype(BF16)


def _inproj_body(x_ref, mod_ref, w_ref, qp_ref, kcv_ref, ksa_ref, vst_ref, kwp_ref, vwt_ref,
                 gt_ref, gm_ref, z_ref, xbc_ref, dt_ref):
    i = pl.program_id(1)
    x = x_ref[0]
    m = mod_ref[0]
    h = (x * (1.0 + m[4:5]) + m[3:4]).astype(BF16)

    def proj(c0, width):
        return _dot(h, w_ref[:, c0:c0 + width])

    lane = lax.broadcasted_iota(jnp.int32, (1, LANES), 1)
    q_flag = jnp.where(lane == HEAD_DIM, NEG_BIG, 0.0)
    for hh in range(ATT_HEADS):
        qp_ref[0, hh] = (proj(COL_Q + LANES * hh, LANES) * (HEAD_DIM ** -0.5 * LOG2E) + q_flag).astype(BF16)
    for k in range(4):
        kcv_ref[0, k] = proj(COL_KV + LANES * k, LANES).astype(BF16)
    gt_ref[0] = _sigmoid(proj(COL_GATE, LANES))
    gm_ref[0] = proj(COL_GMLP, 512)
    z_ref[0] = proj(COL_Z, D_SSM)
    xbc_ref[0] = proj(COL_XBC, D_XBC)
    dt_ref[0] = proj(COL_DT, LANES)

    @pl.when(i == 0)
    def _():
        dummy = jnp.broadcast_to(jnp.where(lane == HEAD_DIM, 1.0, 0.0), (TM, LANES)).astype(BF16)
        zeros = jnp.zeros((TM, LANES), BF16)
        for g in range(ATT_KV_GROUPS):
            ksa_ref[0, g, :, 0:LANES] = dummy
            ksa_ref[0, g, :, LANES:2 * LANES] = zeros
            kwp_ref[0, g] = dummy
            vst_ref[0, g] = jnp.zeros((LANES, TM), BF16)
            vwt_ref[0, g] = jnp.zeros((LANES, TM), BF16)

    @pl.when(i > 0)
    def _():
        pos = (i - 1) * TM + lax.broadcasted_iota(jnp.int32, (TM, LANES), 0)
        blk_lane = (pos >> 6) & (SEL_CHUNK_BLOCKS - 1)
        onehot = jnp.where(blk_lane == lax.broadcasted_iota(jnp.int32, (TM, LANES), 1), 1.0, 0.0).astype(BF16)
        ones_row = lax.broadcasted_iota(jnp.int32, (LANES, 1), 0) == HEAD_DIM
        for g in range(ATT_KV_GROUPS):
            ksa_ref[0, g, :, 0:LANES] = proj(COL_KV + LANES * (4 + g), LANES).astype(BF16)
            ksa_ref[0, g, :, LANES:2 * LANES] = onehot
            vst_ref[0, g] = jnp.where(ones_row, 1.0, proj(COL_KV + LANES * (6 + g), LANES).T).astype(BF16)
            kwp_ref[0, g] = proj(COL_KV + LANES * (8 + g), LANES).astype(BF16)
            vwt_ref[0, g] = jnp.where(ones_row, 1.0, proj(COL_KV + LANES * (10 + g), LANES).T).astype(BF16)


def _mixer_inproj(x, mod_l, w_in):
    bn, t, d = x.shape
    tp = t + KPAD
    row = lambda width: pl.BlockSpec((1, TM, width), lambda bi, i: (bi, jnp.maximum(i - 1, 0), 0))
    sds = jax.ShapeDtypeStruct
    return pl.pallas_call(
        _inproj_body,
        grid=(bn, t // TM + 1),
        in_specs=[row(d),
                  pl.BlockSpec((1, N_SUB * 3, d), lambda bi, i: (bi, 0, 0)),
                  pl.BlockSpec((d, N_COLS), lambda bi, i: (0, 0))],
        out_specs=[pl.BlockSpec((1, ATT_HEADS, TM, LANES), lambda bi, i: (bi, 0, jnp.maximum(i - 1, 0), 0)),
                   pl.BlockSpec((1, 4, TM, LANES), lambda bi, i: (bi, 0, jnp.maximum(i - 1, 0), 0)),
                   pl.BlockSpec((1, ATT_KV_GROUPS, TM, 2 * LANES), lambda bi, i: (bi, 0, i, 0)),
                   pl.BlockSpec((1, ATT_KV_GROUPS, LANES, TM), lambda bi, i: (bi, 0, 0, i)),
                   pl.BlockSpec((1, ATT_KV_GROUPS, TM, LANES), lambda bi, i: (bi, 0, i, 0)),
                   pl.BlockSpec((1, ATT_KV_GROUPS, LANES, TM), lambda bi, i: (bi, 0, 0, i)),
                   row(LANES), row(512), row(D_SSM), row(D_XBC), row(LANES)],
        out_shape=[sds((bn, ATT_HEADS, t, LANES), BF16),
                   sds((bn, 4, t, LANES), BF16),
                   sds((bn, ATT_KV_GROUPS, tp, 2 * LANES), BF16),
                   sds((bn, ATT_KV_GROUPS, LANES, tp), BF16),
                   sds((bn, ATT_KV_GROUPS, tp, LANES), BF16),
                   sds((bn, ATT_KV_GROUPS, LANES, tp), BF16),
                   sds((bn, t, LANES), F32), sds((bn, t, 512), F32), sds((bn, t, D_SSM), F32),
                   sds((bn, t, D_XBC), F32), sds((bn, t, LANES), F32)],
        compiler_params=_cparams(2, 48),
        name="mixer_inproj",
    )(x, mod_l, w_in)


def _cmp_rows(t):
    return -(-(t // CMP_STRIDE + 120) // LANES) * LANES


def _compress_body(ck_ref, cv_ref, w1_ref, pe_ref, b1_ref, w2_ref, ovt_ref, k_ref, lhs_ref):
    n_rows = ck_ref.shape[2]
    rows = k_ref.shape[2]
    n_cmp = n_rows - CMP_LEN // CMP_STRIDE + 1
    half = CMP_STRIDE * HEAD_DIM
    row = lax.broadcasted_iota(jnp.int32, (rows, LANES), 0)
    lane = lax.broadcasted_iota(jnp.int32, (rows, LANES), 1)
    real = (row >= CMP_KPAD) & (row < CMP_KPAD + n_cmp)

    def mlp(c, j):
        w1 = w1_ref[j]
        h1 = _dot(c, w1[:half])
        h2 = _dot(c, w1[half:])
        cvec = _dot(pe_ref[j], w1)[0:1] + b1_ref[j]
        hid = h1 + pltpu.roll(h2, n_rows - 1, 0) + cvec
        out = _dot(_gelu_tanh(hid).astype(BF16), w2_ref[j])
        tail = rows - CMP_KPAD - n_rows
        full = jnp.concatenate([jnp.zeros((CMP_KPAD, LANES), F32), out, jnp.zeros((tail, LANES), F32)], axis=0)
        return jnp.where(real, full, 0.0)

    flags = jnp.where(lane == HEAD_DIM, jnp.where(real, 0.0, 1.0),
                      jnp.where(lane - CMP_FLAG0 == (row >> CMP_CHUNK_LOG2), 1.0, 0.0))
    k_ref[0, 0] = (mlp(ck_ref[0, 0], 0) + flags).astype(BF16)
    ones_row = lax.broadcasted_iota(jnp.int32, (LANES, 1), 0) == HEAD_DIM
    lhs_ref[0, 0, 0:LANES, :] = jnp.where(ones_row, 1.0, mlp(cv_ref[0, 0], 1).T).astype(BF16)
    lhs_ref[0, 0, LANES:, :] = ovt_ref[...].astype(BF16)


def _compress(kcv, cmp_pe, cmp_w1, cmp_b1, cmp_w2, ovt):
    bn, _, t, _ = kcv.shape
    n_rows = t // CMP_STRIDE
    width = CMP_STRIDE * HEAD_DIM
    c = kcv[..., :HEAD_DIM].reshape(bn, 4, n_rows, width)
    pe = jnp.zeros((2, 8, CMP_LEN * HEAD_DIM), F32).at[:, 0].set(cmp_pe.reshape(2, -1)).astype(BF16)
    w2 = jnp.pad(cmp_w2, ((0, 0), (0, 0), (0, LANES - HEAD_DIM))).astype(BF16)
    n_slc, rows = ovt.shape
    full = lambda shape: pl.BlockSpec(shape, lambda bi, g: (0,) * len(shape))
    return pl.pallas_call(
        _compress_body,
        grid=(bn, ATT_KV_GROUPS),
        in_specs=[pl.BlockSpec((1, 1, n_rows, width), lambda bi, g: (bi, g, 0, 0)),
                  pl.BlockSpec((1, 1, n_rows, width), lambda bi, g: (bi, ATT_KV_GROUPS + g, 0, 0)),
                  full((2, CMP_LEN * HEAD_DIM, CMP_HIDDEN)), full((2, 8, CMP_LEN * HEAD_DIM)),
                  full((2, 1, CMP_HIDDEN)), full((2, CMP_HIDDEN, LANES)), full((n_slc, rows))],
        out_specs=[pl.BlockSpec((1, 1, rows, LANES), lambda bi, g: (bi, g, 0, 0)),
                   pl.BlockSpec((1, 1, LANES + n_slc, rows), lambda bi, g: (bi, g, 0, 0))],
        out_shape=[jax.ShapeDtypeStruct((bn, ATT_KV_GROUPS, rows, LANES), BF16),
                   jax.ShapeDtypeStruct((bn, ATT_KV_GROUPS, LANES + n_slc, rows), BF16)],
        compiler_params=_cparams(2, 40),
        name="nsa_compress",
    )(c, c, cmp_w1.astype(BF16), pe, cmp_b1[:, None, :], w2, ovt)


def _rel_bucket(dist):
    n = jnp.maximum(dist, 0)
    max_exact = REL_BUCKETS // 2
    nf = jnp.maximum(n, 1).astype(F32)
    large = max_exact + (jnp.log(nf / max_exact) / math.log(REL_MAX_DIST / max_exact)
                         * (REL_BUCKETS - max_exact)).astype(jnp.int32)
    large = jnp.minimum(large, REL_BUCKETS - 1)
    return jnp.where(n < max_exact, n, large)


def _bias_tables(rel_bias):
    far = rel_bias[REL_BUCKETS - 1]

    def table(dist, valid, shift):
        onehot = (_rel_bucket(dist)[..., None] == jnp.arange(REL_BUCKETS)).astype(F32)
        b = jnp.einsum("qkb,bh->hqk", onehot, rel_bias, precision=HI)
        if shift:
            b = b - far[:, None, None]
        return jnp.where(valid[None], b * LOG2E, -jnp.inf)

    n_var = (1 << CMP_CHUNK_LOG2) * CMP_STRIDE // TQ_C
    d_c = jnp.asarray(TQ_C * np.arange(n_var)[:, None, None] + np.arange(TQ_C)[None, None, :]
                      - CMP_STRIDE * np.arange(CMP_NEAR)[None, :, None]
                      + (CMP_STRIDE * CMP_KPAD - CMP_LEN + 1))
    nb_c = jnp.stack([table(d_c[v], d_c[v] >= 0, True) for v in range(n_var)])
    ki = np.arange(2 * TK)[:, None]
    d_s = jnp.asarray(np.arange(TQ)[None, :] - ki + TK)
    nb_s = table(d_s, d_s >= 0, True)
    d_w = jnp.asarray(np.arange(TQ)[None, :] - np.arange(TQ + WINDOW)[:, None] + WINDOW)
    nb_w = table(d_w, (d_w >= 0) & (d_w < WINDOW), False)
    return nb_c, nb_s, nb_w


def _overlap_matrix(t):
    n_cmp = (t - CMP_LEN) // CMP_STRIDE + 1
    n_slc = t // SLC_BLOCK
    cs = np.arange(n_cmp) * CMP_STRIDE
    ss = np.arange(n_slc) * SLC_BLOCK
    ov = np.clip(np.minimum(cs[:, None] + CMP_LEN, ss[None, :] + SLC_BLOCK)
                 - np.maximum(cs[:, None], ss[None, :]), 0, None) / CMP_LEN
    full = np.zeros((n_slc, _cmp_rows(t)), np.float32)
    full[:, CMP_KPAD:CMP_KPAD + n_cmp] = ov.T
    return jnp.asarray(full)


def _cmp_attn_body(q_ref, k_ref, lhs_ref, nba_ref, nbb_ref, o_ref, imp_ref, s_ref):
    step = pl.program_id(2)
    rows = k_ref.shape[2]
    chunk = 1 << CMP_CHUNK_LOG2
    nb_refs = (nba_ref, nbb_ref)
    rowi = lax.broadcasted_iota(jnp.int32, (LANES, TQ_C), 0)

    def near_chunk_of(sub):
        return ((CMP_SUBTILES * step + sub) * (TQ_C // CMP_STRIDE)) >> CMP_CHUNK_LOG2

    def body(n_keys):
        kf = k_ref[0, 0, 0:n_keys, :]
        lhs = lhs_ref[0, 0, :, 0:n_keys]
        units = []
        for sub in range(CMP_SUBTILES):
            near_chunk = near_chunk_of(sub)
            chunk_flag = jnp.where((rowi >= CMP_FLAG0) & (rowi - CMP_FLAG0 > near_chunk + 1), NEG_BIG, 0.0)
            a0 = pl.multiple_of(near_chunk << CMP_CHUNK_LOG2, chunk)
            for r in range(ATT_HPG):
                q_t = (q_ref[0, r, sub * TQ_C:(sub + 1) * TQ_C, :].astype(F32).T + chunk_flag).astype(BF16)
                units.append((sub, r, q_t, a0))
        for u, (_, _, q_t, _) in enumerate(units[:SEL_AHEAD]):
            s_ref[u, 0:n_keys, :] = _dot(kf, q_t)
        imps = [jnp.zeros((imp_ref.shape[2], TQ_C), F32) for _ in range(CMP_SUBTILES)]
        outs = []
        for u, (sub, r, _, a0) in enumerate(units):
            if u + SEL_AHEAD < len(units):
                s_ref[u + SEL_AHEAD, 0:n_keys, :] = _dot(kf, units[u + SEL_AHEAD][2])
            s_ref[u, pl.ds(a0, CMP_NEAR), :] += nb_refs[sub][0, r]
            s = s_ref[u, 0:n_keys, :]
            mx = jnp.max(s, axis=0, keepdims=True)
            mx = jnp.where(mx < 0.5 * NEG_BIG, 0.0, mx)
            p = jnp.exp2(s - mx).astype(BF16)
            res = _dot(lhs, p)
            inv = 1.0 / jnp.maximum(res[HEAD_DIM:HEAD_DIM + 1, :], 1e-30)
            outs.append((res[0:LANES, :] * inv).T[:, :HEAD_DIM])
            imps[sub] = imps[sub] + res[LANES:, :] * inv
        for sub in range(CMP_SUBTILES):
            qs = slice(sub * TQ_C, (sub + 1) * TQ_C)
            o_ref[0, qs, :] = jnp.concatenate(outs[ATT_HPG * sub:ATT_HPG * (sub + 1)], axis=1)
            imp_ref[0, 0, :, qs] = imps[sub]

    need = (near_chunk_of(CMP_SUBTILES - 1) + 2) * chunk
    sizes = sorted({min(rows, s) for s in CMP_KEY_SIZES} | {rows})
    lo = 0
    for n_keys in sizes:
        @pl.when((need > lo) & (need <= n_keys))
        def _(n_keys=n_keys):
            body(n_keys)
        lo = n_keys


def _cmp_attention(qp, kcp, lhs, nb_c):
    bn, _, t, _ = qp.shape
    rows = kcp.shape[2]
    n_slc = lhs.shape[2] - LANES
    n_var = nb_c.shape[0]
    tq = CMP_SUBTILES * TQ_C
    table = lambda sub: pl.BlockSpec((1, ATT_HPG, CMP_NEAR, TQ_C),
                                     lambda bi, g, m: ((CMP_SUBTILES * m + sub) % n_var, g, 0, 0))
    return pl.pallas_call(
        _cmp_attn_body,
        grid=(bn, ATT_KV_GROUPS, t // tq),
        in_specs=[pl.BlockSpec((1, ATT_HPG, tq, LANES), lambda bi, g, m: (bi, g, m, 0)),
                  pl.BlockSpec((1, 1, rows, LANES), lambda bi, g, m: (bi, g, 0, 0)),
                  pl.BlockSpec((1, 1, LANES + n_slc, rows), lambda bi, g, m: (bi, g, 0, 0)),
                  table(0), table(1)],
        out_specs=[pl.BlockSpec((1, tq, ATT_HPG * HEAD_DIM), lambda bi, g, m: (bi, m, g)),
                   pl.BlockSpec((1, 1, n_slc, tq), lambda bi, g, m: (bi, g, 0, m))],
        out_shape=[jax.ShapeDtypeStruct((bn, t, ATT_HEADS * HEAD_DIM), F32),
                   jax.ShapeDtypeStruct((bn, ATT_KV_GROUPS, n_slc, t), F32)],
        scratch_shapes=[pltpu.VMEM((CMP_SUBTILES * ATT_HPG, rows, TQ_C), F32)],
        compiler_params=_cparams(3, 48),
        name="nsa_cmp_attn",
    )(qp, kcp, lhs, nb_c, nb_c)


TOPK_COLS = 1024


def _topk_body(imp_ref, o_ref):
    i = pl.program_id(2)
    imp = imp_ref[0, 0]
    n_slc, cols = imp.shape
    t = i * cols + lax.broadcasted_iota(jnp.int32, (1, cols), 1)
    cur = t >> 6
    blk = lax.broadcasted_iota(jnp.int32, (n_slc, cols), 0)
    forced = (blk == 0) | (blk == cur) | (blk == cur - 1)
    vals = jnp.where(forced, SEL_FORCE, jnp.where(blk <= cur, imp, -SEL_FORCE))
    blkf = blk.astype(F32)
    for _ in range(min(SLC_TOPN, n_slc)):
        mx = jnp.max(vals, axis=0, keepdims=True)
        first = jnp.min(jnp.where(vals == mx, blkf, float(n_slc)), axis=0, keepdims=True)
        vals = jnp.where(blkf == first, -jnp.inf, vals)
    o_ref[0, 0] = jnp.where(vals == -jnp.inf, 0.0, NEG_BIG).astype(BF16)


def _select_blocks(imp_t):
    bn, g, n_slc, t = imp_t.shape
    cols = min(TOPK_COLS, t)
    spec = pl.BlockSpec((1, 1, n_slc, cols), lambda bi, gi, i: (bi, gi, 0, i))
    return pl.pallas_call(
        _topk_body,
        grid=(bn, g, t // cols),
        in_specs=[spec],
        out_specs=spec,
        out_shape=jax.ShapeDtypeStruct(imp_t.shape, BF16),
        compiler_params=_cparams(3, 40),
        name="nsa_topk",
    )(imp_t)


def _sel_attn_body(q_ref, sel_ref, k_ref, vt_ref, nb_ref, o_ref, qat_ref, m_ref, acc_ref, sc0_ref, sc1_ref):
    sc_refs = (sc0_ref, sc1_ref)
    pair = pl.program_id(2)
    n_chunks = qat_ref.shape[0]
    m_a = SEL_SUBTILES * pair
    for sub in range(SEL_SUBTILES):
        qs = slice(sub * TQ, (sub + 1) * TQ)
        for r in range(ATT_HPG):
            q_t = q_ref[0, r, qs, :].astype(F32).T.astype(BF16)
            for c in range(n_chunks):
                qat_ref[c, ATT_HPG * sub + r, 0:LANES, :] = q_t
        for c in range(n_chunks):
            sel_t = sel_ref[0, 0, c * LANES:(c + 1) * LANES, qs]
            for r in range(ATT_HPG):
                qat_ref[c, ATT_HPG * sub + r, LANES:2 * LANES, :] = sel_t
    m_ref[...] = jnp.full(m_ref.shape, -jnp.inf, F32)
    acc_ref[...] = jnp.zeros(acc_ref.shape, F32)

    tiles_per_chunk = SEL_CHUNK_BLOCKS * SLC_BLOCK // TK
    units_a = tuple(range(ATT_HPG))
    units_b = tuple(range(ATT_HPG, 2 * ATT_HPG))
    bias_hi = lambda r: nb_ref[r, TK:2 * TK, :]
    bias_lo = lambda r: nb_ref[r, 0:TK, :]

    def load(j, n_tiles):
        row0 = pl.multiple_of(KPAD + TK * j, TK)
        return (k_ref[0, 0, pl.ds(row0, n_tiles * TK), :], vt_ref[0, 0, :, pl.ds(row0, n_tiles * TK)],
                jnp.maximum(j, 0) // tiles_per_chunk)

    def qk(tile, u):
        kt, _, c = tile
        return _dot(kt, qat_ref[c, u])

    def update(tile, u, s, bias):
        if bias is not None:
            s = s + bias(u % ATT_HPG)
        m_prev = m_ref[u]
        m_new = jnp.maximum(m_prev, jnp.max(s, axis=0, keepdims=True))
        alpha = jnp.exp2(m_prev - m_new)
        p = jnp.exp2(s - m_new).astype(BF16)
        acc_ref[u] = alpha * acc_ref[u] + _dot(tile[1], p)
        m_ref[u] = m_new

    def step(work):
        todo = [(tile, u, bias) for tile, units, bias in work for u in units]
        scores = [qk(tile, u) for tile, u, _ in todo[:SEL_AHEAD]]
        for i, (tile, u, bias) in enumerate(todo):
            if i + SEL_AHEAD < len(todo):
                scores.append(qk(todo[i + SEL_AHEAD][0], todo[i + SEL_AHEAD][1]))
            s = scores[i]
            scores[i] = None
            update(tile, u, s, bias)

    def pipelined_step(tile, slot, next_tile):
        for i, u in enumerate(all_units):
            if next_tile is not None:
                for k in range(2 * i, min(2 * i + 2, len(all_units))):
                    sc_refs[1 - slot][k] = qk(next_tile, all_units[k])
            update(tile, u, sc_refs[slot][i], None)

    all_units = units_a + units_b
    step([(load(m_a, 1), units_a, bias_hi), (load(m_a + 1, 1), units_b, bias_hi)])
    step([(load(m_a - 1, 1), units_a, bias_lo), (load(m_a, 1), units_b, bias_lo)])
    step([(load(m_a - 1, 1), units_b, None)])

    n_common = jnp.maximum(m_a - 1, 0)
    n_big = n_common // SEL_FAR_TILES
    far_tile = lambda i: load(SEL_FAR_TILES * i, SEL_FAR_TILES)

    @pl.when(n_big > 0)
    def _():
        first = far_tile(0)
        for k, u in enumerate(all_units):
            sc_refs[0][k] = qk(first, u)
        n_pairs = (n_big - 1) // 2

        def far_pair(i, carry):
            pipelined_step(far_tile(2 * i), 0, far_tile(2 * i + 1))
            pipelined_step(far_tile(2 * i + 1), 1, far_tile(2 * i + 2))
            return carry

        lax.fori_loop(0, n_pairs, far_pair, 0)

        @pl.when(n_big - 2 * n_pairs == 1)
        def _():
            pipelined_step(far_tile(n_big - 1), 0, None)

        @pl.when(n_big - 2 * n_pairs == 2)
        def _():
            pipelined_step(far_tile(n_big - 2), 0, far_tile(n_big - 1))
            pipelined_step(far_tile(n_big - 1), 1, None)

    def far_one(j, carry):
        step([(load(j, 1), all_units, None)])
        return carry

    lax.fori_loop(SEL_FAR_TILES * n_big, n_common, far_one, 0)
    for sub in range(SEL_SUBTILES):
        outs = []
        for r in range(ATT_HPG):
            acc = acc_ref[ATT_HPG * sub + r]
            outs.append((acc / acc[HEAD_DIM:HEAD_DIM + 1, :]).T[:, :HEAD_DIM])
        o_ref[0, sub * TQ:(sub + 1) * TQ, :] = jnp.concatenate(outs, axis=1)


def _sel_attention(qp, selneg, ksa, vst, nb_s):
    bn, _, t, _ = qp.shape
    tp = ksa.shape[2]
    n_slc = selneg.shape[2]
    n_chunks = max(n_slc // SEL_CHUNK_BLOCKS, 1)
    tq = SEL_SUBTILES * TQ
    n_units = SEL_SUBTILES * ATT_HPG
    return pl.pallas_call(
        _sel_attn_body,
        grid=(bn, ATT_KV_GROUPS, t // tq),
        in_specs=[pl.BlockSpec((1, ATT_HPG, tq, LANES), lambda bi, g, m: (bi, g, m, 0)),
                  pl.BlockSpec((1, 1, n_slc, tq), lambda bi, g, m: (bi, g, 0, m)),
                  pl.BlockSpec((1, 1, tp, 2 * LANES), lambda bi, g, m: (bi, g, 0, 0)),
                  pl.BlockSpec((1, 1, LANES, tp), lambda bi, g, m: (bi, g, 0, 0)),
                  pl.BlockSpec((ATT_HPG, 2 * TK, TQ), lambda bi, g, m: (g, 0, 0))],
        out_specs=pl.BlockSpec((1, tq, ATT_HPG * HEAD_DIM), lambda bi, g, m: (bi, m, g)),
        out_shape=jax.ShapeDtypeStruct((bn, t, ATT_HEADS * HEAD_DIM), F32),
        scratch_shapes=[pltpu.VMEM((n_chunks, n_units, 2 * LANES, TQ), BF16),
                        pltpu.VMEM((n_units, 1, TQ), F32),
                        pltpu.VMEM((n_units, LANES, TQ), F32),
                        pltpu.VMEM((n_units, SEL_FAR_TILES * TK, TQ), F32),
                        pltpu.VMEM((n_units, SEL_FAR_TILES * TK, TQ), F32)],
        compiler_params=_cparams(3, 56),
        name="nsa_sel_attn",
    )(qp, selneg, ksa, vst, nb_s)


def _win_attn_body(q_ref, k_ref, vt_ref, nb_ref, o_ref):
    step = pl.program_id(2)
    keys = TQ + WINDOW
    units = []
    for sub in range(WIN_SUBTILES):
        row0 = pl.multiple_of((WIN_SUBTILES * step + sub) * TQ, TQ)
        kt = k_ref[0, 0, pl.ds(row0, keys), :]
        vt = vt_ref[0, 0, :, pl.ds(row0, keys)]
        for r in range(ATT_HPG):
            q_t = q_ref[0, r, sub * TQ:(sub + 1) * TQ, :].astype(F32).T.astype(BF16)
            units.append((kt, vt, q_t, r))
    scores = [_dot(kt, q_t) for kt, _, q_t, _ in units[:SEL_AHEAD]]
    outs = []
    for i, (_, vt, _, r) in enumerate(units):
        if i + SEL_AHEAD < len(units):
            scores.append(_dot(units[i + SEL_AHEAD][0], units[i + SEL_AHEAD][2]))
        s = scores[i] + nb_ref[r]
        scores[i] = None
        p = jnp.exp2(s - jnp.max(s, axis=0, keepdims=True)).astype(BF16)
        res = _dot(vt, p)
        outs.append((res / res[HEAD_DIM:HEAD_DIM + 1, :]).T[:, :HEAD_DIM])
    for sub in range(WIN_SUBTILES):
        o_ref[0, sub * TQ:(sub + 1) * TQ, :] = jnp.concatenate(outs[ATT_HPG * sub:ATT_HPG * (sub + 1)], axis=1)


def _win_attention(qp, kwp, vwt, nb_w):
    bn, _, t, _ = qp.shape
    tp = kwp.shape[2]
    tq = WIN_SUBTILES * TQ
    return pl.pallas_call(
        _win_attn_body,
        grid=(bn, ATT_KV_GROUPS, t // tq),
        in_specs=[pl.BlockSpec((1, ATT_HPG, tq, LANES), lambda bi, g, m: (bi, g, m, 0)),
                  pl.BlockSpec((1, 1, tp, LANES), lambda bi, g, m: (bi, g, 0, 0)),
                  pl.BlockSpec((1, 1, LANES, tp), lambda bi, g, m: (bi, g, 0, 0)),
                  pl.BlockSpec((ATT_HPG, TQ + WINDOW, TQ), lambda bi, g, m: (g, 0, 0))],
        out_specs=pl.BlockSpec((1, tq, ATT_HPG * HEAD_DIM), lambda bi, g, m: (bi, m, g)),
        out_shape=jax.ShapeDtypeStruct((bn, t, ATT_HEADS * HEAD_DIM), F32),
        compiler_params=_cparams(3, 48),
        name="nsa_win_attn",
    )(qp, kwp, vwt, nb_w)


GMLP_ROWS = 1024


def _gmlp_body(uv_ref, g_ref, b_ref, ws_ref, bs_ref, o_ref):
    act = _gelu_tanh(uv_ref[0])
    u = act[:, :D_GMLP]
    v = _layer_norm(act[:, D_GMLP:], g_ref[...], b_ref[...])
    ti = lax.broadcasted_iota(jnp.int32, (GMLP_CHUNK, GMLP_CHUNK), 0)
    si = lax.broadcasted_iota(jnp.int32, (GMLP_CHUNK, GMLP_CHUNK), 1)
    w = [jnp.where(ti >= si, ws_ref[g], 0.0).astype(BF16) for g in range(GMLP_GROUPS)]
    lane = lax.broadcasted_iota(jnp.int32, (1, LANES), 1)
    lo = lane < HEAD_DIM
    for c in range(uv_ref.shape[1] // GMLP_CHUNK):
        rs = slice(c * GMLP_CHUNK, (c + 1) * GMLP_CHUNK)
        pairs = []
        for pr in range(GMLP_GROUPS // 2):
            vp = v[rs, pr * LANES:(pr + 1) * LANES]
            pairs.append(_dot(w[2 * pr], jnp.where(lo, vp, 0.0).astype(BF16))
                         + _dot(w[2 * pr + 1], jnp.where(lo, 0.0, vp).astype(BF16)))
        o_ref[0, rs, :] = u[rs] * (jnp.concatenate(pairs, axis=1) + bs_ref[...])


def _gmlp(uv, ln_g, ln_b, ws, bs):
    bn, t, _ = uv.shape
    rows = min(GMLP_ROWS, t)
    bs_lanes = jnp.repeat(bs.T, D_GMLP // GMLP_GROUPS, axis=1)
    full = lambda shape: pl.BlockSpec(shape, lambda bi, i: (0,) * len(shape))
    return pl.pallas_call(
        _gmlp_body,
        grid=(bn, t // rows),
        in_specs=[pl.BlockSpec((1, rows, 2 * D_GMLP), lambda bi, i: (bi, i, 0)),
                  full((1, D_GMLP)), full((1, D_GMLP)),
                  full((GMLP_GROUPS, GMLP_CHUNK, GMLP_CHUNK)), full((GMLP_CHUNK, D_GMLP))],
        out_specs=pl.BlockSpec((1, rows, D_GMLP), lambda bi, i: (bi, i, 0)),
        out_shape=jax.ShapeDtypeStruct((bn, t, D_GMLP), F32),
        compiler_params=_cparams(2, 40),
        name="gmlp",
    )(uv, ln_g[None], ln_b[None], ws, bs_lanes)


def _ssd_body(z_ref, xbc_ref, prev_ref, dt_ref, cw_ref, cb_ref, dtb_ref, alog_ref, dsk_ref, ng_ref, e4_ref,
              o_ref, xe_ref, state_ref):
    c = pl.program_id(1)
    L = SSM_CHUNK

    @pl.when(c == 0)
    def _():
        state_ref[...] = jnp.zeros(state_ref.shape, F32)

    xe_ref[0:8] = jnp.where(c > 0, prev_ref[0], 0.0)
    xe_ref[8:8 + L] = xbc_ref[0]
    conv = cb_ref[...]
    for k in range(SSM_CONV):
        conv = conv + cw_ref[k:k + 1, :] * xe_ref[pl.ds(8 - (SSM_CONV - 1) + k, L), :]
    xc = conv * _sigmoid(conv)
    xs, bm, cm = xc[:, :D_SSM], xc[:, D_SSM:D_SSM + 2 * SSM_STATE], xc[:, D_SSM + 2 * SSM_STATE:]

    dtr = dt_ref[0] + dtb_ref[...]
    dt = jnp.maximum(dtr, 0.0) + jnp.log1p(jnp.exp(-jnp.abs(dtr)))
    a = -jnp.exp(alog_ref[...]) * dt
    ti = lax.broadcasted_iota(jnp.int32, (L, L), 0)
    si = lax.broadcasted_iota(jnp.int32, (L, L), 1)
    causal = ti >= si
    cs = _dot_hi(jnp.where(causal, 1.0, 0.0), a)
    cs_t = cs.T
    e4 = e4_ref[...]
    dt_e = _dot_hi(dt, e4)
    cs_e = _dot_hi(cs, e4)
    last_e = cs_e[L - 1:L, :]
    x_dt = xs * dt_e
    x_dec = (x_dt * jnp.exp(last_e - cs_e)).astype(BF16)
    x_dt = x_dt.astype(BF16)
    grow = jnp.exp(cs_e)
    lane = lax.broadcasted_iota(jnp.int32, (1, LANES), 1)
    lo = lane < SSM_HEAD_DIM
    zero = jnp.zeros((), BF16)

    ys = []
    for g in range(2):
        gs = slice(g * LANES, (g + 1) * LANES)
        cg = cm[:, gs].astype(BF16)
        bg = bm[:, gs]
        scores = _dot_nt(cg, bg.astype(BF16))
        xp = x_dt[:, gs]
        y = None
        for hh in range(2):
            h = 2 * g + hh
            seg = jnp.broadcast_to(cs[:, h:h + 1], (L, L)) - jnp.broadcast_to(cs_t[h:h + 1, :], (L, L))
            w = (scores * jnp.where(causal, jnp.exp(seg), 0.0)).astype(BF16)
            xh = jnp.where(lo, xp, zero) if hh == 0 else jnp.where(lo, zero, xp)
            yh = _dot(w, xh)
            y = yh if y is None else y + yh
        state = state_ref[g]
        y = y + _dot(cg, state.astype(BF16)) * grow[:, gs]
        state_ref[g] = jnp.exp(last_e[:, gs]) * state + _dot(bg.T.astype(BF16), x_dec[:, gs])
        ys.append(y)
    y = jnp.concatenate(ys, axis=1) + xs * dsk_ref[...]
    zz = z_ref[0]
    gg = y * (zz * _sigmoid(zz))
    outs = []
    for g in range(2):
        gp = gg[:, g * LANES:(g + 1) * LANES]
        outs.append(gp * lax.rsqrt(jnp.mean(gp * gp, axis=-1, keepdims=True) + LN_EPS))
    o_ref[0] = jnp.concatenate(outs, axis=1) * ng_ref[...]


def _head_lanes(v):
    return jnp.repeat(v, SSM_HEAD_DIM)[None]


def _ssd(z, xbc, dt, conv_w, conv_b, dt_bias, a_log, d_skip, norm_g):
    bn, t, _ = z.shape
    L = SSM_CHUNK
    first_lanes = lambda v: jnp.zeros((1, LANES), F32).at[0, :SSM_HEADS].set(v)
    e4 = np.zeros((LANES, D_SSM), np.float32)
    for h in range(SSM_HEADS):
        e4[h, h * SSM_HEAD_DIM:(h + 1) * SSM_HEAD_DIM] = 1.0
    full = lambda shape: pl.BlockSpec(shape, lambda bi, c: (0,) * len(shape))
    return pl.pallas_call(
        _ssd_body,
        grid=(bn, t // L),
        in_specs=[pl.BlockSpec((1, L, D_SSM), lambda bi, c: (bi, c, 0)),
                  pl.BlockSpec((1, L, D_XBC), lambda bi, c: (bi, c, 0)),
                  pl.BlockSpec((1, 8, D_XBC), lambda bi, c: (bi, jnp.maximum(c * (L // 8) - 1, 0), 0)),
                  pl.BlockSpec((1, L, LANES), lambda bi, c: (bi, c, 0)),
                  full((SSM_CONV, D_XBC)), full((1, D_XBC)), full((1, LANES)), full((1, LANES)),
                  full((1, D_SSM)), full((1, D_SSM)), full((LANES, D_SSM))],
        out_specs=pl.BlockSpec((1, L, D_SSM), lambda bi, c: (bi, c, 0)),
        out_shape=jax.ShapeDtypeStruct((bn, t, D_SSM), F32),
        scratch_shapes=[pltpu.VMEM((L + 8, D_XBC), F32),
                        pltpu.VMEM((2, SSM_STATE, LANES), F32)],
        compiler_params=_cparams(2, 40),
        name="ssd",
    )(z, xbc, xbc, dt, conv_w, conv_b[None], first_lanes(dt_bias), first_lanes(a_log),
      _head_lanes(d_skip), norm_g[None], jnp.asarray(e4))


def _outproj_body(x_ref, mod_ref, oc_ref, os_ref, ow_ref, gt_ref, e_ref, gm_ref, ssm_ref, w_ref, g_ref, b_ref,
                  o_ref):
    x = x_ref[0]
    m = mod_ref[0]
    gate = m[5:6]
    d_att = ATT_HEADS * HEAD_DIM
    ge = _dot_hi(gt_ref[0], e_ref[...])
    o_att = (ge[:, 0:d_att] * oc_ref[0] + ge[:, d_att:2 * d_att] * os_ref[0]
             + ge[:, 2 * d_att:3 * d_att] * ow_ref[0])
    y = (_dot(o_att.astype(BF16), w_ref[0:d_att, :])
         + _dot(gm_ref[0].astype(BF16), w_ref[d_att:d_att + D_GMLP, :])
         + _dot(ssm_ref[0].astype(BF16), w_ref[d_att + D_GMLP:, :]))
    z = ALPHA * x + (1.0 + gate) * y
    o_ref[0] = _layer_norm(z, g_ref[...], b_ref[...])


def _mixer_outproj(x, mod_l, o_c, o_s, o_w, gates, o_gmlp, o_ssm, w_out, g, b):
    bn, t, d = x.shape
    d_att = ATT_HEADS * HEAD_DIM
    e = np.zeros((LANES, 3 * d_att), np.float32)
    for h in range(ATT_HEADS):
        for br in range(3):
            e[3 * h + br, br * d_att + h * HEAD_DIM:br * d_att + (h + 1) * HEAD_DIM] = 1.0
    row = lambda width: pl.BlockSpec((1, TM, width), lambda bi, i: (bi, i, 0))
    full = lambda shape: pl.BlockSpec(shape, lambda bi, i: (0,) * len(shape))
    return pl.pallas_call(
        _outproj_body,
        grid=(bn, t // TM),
        in_specs=[row(d), pl.BlockSpec((1, N_SUB * 3, d), lambda bi, i: (bi, 0, 0)),
                  row(d_att), row(d_att), row(d_att), row(LANES), full((LANES, 3 * d_att)),
                  row(D_GMLP), row(D_SSM), full((d, d)), full((1, d)), full((1, d))],
        out_specs=row(d),
        out_shape=jax.ShapeDtypeStruct(x.shape, F32),
        compiler_params=_cparams(2, 48),
        name="mixer_outproj",
    )(x, mod_l, o_c, o_s, o_w, gates, jnp.asarray(e), o_gmlp, o_ssm, w_out.astype(BF16), g[None], b[None])


def _nsa(qp, kcv, ksa, vst, kwp, vwt, tables, ovt, cmp_pe, cmp_w1, cmp_b1, cmp_w2):
    nb_c, nb_s, nb_w = tables
    kcp, lhs = _compress(kcv, cmp_pe, cmp_w1, cmp_b1, cmp_w2, ovt)
    o_c, imp_t = _cmp_attention(qp, kcp, lhs, nb_c)
    selneg_t = _select_blocks(imp_t)
    o_s = _sel_attention(qp, selneg_t, ksa, vst, nb_s)
    o_w = _win_attention(qp, kwp, vwt, nb_w)
    return o_c, o_s, o_w


def kernel(x, c, rel_bias, ada_w, ada_b, ln_g, ln_b, ffn_w1, ffn_w3, ffn_w2, w_in, w_out, cmp_pe, cmp_w1, cmp_b1,
           cmp_w2, gmlp_ln_g, gmlp_ln_b, gmlp_ws, gmlp_bs, ssm_conv_w, ssm_conv_b, ssm_dt_bias, ssm_a_log, ssm_d,
           ssm_norm_g):
    t = x.shape[1]
    mod = _adaln_mod(c, ada_w, ada_b)
    tables = _bias_tables(rel_bias)
    ovt = _overlap_matrix(t)
    for l in range(ada_w.shape[0]):
        x = _ffn_sublayer(x, mod[l], ffn_w1[l, 0], ffn_w3[l, 0], ffn_w2[l, 0], ln_g[l, 0], ln_b[l, 0], 0)
        qp, kcv, ksa, vst, kwp, vwt, gates, gm_in, z, xbc, dt = _mixer_inproj(x, mod[l], _relayout_w_in(w_in[l]))
        o_c, o_s, o_w = _nsa(qp, kcv, ksa, vst, kwp, vwt, tables, ovt, cmp_pe[l], cmp_w1[l], cmp_b1[l], cmp_w2[l])
        o_gmlp = _gmlp(gm_in, gmlp_ln_g[l], gmlp_ln_b[l], gmlp_ws[l], gmlp_bs[l])
        o_ssm = _ssd(z, xbc, dt, ssm_conv_w[l], ssm_conv_b[l], ssm_dt_bias[l], ssm_a_log[l], ssm_d[l],
                     ssm_norm_g[l])
        x = _mixer_outproj(x, mod[l], o_c, o_s, o_w, gates, o_gmlp, o_ssm, w_out[l], ln_g[l, 1], ln_b[l, 1])
        x = _ffn_sublayer(x, mod[l], ffn_w1[l, 1], ffn_w3[l, 1], ffn_w2[l, 1], ln_g[l, 2], ln_b[l, 2], 2)
    return x
```

```python
import functools
import math

import numpy as np
import jax
import jax.numpy as jnp
from jax import lax
from jax.experimental import pallas as pl
from jax.experimental.pallas import tpu as pltpu

F32 = jnp.float32
BF16 = jnp.bfloat16
HI = lax.Precision.HIGHEST

D_MODEL = 1024
DEPTH = 2
ATT_HEADS = 8
ATT_KV_GROUPS = 2
ATT_HPG = 4
HEAD_DIM = 64
CMP_LEN = 32
CMP_STRIDE = 16
CMP_HIDDEN = 256
SLC_BLOCK = 64
SLC_TOPN = 16
WINDOW = 512
SEL_FORCE = 1.0e4
REL_BUCKETS = 32
REL_MAX_DIST = 128
GMLP_GROUPS = 4
GMLP_CHUNK = 128
D_GMLP = 256
SSM_HEADS = 4
SSM_HEAD_DIM = 64
D_SSM = 256
SSM_STATE = 128
SSM_CONV = 4
SSM_CHUNK = 256
D_XBC = 768
D_FF = 2816
N_SUB = 3
ALPHA = (2 * DEPTH) ** 0.25
LN_EPS = 1e-5

LANES = 128
LOG2E = math.log2(math.e)
NEG_BIG = -(2.0 ** 100)
KPAD = 512
CMP_KPAD = 16
CMP_CHUNK_LOG2 = 7
CMP_NEAR = 256
CMP_FLAG0 = HEAD_DIM + 1
CMP_SUBTILES = 2
CMP_KEY_SIZES = (512, 768)
TQ_C = 256
TQ = 256
TK = 256
SEL_FAR_TILES = 2
SEL_SUBTILES = 2
SEL_AHEAD = 6
WIN_SUBTILES = 2
GATE_ROWS_PER_GROUP = 3 * ATT_HPG
GATE_ROWS = 16
TM = 512
FF_CHUNK = 256
SEL_CHUNK_BLOCKS = 128

COL_Q = 0
COL_KV = 1024
COL_GATE = 2560
COL_GMLP = 2688
COL_Z = 3200
COL_XBC = 3456
COL_DT = 4224
N_COLS = 4352


def _cparams(n_axes, vmem_mb):
    return pltpu.CompilerParams(dimension_semantics=("arbitrary",) * n_axes,
                                vmem_limit_bytes=vmem_mb * 1024 * 1024)


def _sigmoid(x):
    return 1.0 / (1.0 + jnp.exp(-x))


def _gelu_tanh(x):
    return x * (0.5 * (1.0 + jnp.tanh(math.sqrt(2.0 / math.pi) * (x + 0.044715 * (x * x * x)))))


def _layer_norm(z, g, b):
    mu = jnp.mean(z, axis=-1, keepdims=True)
    d = z - mu
    var = jnp.mean(d * d, axis=-1, keepdims=True)
    return d * lax.rsqrt(var + LN_EPS) * g + b


def _dot(a, b):
    return jnp.dot(a, b, preferred_element_type=F32)


def _dot_hi(a, b):
    return jnp.dot(a, b, preferred_element_type=F32, precision=HI)


def _dot_nt(a, b):
    return lax.dot_general(a, b, (((1,), (1,)), ((), ())), preferred_element_type=F32)


def _mod_body(c_ref, w_ref, b_ref, o_ref):
    c = c_ref[...]
    o_ref[0] = _dot_hi(c * _sigmoid(c), w_ref[0]) + b_ref[0]


def _adaln_mod(c, ada_w, ada_b):
    n_layers, d, n = ada_w.shape
    bn = c.shape[0]
    tn = 1536
    cp = jnp.zeros((8, d), F32).at[:bn].set(c)
    out = pl.pallas_call(
        _mod_body,
        grid=(n_layers, n // tn),
        in_specs=[pl.BlockSpec((8, d), lambda l, j: (0, 0)),
                  pl.BlockSpec((1, d, tn), lambda l, j: (l, 0, j)),
                  pl.BlockSpec((1, 1, tn), lambda l, j: (l, 0, j))],
        out_specs=pl.BlockSpec((1, 8, tn), lambda l, j: (l, 0, j)),
        out_shape=jax.ShapeDtypeStruct((n_layers, 8, n), F32),
        compiler_params=_cparams(2, 32),
        name="adaln_mod",
    )(cp, ada_w, ada_b.reshape(n_layers, 1, n))
    return out[:, :bn].reshape(n_layers, bn, N_SUB * 3, d)


def _ffn_body(x_ref, mod_ref, w1_ref, w3_ref, w2_ref, g_ref, b_ref, o_ref, *, sub):
    x = x_ref[0]
    m = mod_ref[0]
    shift, scale, gate = m[3 * sub:3 * sub + 1], m[3 * sub + 1:3 * sub + 2], m[3 * sub + 2:3 * sub + 3]
    h = (x * (1.0 + scale) + shift).astype(BF16)
    acc = jnp.zeros(x.shape, F32)
    for j in range(D_FF // FF_CHUNK):
        sl = slice(j * FF_CHUNK, (j + 1) * FF_CHUNK)
        a = _dot(h, w1_ref[:, sl])
        b = _dot(h, w3_ref[:, sl])
        acc = acc + _dot((a * _sigmoid(a) * b).astype(BF16), w2_ref[sl, :])
    z = ALPHA * x + (0.5 * (1.0 + gate)) * acc
    o_ref[0] = _layer_norm(z, g_ref[...], b_ref[...])


def _ffn_sublayer(x, mod_l, w1, w3, w2, g, b, sub):
    bn, t, d = x.shape
    full = lambda shape: pl.BlockSpec(shape, lambda bi, i: (0,) * len(shape))
    return pl.pallas_call(
        functools.partial(_ffn_body, sub=sub),
        grid=(bn, t // TM),
        in_specs=[pl.BlockSpec((1, TM, d), lambda bi, i: (bi, i, 0)),
                  pl.BlockSpec((1, N_SUB * 3, d), lambda bi, i: (bi, 0, 0)),
                  full((d, D_FF)), full((d, D_FF)), full((D_FF, d)),
                  full((1, d)), full((1, d))],
        out_specs=pl.BlockSpec((1, TM, d), lambda bi, i: (bi, i, 0)),
        out_shape=jax.ShapeDtypeStruct(x.shape, F32),
        compiler_params=_cparams(2, 56),
        name="ffn",
    )(x, mod_l, w1.astype(BF16), w3.astype(BF16), w2.astype(BF16), g[None], b[None])


def _pad_cols(w, width):
    return jnp.pad(w, ((0, 0), (0, width - w.shape[1])))


def _relayout_w_in(w_in):
    cols = [_pad_cols(w_in[:, HEAD_DIM * h:HEAD_DIM * (h + 1)], LANES) for h in range(ATT_HEADS)]
    for i in range(6):
        for g in range(ATT_KV_GROUPS):
            base = 512 + 128 * i + HEAD_DIM * g
            cols.append(_pad_cols(w_in[:, base:base + HEAD_DIM], LANES))
    cols.append(_pad_cols(w_in[:, 1280:1304], LANES))
    cols.append(w_in[:, 1304:1816])
    cols.append(w_in[:, 1816:2072])
    cols.append(w_in[:, 2072:2840])
    cols.append(_pad_cols(w_in[:, 2840:2844], LANES))
    return jnp.concatenate(cols, axis=1).astype(BF16)


def _inproj_body(x_ref, mod_ref, w_ref, qp_ref, kcv_ref, ksa_ref, vst_ref, kwp_ref, vwt_ref,
                 gt_ref, gm_ref, z_ref, xbc_ref, dt_ref):
    i = pl.program_id(1)
    x = x_ref[0]
    m = mod_ref[0]
    h = (x * (1.0 + m[4:5]) + m[3:4]).astype(BF16)

    def proj(c0, width):
        return _dot(h, w_ref[:, c0:c0 + width])

    lane = lax.broadcasted_iota(jnp.int32, (1, LANES), 1)
    q_flag = jnp.where(lane == HEAD_DIM, NEG_BIG, 0.0)
    for hh in range(ATT_HEADS):
        qp_ref[0, hh] = (proj(COL_Q + LANES * hh, LANES) * (HEAD_DIM ** -0.5 * LOG2E) + q_flag).astype(BF16)
    for k in range(4):
        kcv_ref[0, k] = proj(COL_KV + LANES * k, LANES).astype(BF16)
    gates_t = _sigmoid(proj(COL_GATE, LANES)).T
    for g in range(ATT_KV_GROUPS):
        gt_ref[0, g] = gates_t[GATE_ROWS_PER_GROUP * g:GATE_ROWS_PER_GROUP * g + GATE_ROWS, :]
    gm_ref[0] = proj(COL_GMLP, 512)
    z_ref[0] = proj(COL_Z, D_SSM)
    xbc_ref[0] = proj(COL_XBC, D_XBC)
    dt_ref[0] = proj(COL_DT, LANES)

    @pl.when(i == 0)
    def _():
        dummy = jnp.broadcast_to(jnp.where(lane == HEAD_DIM, 1.0, 0.0), (TM, LANES)).astype(BF16)
        zeros = jnp.zeros((TM, LANES), BF16)
        for g in range(ATT_KV_GROUPS):
            ksa_ref[0, g, :, 0:LANES] = dummy
            ksa_ref[0, g, :, LANES:2 * LANES] = zeros
            kwp_ref[0, g, :, 0:LANES] = dummy
            kwp_ref[0, g, :, LANES:2 * LANES] = zeros
            vst_ref[0, g] = jnp.zeros((LANES, TM), BF16)
            vwt_ref[0, g] = jnp.zeros((LANES, TM), BF16)

    @pl.when(i > 0)
    def _():
        pos = (i - 1) * TM + lax.broadcasted_iota(jnp.int32, (TM, LANES), 0)
        blk_lane = (pos >> 6) & (SEL_CHUNK_BLOCKS - 1)
        onehot = jnp.where(blk_lane == lax.broadcasted_iota(jnp.int32, (TM, LANES), 1), 1.0, 0.0).astype(BF16)
        ones_row = lax.broadcasted_iota(jnp.int32, (LANES, 1), 0) == HEAD_DIM
        for g in range(ATT_KV_GROUPS):
            ksa_ref[0, g, :, 0:LANES] = proj(COL_KV + LANES * (4 + g), LANES).astype(BF16)
            ksa_ref[0, g, :, LANES:2 * LANES] = onehot
            vst_ref[0, g] = jnp.where(ones_row, 1.0, proj(COL_KV + LANES * (6 + g), LANES).T).astype(BF16)
            kwp_ref[0, g, :, 0:LANES] = proj(COL_KV + LANES * (8 + g), LANES).astype(BF16)
            kwp_ref[0, g, :, LANES:2 * LANES] = jnp.zeros((TM, LANES), BF16)
            vwt_ref[0, g] = jnp.where(ones_row, 1.0, proj(COL_KV + LANES * (10 + g), LANES).T).astype(BF16)


def _mixer_inproj(x, mod_l, w_in):
    bn, t, d = x.shape
    tp = t + KPAD
    row = lambda width: pl.BlockSpec((1, TM, width), lambda bi, i: (bi, jnp.maximum(i - 1, 0), 0))
    sds = jax.ShapeDtypeStruct
    return pl.pallas_call(
        _inproj_body,
        grid=(bn, t // TM + 1),
        in_specs=[row(d),
                  pl.BlockSpec((1, N_SUB * 3, d), lambda bi, i: (bi, 0, 0)),
                  pl.BlockSpec((d, N_COLS), lambda bi, i: (0, 0))],
        out_specs=[pl.BlockSpec((1, ATT_HEADS, TM, LANES), lambda bi, i: (bi, 0, jnp.maximum(i - 1, 0), 0)),
                   pl.BlockSpec((1, 4, TM, LANES), lambda bi, i: (bi, 0, jnp.maximum(i - 1, 0), 0)),
                   pl.BlockSpec((1, ATT_KV_GROUPS, TM, 2 * LANES), lambda bi, i: (bi, 0, i, 0)),
                   pl.BlockSpec((1, ATT_KV_GROUPS, LANES, TM), lambda bi, i: (bi, 0, 0, i)),
                   pl.BlockSpec((1, ATT_KV_GROUPS, TM, 2 * LANES), lambda bi, i: (bi, 0, i, 0)),
                   pl.BlockSpec((1, ATT_KV_GROUPS, LANES, TM), lambda bi, i: (bi, 0, 0, i)),
                   pl.BlockSpec((1, ATT_KV_GROUPS, GATE_ROWS, TM), lambda bi, i: (bi, 0, 0, jnp.maximum(i - 1, 0))),
                   row(512), row(D_SSM), row(D_XBC), row(LANES)],
        out_shape=[sds((bn, ATT_HEADS, t, LANES), BF16),
                   sds((bn, 4, t, LANES), BF16),
                   sds((bn, ATT_KV_GROUPS, tp, 2 * LANES), BF16),
                   sds((bn, ATT_KV_GROUPS, LANES, tp), BF16),
                   sds((bn, ATT_KV_GROUPS, tp, 2 * LANES), BF16),
                   sds((bn, ATT_KV_GROUPS, LANES, tp), BF16),
                   sds((bn, ATT_KV_GROUPS, GATE_ROWS, t), F32), sds((bn, t, 512), F32), sds((bn, t, D_SSM), F32),
                   sds((bn, t, D_XBC), F32), sds((bn, t, LANES), F32)],
        compiler_params=_cparams(2, 48),
        name="mixer_inproj",
    )(x, mod_l, w_in)


def _cmp_rows(t):
    return -(-(t // CMP_STRIDE + 120) // LANES) * LANES


def _compress_body(ck_ref, cv_ref, w1_ref, pe_ref, b1_ref, w2_ref, ovt_ref, k_ref, lhs_ref):
    n_rows = ck_ref.shape[2]
    rows = k_ref.shape[2]
    n_cmp = n_rows - CMP_LEN // CMP_STRIDE + 1
    half = CMP_STRIDE * HEAD_DIM
    row = lax.broadcasted_iota(jnp.int32, (rows, LANES), 0)
    lane = lax.broadcasted_iota(jnp.int32, (rows, LANES), 1)
    real = (row >= CMP_KPAD) & (row < CMP_KPAD + n_cmp)

    def mlp(c, j):
        w1 = w1_ref[j]
        h1 = _dot(c, w1[:half])
        h2 = _dot(c, w1[half:])
        cvec = _dot(pe_ref[j], w1)[0:1] + b1_ref[j]
        hid = h1 + pltpu.roll(h2, n_rows - 1, 0) + cvec
        out = _dot(_gelu_tanh(hid).astype(BF16), w2_ref[j])
        tail = rows - CMP_KPAD - n_rows
        full = jnp.concatenate([jnp.zeros((CMP_KPAD, LANES), F32), out, jnp.zeros((tail, LANES), F32)], axis=0)
        return jnp.where(real, full, 0.0)

    flags = jnp.where(lane == HEAD_DIM, jnp.where(real, 0.0, 1.0),
                      jnp.where(lane - CMP_FLAG0 == (row >> CMP_CHUNK_LOG2), 1.0, 0.0))
    k_ref[0, 0] = (mlp(ck_ref[0, 0], 0) + flags).astype(BF16)
    ones_row = lax.broadcasted_iota(jnp.int32, (LANES, 1), 0) == HEAD_DIM
    lhs_ref[0, 0, 0:LANES, :] = jnp.where(ones_row, 1.0, mlp(cv_ref[0, 0], 1).T).astype(BF16)
    lhs_ref[0, 0, LANES:, :] = ovt_ref[...].astype(BF16)


def _compress(kcv, cmp_pe, cmp_w1, cmp_b1, cmp_w2, ovt):
    bn, _, t, _ = kcv.shape
    n_rows = t // CMP_STRIDE
    width = CMP_STRIDE * HEAD_DIM
    c = kcv[..., :HEAD_DIM].reshape(bn, 4, n_rows, width)
    pe = jnp.zeros((2, 8, CMP_LEN * HEAD_DIM), F32).at[:, 0].set(cmp_pe.reshape(2, -1)).astype(BF16)
    w2 = jnp.pad(cmp_w2, ((0, 0), (0, 0), (0, LANES - HEAD_DIM))).astype(BF16)
    n_slc, rows = ovt.shape
    full = lambda shape: pl.BlockSpec(shape, lambda bi, g: (0,) * len(shape))
    return pl.pallas_call(
        _compress_body,
        grid=(bn, ATT_KV_GROUPS),
        in_specs=[pl.BlockSpec((1, 1, n_rows, width), lambda bi, g: (bi, g, 0, 0)),
                  pl.BlockSpec((1, 1, n_rows, width), lambda bi, g: (bi, ATT_KV_GROUPS + g, 0, 0)),
                  full((2, CMP_LEN * HEAD_DIM, CMP_HIDDEN)), full((2, 8, CMP_LEN * HEAD_DIM)),
                  full((2, 1, CMP_HIDDEN)), full((2, CMP_HIDDEN, LANES)), full((n_slc, rows))],
        out_specs=[pl.BlockSpec((1, 1, rows, LANES), lambda bi, g: (bi, g, 0, 0)),
                   pl.BlockSpec((1, 1, LANES + n_slc, rows), lambda bi, g: (bi, g, 0, 0))],
        out_shape=[jax.ShapeDtypeStruct((bn, ATT_KV_GROUPS, rows, LANES), BF16),
                   jax.ShapeDtypeStruct((bn, ATT_KV_GROUPS, LANES + n_slc, rows), BF16)],
        compiler_params=_cparams(2, 40),
        name="nsa_compress",
    )(c, c, cmp_w1.astype(BF16), pe, cmp_b1[:, None, :], w2, ovt)


def _rel_bucket(dist):
    n = jnp.maximum(dist, 0)
    max_exact = REL_BUCKETS // 2
    nf = jnp.maximum(n, 1).astype(F32)
    large = max_exact + (jnp.log(nf / max_exact) / math.log(REL_MAX_DIST / max_exact)
                         * (REL_BUCKETS - max_exact)).astype(jnp.int32)
    large = jnp.minimum(large, REL_BUCKETS - 1)
    return jnp.where(n < max_exact, n, large)


def _bias_tables(rel_bias):
    far = rel_bias[REL_BUCKETS - 1]

    def table(dist, valid, shift):
        onehot = (_rel_bucket(dist)[..., None] == jnp.arange(REL_BUCKETS)).astype(F32)
        b = jnp.einsum("qkb,bh->hqk", onehot, rel_bias, precision=HI)
        if shift:
            b = b - far[:, None, None]
        return jnp.where(valid[None], b * LOG2E, -jnp.inf)

    n_var = (1 << CMP_CHUNK_LOG2) * CMP_STRIDE // TQ_C
    d_c = jnp.asarray(TQ_C * np.arange(n_var)[:, None, None] + np.arange(TQ_C)[None, None, :]
                      - CMP_STRIDE * np.arange(CMP_NEAR)[None, :, None]
                      + (CMP_STRIDE * CMP_KPAD - CMP_LEN + 1))
    nb_c = jnp.stack([table(d_c[v], d_c[v] >= 0, True) for v in range(n_var)])
    nb_c = jnp.maximum(nb_c, NEG_BIG)
    ki = np.arange(2 * TK)[:, None]
    d_s = jnp.asarray(np.arange(TQ)[None, :] - ki + TK)
    nb_s = table(d_s, d_s >= 0, True)
    d_w = jnp.asarray(np.arange(TQ)[None, :] - np.arange(TQ + WINDOW)[:, None] + WINDOW)
    nb_w = table(d_w, (d_w >= 0) & (d_w < WINDOW), False)
    return nb_c, nb_s, nb_w


def _overlap_matrix(t):
    n_cmp = (t - CMP_LEN) // CMP_STRIDE + 1
    n_slc = t // SLC_BLOCK
    cs = np.arange(n_cmp) * CMP_STRIDE
    ss = np.arange(n_slc) * SLC_BLOCK
    ov = np.clip(np.minimum(cs[:, None] + CMP_LEN, ss[None, :] + SLC_BLOCK)
                 - np.maximum(cs[:, None], ss[None, :]), 0, None) / CMP_LEN
    full = np.zeros((n_slc, _cmp_rows(t)), np.float32)
    full[:, CMP_KPAD:CMP_KPAD + n_cmp] = ov.T
    return jnp.asarray(full)


def _gate_row(gt_ref, r, branch, qs):
    row = 3 * r + branch
    return gt_ref[0, 0, row:row + 1, qs]


def _cmp_attn_body(q_ref, k_ref, lhs_ref, nba_ref, nbb_ref, gt_ref, o_ref, imp_ref, s_ref):
    step = pl.program_id(2)
    rows = k_ref.shape[2]
    chunk = 1 << CMP_CHUNK_LOG2
    nb_refs = (nba_ref, nbb_ref)
    rowi = lax.broadcasted_iota(jnp.int32, (LANES, TQ_C), 0)

    def near_chunk_of(sub):
        return ((CMP_SUBTILES * step + sub) * (TQ_C // CMP_STRIDE)) >> CMP_CHUNK_LOG2

    def body(n_keys):
        kf = k_ref[0, 0, 0:n_keys, :]
        lhs = lhs_ref[0, 0, :, 0:n_keys]
        units = []
        for sub in range(CMP_SUBTILES):
            near_chunk = near_chunk_of(sub)
            chunk_flag = jnp.where((rowi >= CMP_FLAG0) & (rowi - CMP_FLAG0 > near_chunk + 1), NEG_BIG, 0.0)
            a0 = pl.multiple_of(near_chunk << CMP_CHUNK_LOG2, chunk)
            for r in range(ATT_HPG):
                q_t = (q_ref[0, r, sub * TQ_C:(sub + 1) * TQ_C, :].astype(F32).T + chunk_flag).astype(BF16)
                units.append((sub, r, q_t, a0))
        for u, (_, _, q_t, _) in enumerate(units[:SEL_AHEAD]):
            s_ref[u, 0:n_keys, :] = _dot(kf, q_t)
        imps = [jnp.zeros((imp_ref.shape[2], TQ_C), F32) for _ in range(CMP_SUBTILES)]
        outs = []
        for u, (sub, r, _, a0) in enumerate(units):
            if u + SEL_AHEAD < len(units):
                s_ref[u + SEL_AHEAD, 0:n_keys, :] = _dot(kf, units[u + SEL_AHEAD][2])
            s_ref[u, pl.ds(a0, CMP_NEAR), :] += nb_refs[sub][0, r]
            s = s_ref[u, 0:n_keys, :]
            mx = jnp.max(s, axis=0, keepdims=True)
            mx = jnp.where(mx < 0.5 * NEG_BIG, 0.0, mx)
            p = jnp.exp2(s - mx).astype(BF16)
            res = _dot(lhs, p)
            inv = 1.0 / jnp.maximum(res[HEAD_DIM:HEAD_DIM + 1, :], 1e-30)
            gate = _gate_row(gt_ref, r, 0, slice(sub * TQ_C, (sub + 1) * TQ_C))
            outs.append((res[0:LANES, :] * inv * gate).T[:, :HEAD_DIM])
            imps[sub] = imps[sub] + res[LANES:, :] * inv
        for sub in range(CMP_SUBTILES):
            qs = slice(sub * TQ_C, (sub + 1) * TQ_C)
            o_ref[0, qs, :] = jnp.concatenate(outs[ATT_HPG * sub:ATT_HPG * (sub + 1)], axis=1)
            imp_ref[0, 0, :, qs] = imps[sub]

    need = (near_chunk_of(CMP_SUBTILES - 1) + 2) * chunk
    sizes = sorted({min(rows, s) for s in CMP_KEY_SIZES} | {rows})
    lo = 0
    for n_keys in sizes:
        @pl.when((need > lo) & (need <= n_keys))
        def _(n_keys=n_keys):
            body(n_keys)
        lo = n_keys


def _gate_spec(tq):
    return pl.BlockSpec((1, 1, GATE_ROWS, tq), lambda bi, g, m: (bi, g, 0, m))


def _cmp_attention(qp, kcp, lhs, nb_c, gates_t):
    bn, _, t, _ = qp.shape
    rows = kcp.shape[2]
    n_slc = lhs.shape[2] - LANES
    n_var = nb_c.shape[0]
    tq = CMP_SUBTILES * TQ_C
    table = lambda sub: pl.BlockSpec((1, ATT_HPG, CMP_NEAR, TQ_C),
                                     lambda bi, g, m: ((CMP_SUBTILES * m + sub) % n_var, g, 0, 0))
    return pl.pallas_call(
        _cmp_attn_body,
        grid=(bn, ATT_KV_GROUPS, t // tq),
        in_specs=[pl.BlockSpec((1, ATT_HPG, tq, LANES), lambda bi, g, m: (bi, g, m, 0)),
                  pl.BlockSpec((1, 1, rows, LANES), lambda bi, g, m: (bi, g, 0, 0)),
                  pl.BlockSpec((1, 1, LANES + n_slc, rows), lambda bi, g, m: (bi, g, 0, 0)),
                  table(0), table(1), _gate_spec(tq)],
        out_specs=[pl.BlockSpec((1, tq, ATT_HPG * HEAD_DIM), lambda bi, g, m: (bi, m, g)),
                   pl.BlockSpec((1, 1, n_slc, tq), lambda bi, g, m: (bi, g, 0, m))],
        out_shape=[jax.ShapeDtypeStruct((bn, t, ATT_HEADS * HEAD_DIM), F32),
                   jax.ShapeDtypeStruct((bn, ATT_KV_GROUPS, n_slc, t), F32)],
        scratch_shapes=[pltpu.VMEM((CMP_SUBTILES * ATT_HPG, rows, TQ_C), F32)],
        compiler_params=_cparams(3, 48),
        name="nsa_cmp_attn",
    )(qp, kcp, lhs, nb_c, nb_c, gates_t)


TOPK_COLS = 1024


def _topk_body(imp_ref, o_ref):
    i = pl.program_id(2)
    imp = imp_ref[0, 0]
    n_slc, cols = imp.shape
    t = i * cols + lax.broadcasted_iota(jnp.int32, (1, cols), 1)
    cur = t >> 6
    blk = lax.broadcasted_iota(jnp.int32, (n_slc, cols), 0)
    forced = (blk == 0) | (blk == cur) | (blk == cur - 1)
    vals = jnp.where(forced, SEL_FORCE, jnp.where(blk <= cur, imp, -SEL_FORCE))
    blkf = blk.astype(F32)
    for _ in range(min(SLC_TOPN, n_slc)):
        mx = jnp.max(vals, axis=0, keepdims=True)
        first = jnp.min(jnp.where(vals == mx, blkf, float(n_slc)), axis=0, keepdims=True)
        vals = jnp.where(blkf == first, -jnp.inf, vals)
    o_ref[0, 0] = jnp.where(vals == -jnp.inf, 0.0, NEG_BIG).astype(BF16)


def _select_blocks(imp_t):
    bn, g, n_slc, t = imp_t.shape
    cols = min(TOPK_COLS, t)
    spec = pl.BlockSpec((1, 1, n_slc, cols), lambda bi, gi, i: (bi, gi, 0, i))
    return pl.pallas_call(
        _topk_body,
        grid=(bn, g, t // cols),
        in_specs=[spec],
        out_specs=spec,
        out_shape=jax.ShapeDtypeStruct(imp_t.shape, BF16),
        compiler_params=_cparams(3, 40),
        name="nsa_topk",
    )(imp_t)


def _sel_attn_body(q_ref, sel_ref, k_ref, vt_ref, nb_ref, gt_ref, o_ref, qat_ref, m_ref, acc_ref, sc0_ref,
                   sc1_ref):
    sc_refs = (sc0_ref, sc1_ref)
    pair = pl.program_id(2)
    n_chunks = qat_ref.shape[0]
    m_a = SEL_SUBTILES * pair
    for sub in range(SEL_SUBTILES):
        qs = slice(sub * TQ, (sub + 1) * TQ)
        for r in range(ATT_HPG):
            q_t = q_ref[0, r, qs, :].astype(F32).T.astype(BF16)
            for c in range(n_chunks):
                qat_ref[c, ATT_HPG * sub + r, 0:LANES, :] = q_t
        for c in range(n_chunks):
            sel_t = sel_ref[0, 0, c * LANES:(c + 1) * LANES, qs]
            for r in range(ATT_HPG):
                qat_ref[c, ATT_HPG * sub + r, LANES:2 * LANES, :] = sel_t
    m_ref[...] = jnp.full(m_ref.shape, -jnp.inf, F32)
    acc_ref[...] = jnp.zeros(acc_ref.shape, F32)

    tiles_per_chunk = SEL_CHUNK_BLOCKS * SLC_BLOCK // TK
    units_a = tuple(range(ATT_HPG))
    units_b = tuple(range(ATT_HPG, 2 * ATT_HPG))
    bias_hi = lambda r: nb_ref[r, TK:2 * TK, :]
    bias_lo = lambda r: nb_ref[r, 0:TK, :]

    def load(j, n_tiles):
        row0 = pl.multiple_of(KPAD + TK * j, TK)
        return (k_ref[0, 0, pl.ds(row0, n_tiles * TK), :], vt_ref[0, 0, :, pl.ds(row0, n_tiles * TK)],
                jnp.maximum(j, 0) // tiles_per_chunk)

    def qk(tile, u):
        kt, _, c = tile
        return _dot(kt, qat_ref[c, u])

    def update(tile, u, s, bias):
        if bias is not None:
            s = s + bias(u % ATT_HPG)
        m_prev = m_ref[u]
        m_new = jnp.maximum(m_prev, jnp.max(s, axis=0, keepdims=True))
        alpha = jnp.exp2(m_prev - m_new)
        p = jnp.exp2(s - m_new).astype(BF16)
        acc_ref[u] = alpha * acc_ref[u] + _dot(tile[1], p)
        m_ref[u] = m_new

    def step(work):
        todo = [(tile, u, bias) for tile, units, bias in work for u in units]
        scores = [qk(tile, u) for tile, u, _ in todo[:SEL_AHEAD]]
        for i, (tile, u, bias) in enumerate(todo):
            if i + SEL_AHEAD < len(todo):
                scores.append(qk(todo[i + SEL_AHEAD][0], todo[i + SEL_AHEAD][1]))
            s = scores[i]
            scores[i] = None
            update(tile, u, s, bias)

    def pipelined_step(tile, slot, next_tile):
        for i, u in enumerate(all_units):
            if next_tile is not None:
                for k in range(2 * i, min(2 * i + 2, len(all_units))):
                    sc_refs[1 - slot][k] = qk(next_tile, all_units[k])
            update(tile, u, sc_refs[slot][i], None)

    all_units = units_a + units_b
    step([(load(m_a, 1), units_a, bias_hi), (load(m_a + 1, 1), units_b, bias_hi)])
    step([(load(m_a - 1, 1), units_a, bias_lo), (load(m_a, 1), units_b, bias_lo)])
    step([(load(m_a - 1, 1), units_b, None)])

    n_common = jnp.maximum(m_a - 1, 0)
    n_big = n_common // SEL_FAR_TILES
    far_tile = lambda i: load(SEL_FAR_TILES * i, SEL_FAR_TILES)

    @pl.when(n_big > 0)
    def _():
        first = far_tile(0)
        for k, u in enumerate(all_units):
            sc_refs[0][k] = qk(first, u)
        n_pairs = (n_big - 1) // 2

        def far_pair(i, carry):
            pipelined_step(far_tile(2 * i), 0, far_tile(2 * i + 1))
            pipelined_step(far_tile(2 * i + 1), 1, far_tile(2 * i + 2))
            return carry

        lax.fori_loop(0, n_pairs, far_pair, 0)

        @pl.when(n_big - 2 * n_pairs == 1)
        def _():
            pipelined_step(far_tile(n_big - 1), 0, None)

        @pl.when(n_big - 2 * n_pairs == 2)
        def _():
            pipelined_step(far_tile(n_big - 2), 0, far_tile(n_big - 1))
            pipelined_step(far_tile(n_big - 1), 1, None)

    def far_one(j, carry):
        step([(load(j, 1), all_units, None)])
        return carry

    lax.fori_loop(SEL_FAR_TILES * n_big, n_common, far_one, 0)
    for sub in range(SEL_SUBTILES):
        outs = []
        for r in range(ATT_HPG):
            acc = acc_ref[ATT_HPG * sub + r]
            gate = _gate_row(gt_ref, r, 1, slice(sub * TQ, (sub + 1) * TQ))
            outs.append((acc / acc[HEAD_DIM:HEAD_DIM + 1, :] * gate).T[:, :HEAD_DIM])
        o_ref[0, sub * TQ:(sub + 1) * TQ, :] = jnp.concatenate(outs, axis=1)


def _sel_attention(qp, selneg, ksa, vst, nb_s, gates_t):
    bn, _, t, _ = qp.shape
    tp = ksa.shape[2]
    n_slc = selneg.shape[2]
    n_chunks = max(n_slc // SEL_CHUNK_BLOCKS, 1)
    tq = SEL_SUBTILES * TQ
    n_units = SEL_SUBTILES * ATT_HPG
    return pl.pallas_call(
        _sel_attn_body,
        grid=(bn, ATT_KV_GROUPS, t // tq),
        in_specs=[pl.BlockSpec((1, ATT_HPG, tq, LANES), lambda bi, g, m: (bi, g, m, 0)),
                  pl.BlockSpec((1, 1, n_slc, tq), lambda bi, g, m: (bi, g, 0, m)),
                  pl.BlockSpec((1, 1, tp, 2 * LANES), lambda bi, g, m: (bi, g, 0, 0)),
                  pl.BlockSpec((1, 1, LANES, tp), lambda bi, g, m: (bi, g, 0, 0)),
                  pl.BlockSpec((ATT_HPG, 2 * TK, TQ), lambda bi, g, m: (g, 0, 0)),
                  _gate_spec(tq)],
        out_specs=pl.BlockSpec((1, tq, ATT_HPG * HEAD_DIM), lambda bi, g, m: (bi, m, g)),
        out_shape=jax.ShapeDtypeStruct((bn, t, ATT_HEADS * HEAD_DIM), F32),
        scratch_shapes=[pltpu.VMEM((n_chunks, n_units, 2 * LANES, TQ), BF16),
                        pltpu.VMEM((n_units, 1, TQ), F32),
                        pltpu.VMEM((n_units, LANES, TQ), F32),
                        pltpu.VMEM((n_units, SEL_FAR_TILES * TK, TQ), F32),
                        pltpu.VMEM((n_units, SEL_FAR_TILES * TK, TQ), F32)],
        compiler_params=_cparams(3, 56),
        name="nsa_sel_attn",
    )(qp, selneg, ksa, vst, nb_s, gates_t)


def _win_attn_body(q_ref, k_ref, vt_ref, nb_ref, gt_ref, o_ref):
    step = pl.program_id(2)
    keys = TQ + WINDOW
    units = []
    for sub in range(WIN_SUBTILES):
        row0 = pl.multiple_of((WIN_SUBTILES * step + sub) * TQ, TQ)
        kt = k_ref[0, 0, pl.ds(row0, keys), :]
        vt = vt_ref[0, 0, :, pl.ds(row0, keys)]
        for r in range(ATT_HPG):
            q_t = q_ref[0, r, sub * TQ:(sub + 1) * TQ, :].astype(F32).T.astype(BF16)
            q_t = jnp.concatenate([q_t, jnp.zeros((LANES, TQ), BF16)], axis=0)
            units.append((kt, vt, q_t, r))
    scores = [_dot(kt, q_t) for kt, _, q_t, _ in units[:SEL_AHEAD]]
    outs = []
    for i, (_, vt, _, r) in enumerate(units):
        if i + SEL_AHEAD < len(units):
            scores.append(_dot(units[i + SEL_AHEAD][0], units[i + SEL_AHEAD][2]))
        s = scores[i] + nb_ref[r]
        scores[i] = None
        p = jnp.exp2(s - jnp.max(s, axis=0, keepdims=True)).astype(BF16)
        res = _dot(vt, p)
        sub = i // ATT_HPG
        gate = _gate_row(gt_ref, r, 2, slice(sub * TQ, (sub + 1) * TQ))
        outs.append((res / res[HEAD_DIM:HEAD_DIM + 1, :] * gate).T[:, :HEAD_DIM])
    for sub in range(WIN_SUBTILES):
        o_ref[0, sub * TQ:(sub + 1) * TQ, :] = jnp.concatenate(outs[ATT_HPG * sub:ATT_HPG * (sub + 1)], axis=1)


def _win_attention(qp, kwp, vwt, nb_w, gates_t):
    bn, _, t, _ = qp.shape
    tp = kwp.shape[2]
    tq = WIN_SUBTILES * TQ
    return pl.pallas_call(
        _win_attn_body,
        grid=(bn, ATT_KV_GROUPS, t // tq),
        in_specs=[pl.BlockSpec((1, ATT_HPG, tq, LANES), lambda bi, g, m: (bi, g, m, 0)),
                  pl.BlockSpec((1, 1, tp, 2 * LANES), lambda bi, g, m: (bi, g, 0, 0)),
                  pl.BlockSpec((1, 1, LANES, tp), lambda bi, g, m: (bi, g, 0, 0)),
                  pl.BlockSpec((ATT_HPG, TQ + WINDOW, TQ), lambda bi, g, m: (g, 0, 0)),
                  _gate_spec(tq)],
        out_specs=pl.BlockSpec((1, tq, ATT_HPG * HEAD_DIM), lambda bi, g, m: (bi, m, g)),
        out_shape=jax.ShapeDtypeStruct((bn, t, ATT_HEADS * HEAD_DIM), F32),
        compiler_params=_cparams(3, 48),
        name="nsa_win_attn",
    )(qp, kwp, vwt, nb_w, gates_t)


GMLP_ROWS = 1024


def _gmlp_body(uv_ref, g_ref, b_ref, ws_ref, bs_ref, o_ref):
    act = _gelu_tanh(uv_ref[0])
    u = act[:, :D_GMLP]
    v = _layer_norm(act[:, D_GMLP:], g_ref[...], b_ref[...])
    ti = lax.broadcasted_iota(jnp.int32, (GMLP_CHUNK, GMLP_CHUNK), 0)
    si = lax.broadcasted_iota(jnp.int32, (GMLP_CHUNK, GMLP_CHUNK), 1)
    w = [jnp.where(ti >= si, ws_ref[g], 0.0).astype(BF16) for g in range(GMLP_GROUPS)]
    lane = lax.broadcasted_iota(jnp.int32, (1, LANES), 1)
    lo = lane < HEAD_DIM
    for c in range(uv_ref.shape[1] // GMLP_CHUNK):
        rs = slice(c * GMLP_CHUNK, (c + 1) * GMLP_CHUNK)
        pairs = []
        for pr in range(GMLP_GROUPS // 2):
            vp = v[rs, pr * LANES:(pr + 1) * LANES]
            pairs.append(_dot(w[2 * pr], jnp.where(lo, vp, 0.0).astype(BF16))
                         + _dot(w[2 * pr + 1], jnp.where(lo, 0.0, vp).astype(BF16)))
        o_ref[0, rs, :] = u[rs] * (jnp.concatenate(pairs, axis=1) + bs_ref[...])


def _gmlp(uv, ln_g, ln_b, ws, bs):
    bn, t, _ = uv.shape
    rows = min(GMLP_ROWS, t)
    bs_lanes = jnp.repeat(bs.T, D_GMLP // GMLP_GROUPS, axis=1)
    full = lambda shape: pl.BlockSpec(shape, lambda bi, i: (0,) * len(shape))
    return pl.pallas_call(
        _gmlp_body,
        grid=(bn, t // rows),
        in_specs=[pl.BlockSpec((1, rows, 2 * D_GMLP), lambda bi, i: (bi, i, 0)),
                  full((1, D_GMLP)), full((1, D_GMLP)),
                  full((GMLP_GROUPS, GMLP_CHUNK, GMLP_CHUNK)), full((GMLP_CHUNK, D_GMLP))],
        out_specs=pl.BlockSpec((1, rows, D_GMLP), lambda bi, i: (bi, i, 0)),
        out_shape=jax.ShapeDtypeStruct((bn, t, D_GMLP), F32),
        compiler_params=_cparams(2, 40),
        name="gmlp",
    )(uv, ln_g[None], ln_b[None], ws, bs_lanes)


def _ssd_body(z_ref, xbc_ref, prev_ref, dt_ref, cw_ref, cb_ref, dtb_ref, alog_ref, dsk_ref, ng_ref, e4_ref,
              o_ref, xe_ref, state_ref):
    c = pl.program_id(1)
    L = SSM_CHUNK

    @pl.when(c == 0)
    def _():
        state_ref[...] = jnp.zeros(state_ref.shape, F32)

    xe_ref[0:8] = jnp.where(c > 0, prev_ref[0], 0.0)
    xe_ref[8:8 + L] = xbc_ref[0]
    conv = cb_ref[...]
    for k in range(SSM_CONV):
        conv = conv + cw_ref[k:k + 1, :] * xe_ref[pl.ds(8 - (SSM_CONV - 1) + k, L), :]
    xc = conv * _sigmoid(conv)
    xs, bm, cm = xc[:, :D_SSM], xc[:, D_SSM:D_SSM + 2 * SSM_STATE], xc[:, D_SSM + 2 * SSM_STATE:]

    dtr = dt_ref[0] + dtb_ref[...]
    dt = jnp.maximum(dtr, 0.0) + jnp.log1p(jnp.exp(-jnp.abs(dtr)))
    a = -jnp.exp(alog_ref[...]) * dt
    ti = lax.broadcasted_iota(jnp.int32, (L, L), 0)
    si = lax.broadcasted_iota(jnp.int32, (L, L), 1)
    causal = ti >= si
    cs = _dot_hi(jnp.where(causal, 1.0, 0.0), a)
    cs_t = cs.T
    e4 = e4_ref[...]
    dt_e = _dot_hi(dt, e4)
    cs_e = _dot_hi(cs, e4)
    last_e = cs_e[L - 1:L, :]
    x_dt = xs * dt_e
    x_dec = (x_dt * jnp.exp(last_e - cs_e)).astype(BF16)
    x_dt = x_dt.astype(BF16)
    grow = jnp.exp(cs_e)
    lane = lax.broadcasted_iota(jnp.int32, (1, LANES), 1)
    lo = lane < SSM_HEAD_DIM
    zero = jnp.zeros((), BF16)

    ys = []
    for g in range(2):
        gs = slice(g * LANES, (g + 1) * LANES)
        cg = cm[:, gs].astype(BF16)
        bg = bm[:, gs]
        scores = _dot_nt(cg, bg.astype(BF16))
        xp = x_dt[:, gs]
        y = None
        for hh in range(2):
            h = 2 * g + hh
            seg = jnp.broadcast_to(cs[:, h:h + 1], (L, L)) - jnp.broadcast_to(cs_t[h:h + 1, :], (L, L))
            w = (scores * jnp.where(causal, jnp.exp(seg), 0.0)).astype(BF16)
            xh = jnp.where(lo, xp, zero) if hh == 0 else jnp.where(lo, zero, xp)
            yh = _dot(w, xh)
            y = yh if y is None else y + yh
        state = state_ref[g]
        y = y + _dot(cg, state.astype(BF16)) * grow[:, gs]
        state_ref[g] = jnp.exp(last_e[:, gs]) * state + _dot(bg.T.astype(BF16), x_dec[:, gs])
        ys.append(y)
    y = jnp.concatenate(ys, axis=1) + xs * dsk_ref[...]
    zz = z_ref[0]
    gg = y * (zz * _sigmoid(zz))
    outs = []
    for g in range(2):
        gp = gg[:, g * LANES:(g + 1) * LANES]
        outs.append(gp * lax.rsqrt(jnp.mean(gp * gp, axis=-1, keepdims=True) + LN_EPS))
    o_ref[0] = jnp.concatenate(outs, axis=1) * ng_ref[...]


def _head_lanes(v):
    return jnp.repeat(v, SSM_HEAD_DIM)[None]


def _ssd(z, xbc, dt, conv_w, conv_b, dt_bias, a_log, d_skip, norm_g):
    bn, t, _ = z.shape
    L = SSM_CHUNK
    first_lanes = lambda v: jnp.zeros((1, LANES), F32).at[0, :SSM_HEADS].set(v)
    e4 = np.zeros((LANES, D_SSM), np.float32)
    for h in range(SSM_HEADS):
        e4[h, h * SSM_HEAD_DIM:(h + 1) * SSM_HEAD_DIM] = 1.0
    full = lambda shape: pl.BlockSpec(shape, lambda bi, c: (0,) * len(shape))
    return pl.pallas_call(
        _ssd_body,
        grid=(bn, t // L),
        in_specs=[pl.BlockSpec((1, L, D_SSM), lambda bi, c: (bi, c, 0)),
                  pl.BlockSpec((1, L, D_XBC), lambda bi, c: (bi, c, 0)),
                  pl.BlockSpec((1, 8, D_XBC), lambda bi, c: (bi, jnp.maximum(c * (L // 8) - 1, 0), 0)),
                  pl.BlockSpec((1, L, LANES), lambda bi, c: (bi, c, 0)),
                  full((SSM_CONV, D_XBC)), full((1, D_XBC)), full((1, LANES)), full((1, LANES)),
                  full((1, D_SSM)), full((1, D_SSM)), full((LANES, D_SSM))],
        out_specs=pl.BlockSpec((1, L, D_SSM), lambda bi, c: (bi, c, 0)),
        out_shape=jax.ShapeDtypeStruct((bn, t, D_SSM), F32),
        scratch_shapes=[pltpu.VMEM((L + 8, D_XBC), F32),
                        pltpu.VMEM((2, SSM_STATE, LANES), F32)],
        compiler_params=_cparams(2, 40),
        name="ssd",
    )(z, xbc, xbc, dt, conv_w, conv_b[None], first_lanes(dt_bias), first_lanes(a_log),
      _head_lanes(d_skip), norm_g[None], jnp.asarray(e4))


def _outproj_body(x_ref, mod_ref, oc_ref, os_ref, ow_ref, gm_ref, ssm_ref, w_ref, g_ref, b_ref, o_ref):
    x = x_ref[0]
    m = mod_ref[0]
    gate = m[5:6]
    d_att = ATT_HEADS * HEAD_DIM
    o_att = oc_ref[0] + os_ref[0] + ow_ref[0]
    y = (_dot(o_att.astype(BF16), w_ref[0:d_att, :])
         + _dot(gm_ref[0].astype(BF16), w_ref[d_att:d_att + D_GMLP, :])
         + _dot(ssm_ref[0].astype(BF16), w_ref[d_att + D_GMLP:, :]))
    z = ALPHA * x + (1.0 + gate) * y
    o_ref[0] = _layer_norm(z, g_ref[...], b_ref[...])


def _mixer_outproj(x, mod_l, o_c, o_s, o_w, o_gmlp, o_ssm, w_out, g, b):
    bn, t, d = x.shape
    d_att = ATT_HEADS * HEAD_DIM
    row = lambda width: pl.BlockSpec((1, TM, width), lambda bi, i: (bi, i, 0))
    full = lambda shape: pl.BlockSpec(shape, lambda bi, i: (0,) * len(shape))
    return pl.pallas_call(
        _outproj_body,
        grid=(bn, t // TM),
        in_specs=[row(d), pl.BlockSpec((1, N_SUB * 3, d), lambda bi, i: (bi, 0, 0)),
                  row(d_att), row(d_att), row(d_att),
                  row(D_GMLP), row(D_SSM), full((d, d)), full((1, d)), full((1, d))],
        out_specs=row(d),
        out_shape=jax.ShapeDtypeStruct(x.shape, F32),
        compiler_params=_cparams(2, 48),
        name="mixer_outproj",
    )(x, mod_l, o_c, o_s, o_w, o_gmlp, o_ssm, w_out.astype(BF16), g[None], b[None])


def _nsa(qp, kcv, ksa, vst, kwp, vwt, gates_t, tables, ovt, cmp_pe, cmp_w1, cmp_b1, cmp_w2):
    nb_c, nb_s, nb_w = tables
    kcp, lhs = _compress(kcv, cmp_pe, cmp_w1, cmp_b1, cmp_w2, ovt)
    o_c, imp_t = _cmp_attention(qp, kcp, lhs, nb_c, gates_t)
    selneg_t = _select_blocks(imp_t)
    o_s = _sel_attention(qp, selneg_t, ksa, vst, nb_s, gates_t)
    o_w = _win_attention(qp, kwp, vwt, nb_w, gates_t)
    return o_c, o_s, o_w


def kernel(x, c, rel_bias, ada_w, ada_b, ln_g, ln_b, ffn_w1, ffn_w3, ffn_w2, w_in, w_out, cmp_pe, cmp_w1, cmp_b1,
           cmp_w2, gmlp_ln_g, gmlp_ln_b, gmlp_ws, gmlp_bs, ssm_conv_w, ssm_conv_b, ssm_dt_bias, ssm_a_log, ssm_d,
           ssm_norm_g):
    t = x.shape[1]
    mod = _adaln_mod(c, ada_w, ada_b)
    tables = _bias_tables(rel_bias)
    ovt = _overlap_matrix(t)
    for l in range(ada_w.shape[0]):
        x = _ffn_sublayer(x, mod[l], ffn_w1[l, 0], ffn_w3[l, 0], ffn_w2[l, 0], ln_g[l, 0], ln_b[l, 0], 0)
        qp, kcv, ksa, vst, kwp, vwt, gates_t, gm_in, z, xbc, dt = _mixer_inproj(x, mod[l], _relayout_w_in(w_in[l]))
        o_c, o_s, o_w = _nsa(qp, kcv, ksa, vst, kwp, vwt, gates_t, tables, ovt, cmp_pe[l], cmp_w1[l], cmp_b1[l],
                             cmp_w2[l])
        o_gmlp = _gmlp(gm_in, gmlp_ln_g[l], gmlp_ln_b[l], gmlp_ws[l], gmlp_bs[l])
        o_ssm = _ssd(z, xbc, dt, ssm_conv_w[l], ssm_conv_b[l], ssm_dt_bias[l], ssm_a_log[l], ssm_d[l],
                     ssm_norm_g[l])
        x = _mixer_outproj(x, mod[l], o_c, o_s, o_w, o_gmlp, o_ssm, w_out[l], ln_g[l, 1], ln_b[l, 1])
        x = _ffn_sublayer(x, mod[l], ffn_w1[l, 1], ffn_w3[l, 1], ffn_w2[l, 1], ln_g[l, 2], ln_b[l, 2], 2)
    return x
```

```python
import functools
import math

import numpy as np
import jax
import jax.numpy as jnp
from jax import lax
from jax.experimental import pallas as pl
from jax.experimental.pallas import tpu as pltpu

F32 = jnp.float32
BF16 = jnp.bfloat16
HI = lax.Precision.HIGHEST

D_MODEL = 1024
DEPTH = 2
ATT_HEADS = 8
ATT_KV_GROUPS = 2
ATT_HPG = 4
HEAD_DIM = 64
CMP_LEN = 32
CMP_STRIDE = 16
CMP_HIDDEN = 256
SLC_BLOCK = 64
SLC_TOPN = 16
WINDOW = 512
SEL_FORCE = 1.0e4
REL_BUCKETS = 32
REL_MAX_DIST = 128
GMLP_GROUPS = 4
GMLP_CHUNK = 128
D_GMLP = 256
SSM_HEADS = 4
SSM_HEAD_DIM = 64
D_SSM = 256
SSM_STATE = 128
SSM_CONV = 4
SSM_CHUNK = 256
D_XBC = 768
D_FF = 2816
N_SUB = 3
ALPHA = (2 * DEPTH) ** 0.25
LN_EPS = 1e-5

LANES = 128
LOG2E = math.log2(math.e)
NEG_BIG = -(2.0 ** 100)
KPAD = 512
CMP_KPAD = 16
CMP_CHUNK_LOG2 = 7
CMP_NEAR = 256
CMP_FLAG0 = HEAD_DIM + 1
CMP_SUBTILES = 2
CMP_KEY_SIZES = (512, 768)
TQ_C = 256
TQ = 256
TK = 256
SEL_FAR_TILES = 2
SEL_SUBTILES = 2
SEL_AHEAD = 6
WIN_SUBTILES = 2
GATE_ROWS_PER_GROUP = 3 * ATT_HPG
GATE_ROWS = 16
TM = 512
FF_CHUNK = 256
SEL_CHUNK_BLOCKS = 128

COL_Q = 0
COL_KV = 1024
COL_GATE = 2560
COL_GMLP = 2688
COL_Z = 3200
COL_XBC = 3456
COL_DT = 4224
N_COLS = 4352


def _cparams(n_axes, vmem_mb):
    return pltpu.CompilerParams(dimension_semantics=("arbitrary",) * n_axes,
                                vmem_limit_bytes=vmem_mb * 1024 * 1024)


def _sigmoid(x):
    return 1.0 / (1.0 + jnp.exp(-x))


def _gelu_tanh(x):
    return x * (0.5 * (1.0 + jnp.tanh(math.sqrt(2.0 / math.pi) * (x + 0.044715 * (x * x * x)))))


def _layer_norm(z, g, b):
    mu = jnp.mean(z, axis=-1, keepdims=True)
    d = z - mu
    var = jnp.mean(d * d, axis=-1, keepdims=True)
    return d * lax.rsqrt(var + LN_EPS) * g + b


def _dot(a, b):
    return jnp.dot(a, b, preferred_element_type=F32)


def _dot_hi(a, b):
    return jnp.dot(a, b, preferred_element_type=F32, precision=HI)


def _dot_nt(a, b):
    return lax.dot_general(a, b, (((1,), (1,)), ((), ())), preferred_element_type=F32)


def _mod_body(c_ref, w_ref, b_ref, o_ref):
    c = c_ref[...]
    o_ref[0] = _dot_hi(c * _sigmoid(c), w_ref[0]) + b_ref[0]


def _adaln_mod(c, ada_w, ada_b):
    n_layers, d, n = ada_w.shape
    bn = c.shape[0]
    tn = 1536
    cp = jnp.zeros((8, d), F32).at[:bn].set(c)
    out = pl.pallas_call(
        _mod_body,
        grid=(n_layers, n // tn),
        in_specs=[pl.BlockSpec((8, d), lambda l, j: (0, 0)),
                  pl.BlockSpec((1, d, tn), lambda l, j: (l, 0, j)),
                  pl.BlockSpec((1, 1, tn), lambda l, j: (l, 0, j))],
        out_specs=pl.BlockSpec((1, 8, tn), lambda l, j: (l, 0, j)),
        out_shape=jax.ShapeDtypeStruct((n_layers, 8, n), F32),
        compiler_params=_cparams(2, 32),
        name="adaln_mod",
    )(cp, ada_w, ada_b.reshape(n_layers, 1, n))
    return out[:, :bn].reshape(n_layers, bn, N_SUB * 3, d)


def _ffn_body(x_ref, mod_ref, w1_ref, w3_ref, w2_ref, g_ref, b_ref, o_ref, *, sub):
    x = x_ref[0]
    m = mod_ref[0]
    shift, scale, gate = m[3 * sub:3 * sub + 1], m[3 * sub + 1:3 * sub + 2], m[3 * sub + 2:3 * sub + 3]
    h = (x * (1.0 + scale) + shift).astype(BF16)
    acc = jnp.zeros(x.shape, F32)
    for j in range(D_FF // FF_CHUNK):
        sl = slice(j * FF_CHUNK, (j + 1) * FF_CHUNK)
        a = _dot(h, w1_ref[:, sl])
        b = _dot(h, w3_ref[:, sl])
        acc = acc + _dot((a * _sigmoid(a) * b).astype(BF16), w2_ref[sl, :])
    z = ALPHA * x + (0.5 * (1.0 + gate)) * acc
    o_ref[0] = _layer_norm(z, g_ref[...], b_ref[...])


def _ffn_sublayer(x, mod_l, w1, w3, w2, g, b, sub):
    bn, t, d = x.shape
    full = lambda shape: pl.BlockSpec(shape, lambda bi, i: (0,) * len(shape))
    return pl.pallas_call(
        functools.partial(_ffn_body, sub=sub),
        grid=(bn, t // TM),
        in_specs=[pl.BlockSpec((1, TM, d), lambda bi, i: (bi, i, 0)),
                  pl.BlockSpec((1, N_SUB * 3, d), lambda bi, i: (bi, 0, 0)),
                  full((d, D_FF)), full((d, D_FF)), full((D_FF, d)),
                  full((1, d)), full((1, d))],
        out_specs=pl.BlockSpec((1, TM, d), lambda bi, i: (bi, i, 0)),
        out_shape=jax.ShapeDtypeStruct(x.shape, F32),
        compiler_params=_cparams(2, 56),
        name="ffn",
    )(x, mod_l, w1.astype(BF16), w3.astype(BF16), w2.astype(BF16), g[None], b[None])


def _pad_cols(w, width):
    return jnp.pad(w, ((0, 0), (0, width - w.shape[1])))


def _relayout_w_in(w_in):
    cols = [_pad_cols(w_in[:, HEAD_DIM * h:HEAD_DIM * (h + 1)], LANES) for h in range(ATT_HEADS)]
    for i in range(6):
        for g in range(ATT_KV_GROUPS):
            base = 512 + 128 * i + HEAD_DIM * g
            cols.append(_pad_cols(w_in[:, base:base + HEAD_DIM], LANES))
    cols.append(_pad_cols(w_in[:, 1280:1304], LANES))
    cols.append(w_in[:, 1304:1816])
    cols.append(w_in[:, 1816:2072])
    cols.append(w_in[:, 2072:2840])
    cols.append(_pad_cols(w_in[:, 2840:2844], LANES))
    return jnp.concatenate(cols, axis=1).astype(BF16)


def _inproj_body(x_ref, mod_ref, w_ref, qp_ref, kcv_ref, ksa_ref, vst_ref, kwp_ref, vwt_ref,
                 gt_ref, gm_ref, z_ref, xbc_ref, dt_ref):
    i = pl.program_id(1)
    x = x_ref[0]
    m = mod_ref[0]
    h = (x * (1.0 + m[4:5]) + m[3:4]).astype(BF16)

    def proj(c0, width):
        return _dot(h, w_ref[:, c0:c0 + width])

    lane = lax.broadcasted_iota(jnp.int32, (1, LANES), 1)
    q_flag = jnp.where(lane == HEAD_DIM, NEG_BIG, 0.0)
    for hh in range(ATT_HEADS):
        qp_ref[0, hh] = (proj(COL_Q + LANES * hh, LANES) * (HEAD_DIM ** -0.5 * LOG2E) + q_flag).astype(BF16)
    for k in range(4):
        kcv_ref[0, k] = proj(COL_KV + LANES * k, LANES).astype(BF16)
    gates_t = _sigmoid(proj(COL_GATE, LANES)).T
    for g in range(ATT_KV_GROUPS):
        gt_ref[0, g] = gates_t[GATE_ROWS_PER_GROUP * g:GATE_ROWS_PER_GROUP * g + GATE_ROWS, :]
    gm_ref[0] = proj(COL_GMLP, 512)
    z_ref[0] = proj(COL_Z, D_SSM)
    xbc_ref[0] = proj(COL_XBC, D_XBC)
    dt_ref[0] = proj(COL_DT, LANES)

    @pl.when(i == 0)
    def _():
        dummy = jnp.broadcast_to(jnp.where(lane == HEAD_DIM, 1.0, 0.0), (TM, LANES)).astype(BF16)
        zeros = jnp.zeros((TM, LANES), BF16)
        for g in range(ATT_KV_GROUPS):
            ksa_ref[0, g, :, 0:LANES] = dummy
            ksa_ref[0, g, :, LANES:2 * LANES] = zeros
            kwp_ref[0, g] = dummy
            vst_ref[0, g] = jnp.zeros((LANES, TM), BF16)
            vwt_ref[0, g] = jnp.zeros((LANES, TM), BF16)

    @pl.when(i > 0)
    def _():
        pos = (i - 1) * TM + lax.broadcasted_iota(jnp.int32, (TM, LANES), 0)
        blk_lane = (pos >> 6) & (SEL_CHUNK_BLOCKS - 1)
        onehot = jnp.where(blk_lane == lax.broadcasted_iota(jnp.int32, (TM, LANES), 1), 1.0, 0.0).astype(BF16)
        ones_row = lax.broadcasted_iota(jnp.int32, (LANES, 1), 0) == HEAD_DIM
        for g in range(ATT_KV_GROUPS):
            ksa_ref[0, g, :, 0:LANES] = proj(COL_KV + LANES * (4 + g), LANES).astype(BF16)
            ksa_ref[0, g, :, LANES:2 * LANES] = onehot
            vst_ref[0, g] = jnp.where(ones_row, 1.0, proj(COL_KV + LANES * (6 + g), LANES).T).astype(BF16)
            kwp_ref[0, g] = proj(COL_KV + LANES * (8 + g), LANES).astype(BF16)
            vwt_ref[0, g] = jnp.where(ones_row, 1.0, proj(COL_KV + LANES * (10 + g), LANES).T).astype(BF16)


def _mixer_inproj(x, mod_l, w_in):
    bn, t, d = x.shape
    tp = t + KPAD
    row = lambda width: pl.BlockSpec((1, TM, width), lambda bi, i: (bi, jnp.maximum(i - 1, 0), 0))
    sds = jax.ShapeDtypeStruct
    return pl.pallas_call(
        _inproj_body,
        grid=(bn, t // TM + 1),
        in_specs=[row(d),
                  pl.BlockSpec((1, N_SUB * 3, d), lambda bi, i: (bi, 0, 0)),
                  pl.BlockSpec((d, N_COLS), lambda bi, i: (0, 0))],
        out_specs=[pl.BlockSpec((1, ATT_HEADS, TM, LANES), lambda bi, i: (bi, 0, jnp.maximum(i - 1, 0), 0)),
                   pl.BlockSpec((1, 4, TM, LANES), lambda bi, i: (bi, 0, jnp.maximum(i - 1, 0), 0)),
                   pl.BlockSpec((1, ATT_KV_GROUPS, TM, 2 * LANES), lambda bi, i: (bi, 0, i, 0)),
                   pl.BlockSpec((1, ATT_KV_GROUPS, LANES, TM), lambda bi, i: (bi, 0, 0, i)),
                   pl.BlockSpec((1, ATT_KV_GROUPS, TM, LANES), lambda bi, i: (bi, 0, i, 0)),
                   pl.BlockSpec((1, ATT_KV_GROUPS, LANES, TM), lambda bi, i: (bi, 0, 0, i)),
                   pl.BlockSpec((1, ATT_KV_GROUPS, GATE_ROWS, TM), lambda bi, i: (bi, 0, 0, jnp.maximum(i - 1, 0))),
                   row(512), row(D_SSM), row(D_XBC), row(LANES)],
        out_shape=[sds((bn, ATT_HEADS, t, LANES), BF16),
                   sds((bn, 4, t, LANES), BF16),
                   sds((bn, ATT_KV_GROUPS, tp, 2 * LANES), BF16),
                   sds((bn, ATT_KV_GROUPS, LANES, tp), BF16),
                   sds((bn, ATT_KV_GROUPS, tp, LANES), BF16),
                   sds((bn, ATT_KV_GROUPS, LANES, tp), BF16),
                   sds((bn, ATT_KV_GROUPS, GATE_ROWS, t), F32), sds((bn, t, 512), F32), sds((bn, t, D_SSM), F32),
                   sds((bn, t, D_XBC), F32), sds((bn, t, LANES), F32)],
        compiler_params=_cparams(2, 48),
        name="mixer_inproj",
    )(x, mod_l, w_in)


def _cmp_rows(t):
    return -(-(t // CMP_STRIDE + 120) // LANES) * LANES


def _compress_body(ck_ref, cv_ref, w1_ref, pe_ref, b1_ref, w2_ref, ovt_ref, k_ref, lhs_ref):
    n_rows = ck_ref.shape[2]
    rows = k_ref.shape[2]
    n_cmp = n_rows - CMP_LEN // CMP_STRIDE + 1
    half = CMP_STRIDE * HEAD_DIM
    row = lax.broadcasted_iota(jnp.int32, (rows, LANES), 0)
    lane = lax.broadcasted_iota(jnp.int32, (rows, LANES), 1)
    real = (row >= CMP_KPAD) & (row < CMP_KPAD + n_cmp)

    def mlp(c, j):
        w1 = w1_ref[j]
        h1 = _dot(c, w1[:half])
        h2 = _dot(c, w1[half:])
        cvec = _dot(pe_ref[j], w1)[0:1] + b1_ref[j]
        hid = h1 + pltpu.roll(h2, n_rows - 1, 0) + cvec
        out = _dot(_gelu_tanh(hid).astype(BF16), w2_ref[j])
        tail = rows - CMP_KPAD - n_rows
        full = jnp.concatenate([jnp.zeros((CMP_KPAD, LANES), F32), out, jnp.zeros((tail, LANES), F32)], axis=0)
        return jnp.where(real, full, 0.0)

    flags = jnp.where(lane == HEAD_DIM, jnp.where(real, 0.0, 1.0),
                      jnp.where(lane - CMP_FLAG0 == (row >> CMP_CHUNK_LOG2), 1.0, 0.0))
    k_ref[0, 0] = (mlp(ck_ref[0, 0], 0) + flags).astype(BF16)
    ones_row = lax.broadcasted_iota(jnp.int32, (LANES, 1), 0) == HEAD_DIM
    lhs_ref[0, 0, 0:LANES, :] = jnp.where(ones_row, 1.0, mlp(cv_ref[0, 0], 1).T).astype(BF16)
    lhs_ref[0, 0, LANES:, :] = ovt_ref[...].astype(BF16)


def _compress(kcv, cmp_pe, cmp_w1, cmp_b1, cmp_w2, ovt):
    bn, _, t, _ = kcv.shape
    n_rows = t // CMP_STRIDE
    width = CMP_STRIDE * HEAD_DIM
    c = kcv[..., :HEAD_DIM].reshape(bn, 4, n_rows, width)
    pe = jnp.zeros((2, 8, CMP_LEN * HEAD_DIM), F32).at[:, 0].set(cmp_pe.reshape(2, -1)).astype(BF16)
    w2 = jnp.pad(cmp_w2, ((0, 0), (0, 0), (0, LANES - HEAD_DIM))).astype(BF16)
    n_slc, rows = ovt.shape
    full = lambda shape: pl.BlockSpec(shape, lambda bi, g: (0,) * len(shape))
    return pl.pallas_call(
        _compress_body,
        grid=(bn, ATT_KV_GROUPS),
        in_specs=[pl.BlockSpec((1, 1, n_rows, width), lambda bi, g: (bi, g, 0, 0)),
                  pl.BlockSpec((1, 1, n_rows, width), lambda bi, g: (bi, ATT_KV_GROUPS + g, 0, 0)),
                  full((2, CMP_LEN * HEAD_DIM, CMP_HIDDEN)), full((2, 8, CMP_LEN * HEAD_DIM)),
                  full((2, 1, CMP_HIDDEN)), full((2, CMP_HIDDEN, LANES)), full((n_slc, rows))],
        out_specs=[pl.BlockSpec((1, 1, rows, LANES), lambda bi, g: (bi, g, 0, 0)),
                   pl.BlockSpec((1, 1, LANES + n_slc, rows), lambda bi, g: (bi, g, 0, 0))],
        out_shape=[jax.ShapeDtypeStruct((bn, ATT_KV_GROUPS, rows, LANES), BF16),
                   jax.ShapeDtypeStruct((bn, ATT_KV_GROUPS, LANES + n_slc, rows), BF16)],
        compiler_params=_cparams(2, 40),
        name="nsa_compress",
    )(c, c, cmp_w1.astype(BF16), pe, cmp_b1[:, None, :], w2, ovt)


def _rel_bucket(dist):
    n = jnp.maximum(dist, 0)
    max_exact = REL_BUCKETS // 2
    nf = jnp.maximum(n, 1).astype(F32)
    large = max_exact + (jnp.log(nf / max_exact) / math.log(REL_MAX_DIST / max_exact)
                         * (REL_BUCKETS - max_exact)).astype(jnp.int32)
    large = jnp.minimum(large, REL_BUCKETS - 1)
    return jnp.where(n < max_exact, n, large)


def _bias_tables(rel_bias):
    far = rel_bias[REL_BUCKETS - 1]

    def table(dist, valid, shift):
        onehot = (_rel_bucket(dist)[..., None] == jnp.arange(REL_BUCKETS)).astype(F32)
        b = jnp.einsum("qkb,bh->hqk", onehot, rel_bias, precision=HI)
        if shift:
            b = b - far[:, None, None]
        return jnp.where(valid[None], b * LOG2E, -jnp.inf)

    n_var = (1 << CMP_CHUNK_LOG2) * CMP_STRIDE // TQ_C
    d_c = jnp.asarray(TQ_C * np.arange(n_var)[:, None, None] + np.arange(TQ_C)[None, None, :]
                      - CMP_STRIDE * np.arange(CMP_NEAR)[None, :, None]
                      + (CMP_STRIDE * CMP_KPAD - CMP_LEN + 1))
    nb_c = jnp.stack([table(d_c[v], d_c[v] >= 0, True) for v in range(n_var)])
    nb_c = jnp.maximum(nb_c, NEG_BIG)
    ki = np.arange(2 * TK)[:, None]
    d_s = jnp.asarray(np.arange(TQ)[None, :] - ki + TK)
    nb_s = table(d_s, d_s >= 0, True)
    d_w = jnp.asarray(np.arange(TQ)[None, :] - np.arange(TQ + WINDOW)[:, None] + WINDOW)
    nb_w = table(d_w, (d_w >= 0) & (d_w < WINDOW), False)
    return nb_c, nb_s, nb_w


def _overlap_matrix(t):
    n_cmp = (t - CMP_LEN) // CMP_STRIDE + 1
    n_slc = t // SLC_BLOCK
    cs = np.arange(n_cmp) * CMP_STRIDE
    ss = np.arange(n_slc) * SLC_BLOCK
    ov = np.clip(np.minimum(cs[:, None] + CMP_LEN, ss[None, :] + SLC_BLOCK)
                 - np.maximum(cs[:, None], ss[None, :]), 0, None) / CMP_LEN
    full = np.zeros((n_slc, _cmp_rows(t)), np.float32)
    full[:, CMP_KPAD:CMP_KPAD + n_cmp] = ov.T
    return jnp.asarray(full)


def _gate_row(gt_ref, r, branch, qs):
    row = 3 * r + branch
    return gt_ref[0, 0, row:row + 1, qs]


def _cmp_attn_body(q_ref, k_ref, lhs_ref, nba_ref, nbb_ref, gt_ref, o_ref, imp_ref, s_ref):
    step = pl.program_id(2)
    rows = k_ref.shape[2]
    chunk = 1 << CMP_CHUNK_LOG2
    nb_refs = (nba_ref, nbb_ref)
    rowi = lax.broadcasted_iota(jnp.int32, (LANES, TQ_C), 0)

    def near_chunk_of(sub):
        return ((CMP_SUBTILES * step + sub) * (TQ_C // CMP_STRIDE)) >> CMP_CHUNK_LOG2

    def body(n_keys):
        kf = k_ref[0, 0, 0:n_keys, :]
        lhs = lhs_ref[0, 0, :, 0:n_keys]
        for sub in range(CMP_SUBTILES):
            near_chunk = near_chunk_of(sub)
            chunk_flag = jnp.where((rowi >= CMP_FLAG0) & (rowi - CMP_FLAG0 > near_chunk + 1), NEG_BIG, 0.0)
            qs = slice(sub * TQ_C, (sub + 1) * TQ_C)
            q_all = jnp.concatenate([(q_ref[0, r, qs, :].astype(F32).T + chunk_flag).astype(BF16)
                                     for r in range(ATT_HPG)], axis=1)
            s_ref[sub, 0:n_keys, :] = _dot(kf, q_all)
        for sub in range(CMP_SUBTILES):
            qs = slice(sub * TQ_C, (sub + 1) * TQ_C)
            a0 = pl.multiple_of(near_chunk_of(sub) << CMP_CHUNK_LOG2, chunk)
            ps = []
            for r in range(ATT_HPG):
                hs = slice(r * TQ_C, (r + 1) * TQ_C)
                s_ref[sub, pl.ds(a0, CMP_NEAR), hs] += nb_refs[sub][0, r]
                s = s_ref[sub, 0:n_keys, hs]
                mx = jnp.max(s, axis=0, keepdims=True)
                mx = jnp.where(mx < 0.5 * NEG_BIG, 0.0, mx)
                ps.append(jnp.exp2(s - mx).astype(BF16))
            res_all = _dot(lhs, jnp.concatenate(ps, axis=1))
            imp = jnp.zeros((imp_ref.shape[2], TQ_C), F32)
            outs = []
            for r in range(ATT_HPG):
                res = res_all[:, r * TQ_C:(r + 1) * TQ_C]
                inv = 1.0 / jnp.maximum(res[HEAD_DIM:HEAD_DIM + 1, :], 1e-30)
                outs.append((res[0:LANES, :] * inv * _gate_row(gt_ref, r, 0, qs)).T[:, :HEAD_DIM])
                imp = imp + res[LANES:, :] * inv
            o_ref[0, qs, :] = jnp.concatenate(outs, axis=1)
            imp_ref[0, 0, :, qs] = imp

    need = (near_chunk_of(CMP_SUBTILES - 1) + 2) * chunk
    sizes = sorted({min(rows, s) for s in CMP_KEY_SIZES} | {rows})
    lo = 0
    for n_keys in sizes:
        @pl.when((need > lo) & (need <= n_keys))
        def _(n_keys=n_keys):
            body(n_keys)
        lo = n_keys


def _gate_spec(tq):
    return pl.BlockSpec((1, 1, GATE_ROWS, tq), lambda bi, g, m: (bi, g, 0, m))


def _cmp_attention(qp, kcp, lhs, nb_c, gates_t):
    bn, _, t, _ = qp.shape
    rows = kcp.shape[2]
    n_slc = lhs.shape[2] - LANES
    n_var = nb_c.shape[0]
    tq = CMP_SUBTILES * TQ_C
    table = lambda sub: pl.BlockSpec((1, ATT_HPG, CMP_NEAR, TQ_C),
                                     lambda bi, g, m: ((CMP_SUBTILES * m + sub) % n_var, g, 0, 0))
    return pl.pallas_call(
        _cmp_attn_body,
        grid=(bn, ATT_KV_GROUPS, t // tq),
        in_specs=[pl.BlockSpec((1, ATT_HPG, tq, LANES), lambda bi, g, m: (bi, g, m, 0)),
                  pl.BlockSpec((1, 1, rows, LANES), lambda bi, g, m: (bi, g, 0, 0)),
                  pl.BlockSpec((1, 1, LANES + n_slc, rows), lambda bi, g, m: (bi, g, 0, 0)),
                  table(0), table(1), _gate_spec(tq)],
        out_specs=[pl.BlockSpec((1, tq, ATT_HPG * HEAD_DIM), lambda bi, g, m: (bi, m, g)),
                   pl.BlockSpec((1, 1, n_slc, tq), lambda bi, g, m: (bi, g, 0, m))],
        out_shape=[jax.ShapeDtypeStruct((bn, t, ATT_HEADS * HEAD_DIM), F32),
                   jax.ShapeDtypeStruct((bn, ATT_KV_GROUPS, n_slc, t), F32)],
        scratch_shapes=[pltpu.VMEM((CMP_SUBTILES, rows, ATT_HPG * TQ_C), F32)],
        compiler_params=_cparams(3, 48),
        name="nsa_cmp_attn",
    )(qp, kcp, lhs, nb_c, nb_c, gates_t)


TOPK_COLS = 1024
TOPK_ROW_SIZES = (64, 128, 192)


def _topk_body(imp_ref, o_ref):
    i = pl.program_id(2)
    n_slc, cols = imp_ref.shape[2:]
    t = i * cols + lax.broadcasted_iota(jnp.int32, (1, cols), 1)
    cur = t >> 6

    def body(n_rows):
        blk = lax.broadcasted_iota(jnp.int32, (n_rows, cols), 0)
        forced = (blk == 0) | (blk == cur) | (blk == cur - 1)
        vals = jnp.where(forced, SEL_FORCE, jnp.where(blk <= cur, imp_ref[0, 0, 0:n_rows, :], -SEL_FORCE))
        blkf = blk.astype(F32)
        for _ in range(min(SLC_TOPN, n_rows)):
            mx = jnp.max(vals, axis=0, keepdims=True)
            first = jnp.min(jnp.where(vals == mx, blkf, float(n_slc)), axis=0, keepdims=True)
            vals = jnp.where(blkf == first, -jnp.inf, vals)
        o_ref[0, 0, 0:n_rows, :] = jnp.where(vals == -jnp.inf, 0.0, NEG_BIG).astype(BF16)
        if n_rows < n_slc:
            o_ref[0, 0, n_rows:, :] = jnp.full((n_slc - n_rows, cols), NEG_BIG, BF16)

    need = ((i + 1) * cols) >> 6
    sizes = sorted({min(n_slc, s) for s in TOPK_ROW_SIZES} | {n_slc})
    lo = 0
    for n_rows in sizes:
        @pl.when((need > lo) & (need <= n_rows))
        def _(n_rows=n_rows):
            body(n_rows)
        lo = n_rows


def _select_blocks(imp_t):
    bn, g, n_slc, t = imp_t.shape
    cols = min(TOPK_COLS, t)
    spec = pl.BlockSpec((1, 1, n_slc, cols), lambda bi, gi, i: (bi, gi, 0, i))
    return pl.pallas_call(
        _topk_body,
        grid=(bn, g, t // cols),
        in_specs=[spec],
        out_specs=spec,
        out_shape=jax.ShapeDtypeStruct(imp_t.shape, BF16),
        compiler_params=_cparams(3, 40),
        name="nsa_topk",
    )(imp_t)


def _sel_attn_body(q_ref, sel_ref, k_ref, vt_ref, nb_ref, gt_ref, o_ref, qat_ref, m_ref, acc_ref, sc0_ref,
                   sc1_ref):
    sc_refs = (sc0_ref, sc1_ref)
    pair = pl.program_id(2)
    n_chunks = qat_ref.shape[0]
    m_a = SEL_SUBTILES * pair
    for sub in range(SEL_SUBTILES):
        qs = slice(sub * TQ, (sub + 1) * TQ)
        for r in range(ATT_HPG):
            q_t = q_ref[0, r, qs, :].astype(F32).T.astype(BF16)
            for c in range(n_chunks):
                qat_ref[c, ATT_HPG * sub + r, 0:LANES, :] = q_t
        for c in range(n_chunks):
            sel_t = sel_ref[0, 0, c * LANES:(c + 1) * LANES, qs]
            for r in range(ATT_HPG):
                qat_ref[c, ATT_HPG * sub + r, LANES:2 * LANES, :] = sel_t
    m_ref[...] = jnp.full(m_ref.shape, -jnp.inf, F32)
    acc_ref[...] = jnp.zeros(acc_ref.shape, F32)

    tiles_per_chunk = SEL_CHUNK_BLOCKS * SLC_BLOCK // TK
    units_a = tuple(range(ATT_HPG))
    units_b = tuple(range(ATT_HPG, 2 * ATT_HPG))
    bias_hi = lambda r: nb_ref[r, TK:2 * TK, :]
    bias_lo = lambda r: nb_ref[r, 0:TK, :]

    def load(j, n_tiles):
        row0 = pl.multiple_of(KPAD + TK * j, TK)
        return (k_ref[0, 0, pl.ds(row0, n_tiles * TK), :], vt_ref[0, 0, :, pl.ds(row0, n_tiles * TK)],
                jnp.maximum(j, 0) // tiles_per_chunk)

    def qk(tile, u):
        kt, _, c = tile
        return _dot(kt, qat_ref[c, u])

    def update(tile, u, s, bias):
        if bias is not None:
            s = s + bias(u % ATT_HPG)
        m_prev = m_ref[u]
        m_new = jnp.maximum(m_prev, jnp.max(s, axis=0, keepdims=True))
        alpha = jnp.exp2(m_prev - m_new)
        p = jnp.exp2(s - m_new).astype(BF16)
        acc_ref[u] = alpha * acc_ref[u] + _dot(tile[1], p)
        m_ref[u] = m_new

    def step(work):
        todo = [(tile, u, bias) for tile, units, bias in work for u in units]
        scores = [qk(tile, u) for tile, u, _ in todo[:SEL_AHEAD]]
        for i, (tile, u, bias) in enumerate(todo):
            if i + SEL_AHEAD < len(todo):
                scores.append(qk(todo[i + SEL_AHEAD][0], todo[i + SEL_AHEAD][1]))
            s = scores[i]
            scores[i] = None
            update(tile, u, s, bias)

    def pipelined_step(tile, slot, next_tile):
        for i, u in enumerate(all_units):
            if next_tile is not None:
                for k in range(2 * i, min(2 * i + 2, len(all_units))):
                    sc_refs[1 - slot][k] = qk(next_tile, all_units[k])
            update(tile, u, sc_refs[slot][i], None)

    all_units = units_a + units_b
    step([(load(m_a, 1), units_a, bias_hi), (load(m_a + 1, 1), units_b, bias_hi)])
    step([(load(m_a - 1, 1), units_a, bias_lo), (load(m_a, 1), units_b, bias_lo)])
    step([(load(m_a - 1, 1), units_b, None)])

    n_common = jnp.maximum(m_a - 1, 0)
    n_big = n_common // SEL_FAR_TILES
    far_tile = lambda i: load(SEL_FAR_TILES * i, SEL_FAR_TILES)

    @pl.when(n_big > 0)
    def _():
        first = far_tile(0)
        for k, u in enumerate(all_units):
            sc_refs[0][k] = qk(first, u)
        n_pairs = (n_big - 1) // 2

        def far_pair(i, carry):
            pipelined_step(far_tile(2 * i), 0, far_tile(2 * i + 1))
            pipelined_step(far_tile(2 * i + 1), 1, far_tile(2 * i + 2))
            return carry

        lax.fori_loop(0, n_pairs, far_pair, 0)

        @pl.when(n_big - 2 * n_pairs == 1)
        def _():
            pipelined_step(far_tile(n_big - 1), 0, None)

        @pl.when(n_big - 2 * n_pairs == 2)
        def _():
            pipelined_step(far_tile(n_big - 2), 0, far_tile(n_big - 1))
            pipelined_step(far_tile(n_big - 1), 1, None)

    def far_one(j, carry):
        step([(load(j, 1), all_units, None)])
        return carry

    lax.fori_loop(SEL_FAR_TILES * n_big, n_common, far_one, 0)
    for sub in range(SEL_SUBTILES):
        outs = []
        for r in range(ATT_HPG):
            acc = acc_ref[ATT_HPG * sub + r]
            gate = _gate_row(gt_ref, r, 1, slice(sub * TQ, (sub + 1) * TQ))
            outs.append((acc / acc[HEAD_DIM:HEAD_DIM + 1, :] * gate).T[:, :HEAD_DIM])
        o_ref[0, sub * TQ:(sub + 1) * TQ, :] = jnp.concatenate(outs, axis=1)


def _sel_attention(qp, selneg, ksa, vst, nb_s, gates_t):
    bn, _, t, _ = qp.shape
    tp = ksa.shape[2]
    n_slc = selneg.shape[2]
    n_chunks = max(n_slc // SEL_CHUNK_BLOCKS, 1)
    tq = SEL_SUBTILES * TQ
    n_units = SEL_SUBTILES * ATT_HPG
    return pl.pallas_call(
        _sel_attn_body,
        grid=(bn, ATT_KV_GROUPS, t // tq),
        in_specs=[pl.BlockSpec((1, ATT_HPG, tq, LANES), lambda bi, g, m: (bi, g, m, 0)),
                  pl.BlockSpec((1, 1, n_slc, tq), lambda bi, g, m: (bi, g, 0, m)),
                  pl.BlockSpec((1, 1, tp, 2 * LANES), lambda bi, g, m: (bi, g, 0, 0)),
                  pl.BlockSpec((1, 1, LANES, tp), lambda bi, g, m: (bi, g, 0, 0)),
                  pl.BlockSpec((ATT_HPG, 2 * TK, TQ), lambda bi, g, m: (g, 0, 0)),
                  _gate_spec(tq)],
        out_specs=pl.BlockSpec((1, tq, ATT_HPG * HEAD_DIM), lambda bi, g, m: (bi, m, g)),
        out_shape=jax.ShapeDtypeStruct((bn, t, ATT_HEADS * HEAD_DIM), F32),
        scratch_shapes=[pltpu.VMEM((n_chunks, n_units, 2 * LANES, TQ), BF16),
                        pltpu.VMEM((n_units, 1, TQ), F32),
                        pltpu.VMEM((n_units, LANES, TQ), F32),
                        pltpu.VMEM((n_units, SEL_FAR_TILES * TK, TQ), F32),
                        pltpu.VMEM((n_units, SEL_FAR_TILES * TK, TQ), F32)],
        compiler_params=_cparams(3, 56),
        name="nsa_sel_attn",
    )(qp, selneg, ksa, vst, nb_s, gates_t)


def _win_attn_body(q_ref, k_ref, vt_ref, nb_ref, gt_ref, o_ref):
    step = pl.program_id(2)
    keys = TQ + WINDOW
    tiles = []
    for sub in range(WIN_SUBTILES):
        row0 = pl.multiple_of((WIN_SUBTILES * step + sub) * TQ, TQ)
        kt = k_ref[0, 0, pl.ds(row0, keys), :]
        vt = vt_ref[0, 0, :, pl.ds(row0, keys)]
        qs = slice(sub * TQ, (sub + 1) * TQ)
        q_all = jnp.concatenate([q_ref[0, r, qs, :].astype(F32).T.astype(BF16) for r in range(ATT_HPG)], axis=1)
        tiles.append((vt, qs, _dot(kt, q_all)))
    for vt, qs, s_all in tiles:
        ps = []
        for r in range(ATT_HPG):
            s = s_all[:, r * TQ:(r + 1) * TQ] + nb_ref[r]
            ps.append(jnp.exp2(s - jnp.max(s, axis=0, keepdims=True)).astype(BF16))
        res_all = _dot(vt, jnp.concatenate(ps, axis=1))
        outs = []
        for r in range(ATT_HPG):
            res = res_all[:, r * TQ:(r + 1) * TQ]
            gate = _gate_row(gt_ref, r, 2, qs)
            outs.append((res / res[HEAD_DIM:HEAD_DIM + 1, :] * gate).T[:, :HEAD_DIM])
        o_ref[0, qs, :] = jnp.concatenate(outs, axis=1)


def _win_attention(qp, kwp, vwt, nb_w, gates_t):
    bn, _, t, _ = qp.shape
    tp = kwp.shape[2]
    tq = WIN_SUBTILES * TQ
    return pl.pallas_call(
        _win_attn_body,
        grid=(bn, ATT_KV_GROUPS, t // tq),
        in_specs=[pl.BlockSpec((1, ATT_HPG, tq, LANES), lambda bi, g, m: (bi, g, m, 0)),
                  pl.BlockSpec((1, 1, tp, LANES), lambda bi, g, m: (bi, g, 0, 0)),
                  pl.BlockSpec((1, 1, LANES, tp), lambda bi, g, m: (bi, g, 0, 0)),
                  pl.BlockSpec((ATT_HPG, TQ + WINDOW, TQ), lambda bi, g, m: (g, 0, 0)),
                  _gate_spec(tq)],
        out_specs=pl.BlockSpec((1, tq, ATT_HPG * HEAD_DIM), lambda bi, g, m: (bi, m, g)),
        out_shape=jax.ShapeDtypeStruct((bn, t, ATT_HEADS * HEAD_DIM), F32),
        compiler_params=_cparams(3, 48),
        name="nsa_win_attn",
    )(qp, kwp, vwt, nb_w, gates_t)


GMLP_ROWS = 1024


def _gmlp_body(uv_ref, g_ref, b_ref, ws_ref, bs_ref, o_ref):
    act = _gelu_tanh(uv_ref[0])
    u = act[:, :D_GMLP]
    v = _layer_norm(act[:, D_GMLP:], g_ref[...], b_ref[...])
    ti = lax.broadcasted_iota(jnp.int32, (GMLP_CHUNK, GMLP_CHUNK), 0)
    si = lax.broadcasted_iota(jnp.int32, (GMLP_CHUNK, GMLP_CHUNK), 1)
    w = [jnp.where(ti >= si, ws_ref[g], 0.0).astype(BF16) for g in range(GMLP_GROUPS)]
    lane = lax.broadcasted_iota(jnp.int32, (1, LANES), 1)
    lo = lane < HEAD_DIM
    for c in range(uv_ref.shape[1] // GMLP_CHUNK):
        rs = slice(c * GMLP_CHUNK, (c + 1) * GMLP_CHUNK)
        pairs = []
        for pr in range(GMLP_GROUPS // 2):
            vp = v[rs, pr * LANES:(pr + 1) * LANES]
            pairs.append(_dot(w[2 * pr], jnp.where(lo, vp, 0.0).astype(BF16))
                         + _dot(w[2 * pr + 1], jnp.where(lo, 0.0, vp).astype(BF16)))
        o_ref[0, rs, :] = u[rs] * (jnp.concatenate(pairs, axis=1) + bs_ref[...])


def _gmlp(uv, ln_g, ln_b, ws, bs):
    bn, t, _ = uv.shape
    rows = min(GMLP_ROWS, t)
    bs_lanes = jnp.repeat(bs.T, D_GMLP // GMLP_GROUPS, axis=1)
    full = lambda shape: pl.BlockSpec(shape, lambda bi, i: (0,) * len(shape))
    return pl.pallas_call(
        _gmlp_body,
        grid=(bn, t // rows),
        in_specs=[pl.BlockSpec((1, rows, 2 * D_GMLP), lambda bi, i: (bi, i, 0)),
                  full((1, D_GMLP)), full((1, D_GMLP)),
                  full((GMLP_GROUPS, GMLP_CHUNK, GMLP_CHUNK)), full((GMLP_CHUNK, D_GMLP))],
        out_specs=pl.BlockSpec((1, rows, D_GMLP), lambda bi, i: (bi, i, 0)),
        out_shape=jax.ShapeDtypeStruct((bn, t, D_GMLP), F32),
        compiler_params=_cparams(2, 40),
        name="gmlp",
    )(uv, ln_g[None], ln_b[None], ws, bs_lanes)


def _ssd_body(z_ref, xbc_ref, prev_ref, dt_ref, cw_ref, cb_ref, dtb_ref, alog_ref, dsk_ref, ng_ref, e4_ref,
              o_ref, xe_ref, state_ref):
    c = pl.program_id(1)
    L = SSM_CHUNK

    @pl.when(c == 0)
    def _():
        state_ref[...] = jnp.zeros(state_ref.shape, F32)

    xe_ref[0:8] = jnp.where(c > 0, prev_ref[0], 0.0)
    xe_ref[8:8 + L] = xbc_ref[0]
    conv = cb_ref[...]
    for k in range(SSM_CONV):
        conv = conv + cw_ref[k:k + 1, :] * xe_ref[pl.ds(8 - (SSM_CONV - 1) + k, L), :]
    xc = conv * _sigmoid(conv)
    xs, bm, cm = xc[:, :D_SSM], xc[:, D_SSM:D_SSM + 2 * SSM_STATE], xc[:, D_SSM + 2 * SSM_STATE:]

    dtr = dt_ref[0] + dtb_ref[...]
    dt = jnp.maximum(dtr, 0.0) + jnp.log1p(jnp.exp(-jnp.abs(dtr)))
    a = -jnp.exp(alog_ref[...]) * dt
    ti = lax.broadcasted_iota(jnp.int32, (L, L), 0)
    si = lax.broadcasted_iota(jnp.int32, (L, L), 1)
    causal = ti >= si
    cs = _dot_hi(jnp.where(causal, 1.0, 0.0), a)
    cs_t = cs.T
    e4 = e4_ref[...]
    dt_e = _dot_hi(dt, e4)
    cs_e = _dot_hi(cs, e4)
    last_e = cs_e[L - 1:L, :]
    x_dt = xs * dt_e
    x_dec = (x_dt * jnp.exp(last_e - cs_e)).astype(BF16)
    x_dt = x_dt.astype(BF16)
    grow = jnp.exp(cs_e)
    lane = lax.broadcasted_iota(jnp.int32, (1, LANES), 1)
    lo = lane < SSM_HEAD_DIM
    zero = jnp.zeros((), BF16)

    ys = []
    for g in range(2):
        gs = slice(g * LANES, (g + 1) * LANES)
        cg = cm[:, gs].astype(BF16)
        bg = bm[:, gs]
        scores = _dot_nt(cg, bg.astype(BF16))
        xp = x_dt[:, gs]
        y = None
        for hh in range(2):
            h = 2 * g + hh
            seg = jnp.broadcast_to(cs[:, h:h + 1], (L, L)) - jnp.broadcast_to(cs_t[h:h + 1, :], (L, L))
            w = (scores * jnp.where(causal, jnp.exp(seg), 0.0)).astype(BF16)
            xh = jnp.where(lo, xp, zero) if hh == 0 else jnp.where(lo, zero, xp)
            yh = _dot(w, xh)
            y = yh if y is None else y + yh
        state = state_ref[g]
        y = y + _dot(cg, state.astype(BF16)) * grow[:, gs]
        state_ref[g] = jnp.exp(last_e[:, gs]) * state + _dot(bg.T.astype(BF16), x_dec[:, gs])
        ys.append(y)
    y = jnp.concatenate(ys, axis=1) + xs * dsk_ref[...]
    zz = z_ref[0]
    gg = y * (zz * _sigmoid(zz))
    outs = []
    for g in range(2):
        gp = gg[:, g * LANES:(g + 1) * LANES]
        outs.append(gp * lax.rsqrt(jnp.mean(gp * gp, axis=-1, keepdims=True) + LN_EPS))
    o_ref[0] = jnp.concatenate(outs, axis=1) * ng_ref[...]


def _head_lanes(v):
    return jnp.repeat(v, SSM_HEAD_DIM)[None]


def _ssd(z, xbc, dt, conv_w, conv_b, dt_bias, a_log, d_skip, norm_g):
    bn, t, _ = z.shape
    L = SSM_CHUNK
    first_lanes = lambda v: jnp.zeros((1, LANES), F32).at[0, :SSM_HEADS].set(v)
    e4 = np.zeros((LANES, D_SSM), np.float32)
    for h in range(SSM_HEADS):
        e4[h, h * SSM_HEAD_DIM:(h + 1) * SSM_HEAD_DIM] = 1.0
    full = lambda shape: pl.BlockSpec(shape, lambda bi, c: (0,) * len(shape))
    return pl.pallas_call(
        _ssd_body,
        grid=(bn, t // L),
        in_specs=[pl.BlockSpec((1, L, D_SSM), lambda bi, c: (bi, c, 0)),
                  pl.BlockSpec((1, L, D_XBC), lambda bi, c: (bi, c, 0)),
                  pl.BlockSpec((1, 8, D_XBC), lambda bi, c: (bi, jnp.maximum(c * (L // 8) - 1, 0), 0)),
                  pl.BlockSpec((1, L, LANES), lambda bi, c: (bi, c, 0)),
                  full((SSM_CONV, D_XBC)), full((1, D_XBC)), full((1, LANES)), full((1, LANES)),
                  full((1, D_SSM)), full((1, D_SSM)), full((LANES, D_SSM))],
        out_specs=pl.BlockSpec((1, L, D_SSM), lambda bi, c: (bi, c, 0)),
        out_shape=jax.ShapeDtypeStruct((bn, t, D_SSM), F32),
        scratch_shapes=[pltpu.VMEM((L + 8, D_XBC), F32),
                        pltpu.VMEM((2, SSM_STATE, LANES), F32)],
        compiler_params=_cparams(2, 40),
        name="ssd",
    )(z, xbc, xbc, dt, conv_w, conv_b[None], first_lanes(dt_bias), first_lanes(a_log),
      _head_lanes(d_skip), norm_g[None], jnp.asarray(e4))


def _outproj_body(x_ref, mod_ref, oc_ref, os_ref, ow_ref, gm_ref, ssm_ref, w_ref, g_ref, b_ref, o_ref):
    x = x_ref[0]
    m = mod_ref[0]
    gate = m[5:6]
    d_att = ATT_HEADS * HEAD_DIM
    o_att = oc_ref[0] + os_ref[0] + ow_ref[0]
    y = (_dot(o_att.astype(BF16), w_ref[0:d_att, :])
         + _dot(gm_ref[0].astype(BF16), w_ref[d_att:d_att + D_GMLP, :])
         + _dot(ssm_ref[0].astype(BF16), w_ref[d_att + D_GMLP:, :]))
    z = ALPHA * x + (1.0 + gate) * y
    o_ref[0] = _layer_norm(z, g_ref[...], b_ref[...])


def _mixer_outproj(x, mod_l, o_c, o_s, o_w, o_gmlp, o_ssm, w_out, g, b):
    bn, t, d = x.shape
    d_att = ATT_HEADS * HEAD_DIM
    row = lambda width: pl.BlockSpec((1, TM, width), lambda bi, i: (bi, i, 0))
    full = lambda shape: pl.BlockSpec(shape, lambda bi, i: (0,) * len(shape))
    return pl.pallas_call(
        _outproj_body,
        grid=(bn, t // TM),
        in_specs=[row(d), pl.BlockSpec((1, N_SUB * 3, d), lambda bi, i: (bi, 0, 0)),
                  row(d_att), row(d_att), row(d_att),
                  row(D_GMLP), row(D_SSM), full((d, d)), full((1, d)), full((1, d))],
        out_specs=row(d),
        out_shape=jax.ShapeDtypeStruct(x.shape, F32),
        compiler_params=_cparams(2, 48),
        name="mixer_outproj",
    )(x, mod_l, o_c, o_s, o_w, o_gmlp, o_ssm, w_out.astype(BF16), g[None], b[None])


def _nsa(qp, kcv, ksa, vst, kwp, vwt, gates_t, tables, ovt, cmp_pe, cmp_w1, cmp_b1, cmp_w2):
    nb_c, nb_s, nb_w = tables
    kcp, lhs = _compress(kcv, cmp_pe, cmp_w1, cmp_b1, cmp_w2, ovt)
    o_c, imp_t = _cmp_attention(qp, kcp, lhs, nb_c, gates_t)
    selneg_t = _select_blocks(imp_t)
    o_s = _sel_attention(qp, selneg_t, ksa, vst, nb_s, gates_t)
    o_w = _win_attention(qp, kwp, vwt, nb_w, gates_t)
    return o_c, o_s, o_w


def kernel(x, c, rel_bias, ada_w, ada_b, ln_g, ln_b, ffn_w1, ffn_w3, ffn_w2, w_in, w_out, cmp_pe, cmp_w1, cmp_b1,
           cmp_w2, gmlp_ln_g, gmlp_ln_b, gmlp_ws, gmlp_bs, ssm_conv_w, ssm_conv_b, ssm_dt_bias, ssm_a_log, ssm_d,
           ssm_norm_g):
    t = x.shape[1]
    mod = _adaln_mod(c, ada_w, ada_b)
    tables = _bias_tables(rel_bias)
    ovt = _overlap_matrix(t)
    for l in range(ada_w.shape[0]):
        x = _ffn_sublayer(x, mod[l], ffn_w1[l, 0], ffn_w3[l, 0], ffn_w2[l, 0], ln_g[l, 0], ln_b[l, 0], 0)
        qp, kcv, ksa, vst, kwp, vwt, gates_t, gm_in, z, xbc, dt = _mixer_inproj(x, mod[l], _relayout_w_in(w_in[l]))
        o_c, o_s, o_w = _nsa(qp, kcv, ksa, vst, kwp, vwt, gates_t, tables, ovt, cmp_pe[l], cmp_w1[l], cmp_b1[l],
                             cmp_w2[l])
        o_gmlp = _gmlp(gm_in, gmlp_ln_g[l], gmlp_ln_b[l], gmlp_ws[l], gmlp_bs[l])
        o_ssm = _ssd(z, xbc, dt, ssm_conv_w[l], ssm_conv_b[l], ssm_dt_bias[l], ssm_a_log[l], ssm_d[l],
                     ssm_norm_g[l])
        x = _mixer_outproj(x, mod[l], o_c, o_s, o_w, o_gmlp, o_ssm, w_out[l], ln_g[l, 1], ln_b[l, 1])
        x = _ffn_sublayer(x, mod[l], ffn_w1[l, 1], ffn_w3[l, 1], ffn_w2[l, 1], ln_g[l, 2], ln_b[l, 2], 2)
    return x
```

```python
import functools
import math

import numpy as np
import jax
import jax.numpy as jnp
from jax import lax
from jax.experimental import pallas as pl
from jax.experimental.pallas import tpu as pltpu

F32 = jnp.float32
BF16 = jnp.bfloat16
HI = lax.Precision.HIGHEST

D_MODEL = 1024
DEPTH = 2
ATT_HEADS = 8
ATT_KV_GROUPS = 2
ATT_HPG = 4
HEAD_DIM = 64
CMP_LEN = 32
CMP_STRIDE = 16
CMP_HIDDEN = 256
SLC_BLOCK = 64
SLC_TOPN = 16
WINDOW = 512
SEL_FORCE = 1.0e4
REL_BUCKETS = 32
REL_MAX_DIST = 128
GMLP_GROUPS = 4
GMLP_CHUNK = 128
D_GMLP = 256
SSM_HEADS = 4
SSM_HEAD_DIM = 64
D_SSM = 256
SSM_STATE = 128
SSM_CONV = 4
SSM_CHUNK = 256
D_XBC = 768
D_FF = 2816
N_SUB = 3
ALPHA = (2 * DEPTH) ** 0.25
LN_EPS = 1e-5

LANES = 128
LOG2E = math.log2(math.e)
NEG_BIG = -(2.0 ** 100)
KPAD = 512
CMP_KPAD = 16
CMP_CHUNK_LOG2 = 7
CMP_NEAR = 256
CMP_FLAG0 = HEAD_DIM + 1
CMP_SUBTILES = 2
CMP_KEY_SIZES = (384, 512, 640, 768, 896, 1024)
TQ_C = 256
TQ = 256
TK = 256
SEL_FAR_TILES = 2
SEL_SUBTILES = 2
SEL_AHEAD = 6
assert SEL_SUBTILES == 2 and SEL_FAR_TILES == 2
WIN_SUBTILES = 2
GATE_ROWS_PER_GROUP = 3 * ATT_HPG
GATE_ROWS = 16
TM = 512
FF_CHUNK = 256
SEL_CHUNK_BLOCKS = 128

COL_Q = 0
COL_KV = 1024
COL_GATE = 2560
COL_GMLP = 2688
COL_Z = 3200
COL_XBC = 3456
COL_DT = 4224
N_COLS = 4352


def _cparams(n_axes, vmem_mb):
    return pltpu.CompilerParams(dimension_semantics=("arbitrary",) * n_axes,
                                vmem_limit_bytes=vmem_mb * 1024 * 1024)


def _sigmoid(x):
    return 1.0 / (1.0 + jnp.exp(-x))


def _gelu_tanh(x):
    return x * (0.5 * (1.0 + jnp.tanh(math.sqrt(2.0 / math.pi) * (x + 0.044715 * (x * x * x)))))


def _layer_norm(z, g, b):
    mu = jnp.mean(z, axis=-1, keepdims=True)
    d = z - mu
    var = jnp.mean(d * d, axis=-1, keepdims=True)
    return d * lax.rsqrt(var + LN_EPS) * g + b


def _dot(a, b):
    return jnp.dot(a, b, preferred_element_type=F32)


def _dot_hi(a, b):
    return jnp.dot(a, b, preferred_element_type=F32, precision=HI)


def _dot_nt(a, b):
    return lax.dot_general(a, b, (((1,), (1,)), ((), ())), preferred_element_type=F32)


def _mod_body(c_ref, w_ref, b_ref, o_ref):
    c = c_ref[...]
    o_ref[0] = _dot_hi(c * _sigmoid(c), w_ref[0]) + b_ref[0]


def _adaln_mod(c, ada_w, ada_b):
    n_layers, d, n = ada_w.shape
    bn = c.shape[0]
    tn = 1536
    cp = jnp.zeros((8, d), F32).at[:bn].set(c)
    out = pl.pallas_call(
        _mod_body,
        grid=(n_layers, n // tn),
        in_specs=[pl.BlockSpec((8, d), lambda l, j: (0, 0)),
                  pl.BlockSpec((1, d, tn), lambda l, j: (l, 0, j)),
                  pl.BlockSpec((1, 1, tn), lambda l, j: (l, 0, j))],
        out_specs=pl.BlockSpec((1, 8, tn), lambda l, j: (l, 0, j)),
        out_shape=jax.ShapeDtypeStruct((n_layers, 8, n), F32),
        compiler_params=_cparams(2, 32),
        name="adaln_mod",
    )(cp, ada_w, ada_b.reshape(n_layers, 1, n))
    return out[:, :bn].reshape(n_layers, bn, N_SUB * 3, d)


def _ffn_body(x_ref, mod_ref, w1_ref, w3_ref, w2_ref, g_ref, b_ref, o_ref, *, sub):
    x = x_ref[0]
    m = mod_ref[0]
    shift, scale, gate = m[3 * sub:3 * sub + 1], m[3 * sub + 1:3 * sub + 2], m[3 * sub + 2:3 * sub + 3]
    h = (x * (1.0 + scale) + shift).astype(BF16)
    acc = jnp.zeros(x.shape, F32)
    for j in range(D_FF // FF_CHUNK):
        sl = slice(j * FF_CHUNK, (j + 1) * FF_CHUNK)
        a = _dot(h, w1_ref[:, sl])
        b = _dot(h, w3_ref[:, sl])
        acc = acc + _dot((a * _sigmoid(a) * b).astype(BF16), w2_ref[sl, :])
    z = ALPHA * x + (0.5 * (1.0 + gate)) * acc
    o_ref[0] = _layer_norm(z, g_ref[...], b_ref[...])


def _ffn_sublayer(x, mod_l, w1, w3, w2, g, b, sub):
    bn, t, d = x.shape
    full = lambda shape: pl.BlockSpec(shape, lambda bi, i: (0,) * len(shape))
    return pl.pallas_call(
        functools.partial(_ffn_body, sub=sub),
        grid=(bn, t // TM),
        in_specs=[pl.BlockSpec((1, TM, d), lambda bi, i: (bi, i, 0)),
                  pl.BlockSpec((1, N_SUB * 3, d), lambda bi, i: (bi, 0, 0)),
                  full((d, D_FF)), full((d, D_FF)), full((D_FF, d)),
                  full((1, d)), full((1, d))],
        out_specs=pl.BlockSpec((1, TM, d), lambda bi, i: (bi, i, 0)),
        out_shape=jax.ShapeDtypeStruct(x.shape, F32),
        compiler_params=_cparams(2, 56),
        name="ffn",
    )(x, mod_l, w1.astype(BF16), w3.astype(BF16), w2.astype(BF16), g[None], b[None])


def _pad_cols(w, width):
    return jnp.pad(w, ((0, 0), (0, width - w.shape[1])))


def _relayout_w_in(w_in):
    cols = [_pad_cols(w_in[:, HEAD_DIM * h:HEAD_DIM * (h + 1)], LANES) for h in range(ATT_HEADS)]
    for i in range(6):
        for g in range(ATT_KV_GROUPS):
            base = 512 + 128 * i + HEAD_DIM * g
            cols.append(_pad_cols(w_in[:, base:base + HEAD_DIM], LANES))
    cols.append(_pad_cols(w_in[:, 1280:1304], LANES))
    cols.append(w_in[:, 1304:1816])
    cols.append(w_in[:, 1816:2072])
    cols.append(w_in[:, 2072:2840])
    cols.append(_pad_cols(w_in[:, 2840:2844], LANES))
    return jnp.concatenate(cols, axis=1).astype(BF16)


def _inproj_body(x_ref, mod_ref, w_ref, qp_ref, kcv_ref, ksa_ref, vst_ref, kwp_ref, vwt_ref,
                 gt_ref, gm_ref, z_ref, xbc_ref, dt_ref):
    i = pl.program_id(1)
    x = x_ref[0]
    m = mod_ref[0]
    h = (x * (1.0 + m[4:5]) + m[3:4]).astype(BF16)

    def proj(c0, width):
        return _dot(h, w_ref[:, c0:c0 + width])

    lane = lax.broadcasted_iota(jnp.int32, (1, LANES), 1)
    q_flag = jnp.where(lane == HEAD_DIM, NEG_BIG, 0.0)
    for hh in range(ATT_HEADS):
        qp_ref[0, hh] = (proj(COL_Q + LANES * hh, LANES) * (HEAD_DIM ** -0.5 * LOG2E) + q_flag).astype(BF16)
    for k in range(4):
        kcv_ref[0, k] = proj(COL_KV + LANES * k, LANES).astype(BF16)
    gates_t = _sigmoid(proj(COL_GATE, LANES)).T
    for g in range(ATT_KV_GROUPS):
        gt_ref[0, g] = gates_t[GATE_ROWS_PER_GROUP * g:GATE_ROWS_PER_GROUP * g + GATE_ROWS, :]
    gm_ref[0] = proj(COL_GMLP, 512)
    z_ref[0] = proj(COL_Z, D_SSM)
    xbc_ref[0] = proj(COL_XBC, D_XBC)
    dt_ref[0] = proj(COL_DT, LANES)

    @pl.when(i == 0)
    def _():
        dummy = jnp.broadcast_to(jnp.where(lane == HEAD_DIM, 1.0, 0.0), (TM, LANES)).astype(BF16)
        zeros = jnp.zeros((TM, LANES), BF16)
        for g in range(ATT_KV_GROUPS):
            ksa_ref[0, g, :, 0:LANES] = dummy
            ksa_ref[0, g, :, LANES:2 * LANES] = zeros
            kwp_ref[0, g] = dummy
            vst_ref[0, g] = jnp.zeros((LANES, TM), BF16)
            vwt_ref[0, g] = jnp.zeros((LANES, TM), BF16)

    @pl.when(i > 0)
    def _():
        pos = (i - 1) * TM + lax.broadcasted_iota(jnp.int32, (TM, LANES), 0)
        blk_lane = (pos >> 6) & (SEL_CHUNK_BLOCKS - 1)
        onehot = jnp.where(blk_lane == lax.broadcasted_iota(jnp.int32, (TM, LANES), 1), 1.0, 0.0).astype(BF16)
        ones_row = lax.broadcasted_iota(jnp.int32, (LANES, 1), 0) == HEAD_DIM
        for g in range(ATT_KV_GROUPS):
            ksa_ref[0, g, :, 0:LANES] = proj(COL_KV + LANES * (4 + g), LANES).astype(BF16)
            ksa_ref[0, g, :, LANES:2 * LANES] = onehot
            vst_ref[0, g] = jnp.where(ones_row, 1.0, proj(COL_KV + LANES * (6 + g), LANES).T).astype(BF16)
            kwp_ref[0, g] = proj(COL_KV + LANES * (8 + g), LANES).astype(BF16)
            vwt_ref[0, g] = jnp.where(ones_row, 1.0, proj(COL_KV + LANES * (10 + g), LANES).T).astype(BF16)


def _mixer_inproj(x, mod_l, w_in):
    bn, t, d = x.shape
    tp = t + KPAD
    row = lambda width: pl.BlockSpec((1, TM, width), lambda bi, i: (bi, jnp.maximum(i - 1, 0), 0))
    sds = jax.ShapeDtypeStruct
    return pl.pallas_call(
        _inproj_body,
        grid=(bn, t // TM + 1),
        in_specs=[row(d),
                  pl.BlockSpec((1, N_SUB * 3, d), lambda bi, i: (bi, 0, 0)),
                  pl.BlockSpec((d, N_COLS), lambda bi, i: (0, 0))],
        out_specs=[pl.BlockSpec((1, ATT_HEADS, TM, LANES), lambda bi, i: (bi, 0, jnp.maximum(i - 1, 0), 0)),
                   pl.BlockSpec((1, 4, TM, LANES), lambda bi, i: (bi, 0, jnp.maximum(i - 1, 0), 0)),
                   pl.BlockSpec((1, ATT_KV_GROUPS, TM, 2 * LANES), lambda bi, i: (bi, 0, i, 0)),
                   pl.BlockSpec((1, ATT_KV_GROUPS, LANES, TM), lambda bi, i: (bi, 0, 0, i)),
                   pl.BlockSpec((1, ATT_KV_GROUPS, TM, LANES), lambda bi, i: (bi, 0, i, 0)),
                   pl.BlockSpec((1, ATT_KV_GROUPS, LANES, TM), lambda bi, i: (bi, 0, 0, i)),
                   pl.BlockSpec((1, ATT_KV_GROUPS, GATE_ROWS, TM), lambda bi, i: (bi, 0, 0, jnp.maximum(i - 1, 0))),
                   row(512), row(D_SSM), row(D_XBC), row(LANES)],
        out_shape=[sds((bn, ATT_HEADS, t, LANES), BF16),
                   sds((bn, 4, t, LANES), BF16),
                   sds((bn, ATT_KV_GROUPS, tp, 2 * LANES), BF16),
                   sds((bn, ATT_KV_GROUPS, LANES, tp), BF16),
                   sds((bn, ATT_KV_GROUPS, tp, LANES), BF16),
                   sds((bn, ATT_KV_GROUPS, LANES, tp), BF16),
                   sds((bn, ATT_KV_GROUPS, GATE_ROWS, t), F32), sds((bn, t, 512), F32), sds((bn, t, D_SSM), F32),
                   sds((bn, t, D_XBC), F32), sds((bn, t, LANES), F32)],
        compiler_params=_cparams(2, 48),
        name="mixer_inproj",
    )(x, mod_l, w_in)


def _cmp_rows(t):
    return -(-(t // CMP_STRIDE + 120) // LANES) * LANES


def _compress_body(ck_ref, cv_ref, w1_ref, pe_ref, b1_ref, w2_ref, ovt_ref, k_ref, lhs_ref):
    n_rows = ck_ref.shape[2]
    rows = k_ref.shape[2]
    n_cmp = n_rows - CMP_LEN // CMP_STRIDE + 1
    half = CMP_STRIDE * HEAD_DIM
    row = lax.broadcasted_iota(jnp.int32, (rows, LANES), 0)
    lane = lax.broadcasted_iota(jnp.int32, (rows, LANES), 1)
    real = (row >= CMP_KPAD) & (row < CMP_KPAD + n_cmp)

    def mlp(c, j):
        w1 = w1_ref[j]
        h1 = _dot(c, w1[:half])
        h2 = _dot(c, w1[half:])
        cvec = _dot(pe_ref[j], w1)[0:1] + b1_ref[j]
        hid = h1 + pltpu.roll(h2, n_rows - 1, 0) + cvec
        out = _dot(_gelu_tanh(hid).astype(BF16), w2_ref[j])
        tail = rows - CMP_KPAD - n_rows
        full = jnp.concatenate([jnp.zeros((CMP_KPAD, LANES), F32), out, jnp.zeros((tail, LANES), F32)], axis=0)
        return jnp.where(real, full, 0.0)

    flags = jnp.where(lane == HEAD_DIM, jnp.where(real, 0.0, 1.0),
                      jnp.where(lane - CMP_FLAG0 == (row >> CMP_CHUNK_LOG2), 1.0, 0.0))
    k_ref[0, 0] = (mlp(ck_ref[0, 0], 0) + flags).astype(BF16)
    ones_row = lax.broadcasted_iota(jnp.int32, (LANES, 1), 0) == HEAD_DIM
    lhs_ref[0, 0, 0:LANES, :] = jnp.where(ones_row, 1.0, mlp(cv_ref[0, 0], 1).T).astype(BF16)
    lhs_ref[0, 0, LANES:, :] = ovt_ref[...].astype(BF16)


def _compress(kcv, cmp_pe, cmp_w1, cmp_b1, cmp_w2, ovt):
    bn, _, t, _ = kcv.shape
    n_rows = t // CMP_STRIDE
    width = CMP_STRIDE * HEAD_DIM
    c = kcv[..., :HEAD_DIM].reshape(bn, 4, n_rows, width)
    pe = jnp.zeros((2, 8, CMP_LEN * HEAD_DIM), F32).at[:, 0].set(cmp_pe.reshape(2, -1)).astype(BF16)
    w2 = jnp.pad(cmp_w2, ((0, 0), (0, 0), (0, LANES - HEAD_DIM))).astype(BF16)
    n_slc, rows = ovt.shape
    full = lambda shape: pl.BlockSpec(shape, lambda bi, g: (0,) * len(shape))
    return pl.pallas_call(
        _compress_body,
        grid=(bn, ATT_KV_GROUPS),
        in_specs=[pl.BlockSpec((1, 1, n_rows, width), lambda bi, g: (bi, g, 0, 0)),
                  pl.BlockSpec((1, 1, n_rows, width), lambda bi, g: (bi, ATT_KV_GROUPS + g, 0, 0)),
                  full((2, CMP_LEN * HEAD_DIM, CMP_HIDDEN)), full((2, 8, CMP_LEN * HEAD_DIM)),
                  full((2, 1, CMP_HIDDEN)), full((2, CMP_HIDDEN, LANES)), full((n_slc, rows))],
        out_specs=[pl.BlockSpec((1, 1, rows, LANES), lambda bi, g: (bi, g, 0, 0)),
                   pl.BlockSpec((1, 1, LANES + n_slc, rows), lambda bi, g: (bi, g, 0, 0))],
        out_shape=[jax.ShapeDtypeStruct((bn, ATT_KV_GROUPS, rows, LANES), BF16),
                   jax.ShapeDtypeStruct((bn, ATT_KV_GROUPS, LANES + n_slc, rows), BF16)],
        compiler_params=_cparams(2, 40),
        name="nsa_compress",
    )(c, c, cmp_w1.astype(BF16), pe, cmp_b1[:, None, :], w2, ovt)


def _rel_bucket(dist):
    n = jnp.maximum(dist, 0)
    max_exact = REL_BUCKETS // 2
    nf = jnp.maximum(n, 1).astype(F32)
    large = max_exact + (jnp.log(nf / max_exact) / math.log(REL_MAX_DIST / max_exact)
                         * (REL_BUCKETS - max_exact)).astype(jnp.int32)
    large = jnp.minimum(large, REL_BUCKETS - 1)
    return jnp.where(n < max_exact, n, large)


def _bias_tables(rel_bias):
    far = rel_bias[REL_BUCKETS - 1]

    def table(dist, valid, shift):
        onehot = (_rel_bucket(dist)[..., None] == jnp.arange(REL_BUCKETS)).astype(F32)
        b = jnp.einsum("qkb,bh->hqk", onehot, rel_bias, precision=HI)
        if shift:
            b = b - far[:, None, None]
        return jnp.where(valid[None], b * LOG2E, -jnp.inf)

    n_var = (1 << CMP_CHUNK_LOG2) * CMP_STRIDE // TQ_C
    d_c = jnp.asarray(TQ_C * np.arange(n_var)[:, None, None] + np.arange(TQ_C)[None, None, :]
                      - CMP_STRIDE * np.arange(CMP_NEAR)[None, :, None]
                      + (CMP_STRIDE * CMP_KPAD - CMP_LEN + 1))
    nb_c = jnp.stack([table(d_c[v], d_c[v] >= 0, True) for v in range(n_var)])
    nb_c = jnp.maximum(nb_c, NEG_BIG)
    ki = np.arange(2 * TK)[:, None]
    d_s = jnp.asarray(np.arange(TQ)[None, :] - ki + TK)
    nb_s = table(d_s, d_s >= 0, True)
    d_w = jnp.asarray(np.arange(TQ)[None, :] - np.arange(TQ + WINDOW)[:, None] + WINDOW)
    nb_w = table(d_w, (d_w >= 0) & (d_w < WINDOW), False)
    return nb_c, nb_s, nb_w


def _overlap_matrix(t):
    n_cmp = (t - CMP_LEN) // CMP_STRIDE + 1
    n_slc = t // SLC_BLOCK
    cs = np.arange(n_cmp) * CMP_STRIDE
    ss = np.arange(n_slc) * SLC_BLOCK
    ov = np.clip(np.minimum(cs[:, None] + CMP_LEN, ss[None, :] + SLC_BLOCK)
                 - np.maximum(cs[:, None], ss[None, :]), 0, None) / CMP_LEN
    full = np.zeros((n_slc, _cmp_rows(t)), np.float32)
    full[:, CMP_KPAD:CMP_KPAD + n_cmp] = ov.T
    return jnp.asarray(full)


def _gate_row(gt_ref, r, branch, qs):
    row = 3 * r + branch
    return gt_ref[0, 0, row:row + 1, qs]


def _cmp_attn_body(q_ref, k_ref, lhs_ref, nba_ref, nbb_ref, gt_ref, o_ref, imp_ref, s_ref):
    step = pl.program_id(2)
    rows = k_ref.shape[2]
    chunk = 1 << CMP_CHUNK_LOG2
    nb_refs = (nba_ref, nbb_ref)
    rowi = lax.broadcasted_iota(jnp.int32, (LANES, TQ_C), 0)

    def near_chunk_of(sub):
        return ((CMP_SUBTILES * step + sub) * (TQ_C // CMP_STRIDE)) >> CMP_CHUNK_LOG2

    def body(n_keys):
        kf = k_ref[0, 0, 0:n_keys, :]
        lhs = lhs_ref[0, 0, :, 0:n_keys]
        for sub in range(CMP_SUBTILES):
            near_chunk = near_chunk_of(sub)
            chunk_flag = jnp.where((rowi >= CMP_FLAG0) & (rowi - CMP_FLAG0 > near_chunk + 1), NEG_BIG, 0.0)
            qs = slice(sub * TQ_C, (sub + 1) * TQ_C)
            q_all = jnp.concatenate([(q_ref[0, r, qs, :].astype(F32).T + chunk_flag).astype(BF16)
                                     for r in range(ATT_HPG)], axis=1)
            s_ref[sub, 0:n_keys, :] = _dot(kf, q_all)
        for sub in range(CMP_SUBTILES):
            qs = slice(sub * TQ_C, (sub + 1) * TQ_C)
            a0 = pl.multiple_of(near_chunk_of(sub) << CMP_CHUNK_LOG2, chunk)
            ps = []
            for r in range(ATT_HPG):
                hs = slice(r * TQ_C, (r + 1) * TQ_C)
                s_ref[sub, pl.ds(a0, CMP_NEAR), hs] += nb_refs[sub][0, r]
                s = s_ref[sub, 0:n_keys, hs]
                mx = jnp.max(s, axis=0, keepdims=True)
                mx = jnp.where(mx < 0.5 * NEG_BIG, 0.0, mx)
                ps.append(jnp.exp2(s - mx).astype(BF16))
            res_all = _dot(lhs, jnp.concatenate(ps, axis=1))
            imp = jnp.zeros((imp_ref.shape[2], TQ_C), F32)
            outs = []
            for r in range(ATT_HPG):
                res = res_all[:, r * TQ_C:(r + 1) * TQ_C]
                inv = 1.0 / jnp.maximum(res[HEAD_DIM:HEAD_DIM + 1, :], 1e-30)
                outs.append((res[0:LANES, :] * inv * _gate_row(gt_ref, r, 0, qs)).T[:, :HEAD_DIM])
                imp = imp + res[LANES:, :] * inv
            o_ref[0, qs, :] = jnp.concatenate(outs, axis=1)
            imp_ref[0, 0, :, qs] = imp

    need = (near_chunk_of(CMP_SUBTILES - 1) + 2) * chunk
    sizes = sorted({min(rows, s) for s in CMP_KEY_SIZES} | {rows})
    lo = 0
    for n_keys in sizes:
        @pl.when((need > lo) & (need <= n_keys))
        def _(n_keys=n_keys):
            body(n_keys)
        lo = n_keys


def _gate_spec(tq):
    return pl.BlockSpec((1, 1, GATE_ROWS, tq), lambda bi, g, m: (bi, g, 0, m))


def _cmp_attention(qp, kcp, lhs, nb_c, gates_t):
    bn, _, t, _ = qp.shape
    rows = kcp.shape[2]
    n_slc = lhs.shape[2] - LANES
    n_var = nb_c.shape[0]
    tq = CMP_SUBTILES * TQ_C
    table = lambda sub: pl.BlockSpec((1, ATT_HPG, CMP_NEAR, TQ_C),
                                     lambda bi, g, m: ((CMP_SUBTILES * m + sub) % n_var, g, 0, 0))
    return pl.pallas_call(
        _cmp_attn_body,
        grid=(bn, ATT_KV_GROUPS, t // tq),
        in_specs=[pl.BlockSpec((1, ATT_HPG, tq, LANES), lambda bi, g, m: (bi, g, m, 0)),
                  pl.BlockSpec((1, 1, rows, LANES), lambda bi, g, m: (bi, g, 0, 0)),
                  pl.BlockSpec((1, 1, LANES + n_slc, rows), lambda bi, g, m: (bi, g, 0, 0)),
                  table(0), table(1), _gate_spec(tq)],
        out_specs=[pl.BlockSpec((1, tq, ATT_HPG * HEAD_DIM), lambda bi, g, m: (bi, m, g)),
                   pl.BlockSpec((1, 1, n_slc, tq), lambda bi, g, m: (bi, g, 0, m))],
        out_shape=[jax.ShapeDtypeStruct((bn, t, ATT_HEADS * HEAD_DIM), F32),
                   jax.ShapeDtypeStruct((bn, ATT_KV_GROUPS, n_slc, t), F32)],
        scratch_shapes=[pltpu.VMEM((CMP_SUBTILES, rows, ATT_HPG * TQ_C), F32)],
        compiler_params=_cparams(3, 48),
        name="nsa_cmp_attn",
    )(qp, kcp, lhs, nb_c, nb_c, gates_t)


TOPK_COLS = 1024
TOPK_ROW_SIZES = (64, 128, 192)


def _topk_body(imp_ref, o_ref):
    i = pl.program_id(2)
    n_slc, cols = imp_ref.shape[2:]
    t = i * cols + lax.broadcasted_iota(jnp.int32, (1, cols), 1)
    cur = t >> 6

    def body(n_rows):
        blk = lax.broadcasted_iota(jnp.int32, (n_rows, cols), 0)
        forced = (blk == 0) | (blk == cur) | (blk == cur - 1)
        vals = jnp.where(forced, SEL_FORCE, jnp.where(blk <= cur, imp_ref[0, 0, 0:n_rows, :], -SEL_FORCE))
        blkf = blk.astype(F32)
        for _ in range(min(SLC_TOPN, n_rows)):
            mx = jnp.max(vals, axis=0, keepdims=True)
            first = jnp.min(jnp.where(vals == mx, blkf, float(n_slc)), axis=0, keepdims=True)
            vals = jnp.where(blkf == first, -jnp.inf, vals)
        o_ref[0, 0, 0:n_rows, :] = jnp.where(vals == -jnp.inf, 0.0, NEG_BIG).astype(BF16)
        if n_rows < n_slc:
            o_ref[0, 0, n_rows:, :] = jnp.full((n_slc - n_rows, cols), NEG_BIG, BF16)

    need = ((i + 1) * cols) >> 6
    sizes = sorted({min(n_slc, s) for s in TOPK_ROW_SIZES} | {n_slc})
    lo = 0
    for n_rows in sizes:
        @pl.when((need > lo) & (need <= n_rows))
        def _(n_rows=n_rows):
            body(n_rows)
        lo = n_rows


def _select_blocks(imp_t):
    bn, g, n_slc, t = imp_t.shape
    cols = min(TOPK_COLS, t)
    spec = pl.BlockSpec((1, 1, n_slc, cols), lambda bi, gi, i: (bi, gi, 0, i))
    return pl.pallas_call(
        _topk_body,
        grid=(bn, g, t // cols),
        in_specs=[spec],
        out_specs=spec,
        out_shape=jax.ShapeDtypeStruct(imp_t.shape, BF16),
        compiler_params=_cparams(3, 40),
        name="nsa_topk",
    )(imp_t)


def _sel_attn_body(q_ref, sel_ref, k_ref, vt_ref, nb_ref, gt_ref, o_ref, qat_ref, m_ref, acc_ref, sc0_ref,
                   sc1_ref):
    sc_refs = (sc0_ref, sc1_ref)
    pair = pl.program_id(2)
    n_chunks = qat_ref.shape[0]
    m_a = SEL_SUBTILES * pair
    for sub in range(SEL_SUBTILES):
        qs = slice(sub * TQ, (sub + 1) * TQ)
        for r in range(ATT_HPG):
            q_t = q_ref[0, r, qs, :].astype(F32).T.astype(BF16)
            for c in range(n_chunks):
                qat_ref[c, ATT_HPG * sub + r, 0:LANES, :] = q_t
        for c in range(n_chunks):
            sel_t = sel_ref[0, 0, c * LANES:(c + 1) * LANES, qs]
            for r in range(ATT_HPG):
                qat_ref[c, ATT_HPG * sub + r, LANES:2 * LANES, :] = sel_t
    m_ref[...] = jnp.full(m_ref.shape, -jnp.inf, F32)
    acc_ref[...] = jnp.zeros(acc_ref.shape, F32)

    tiles_per_chunk = SEL_CHUNK_BLOCKS * SLC_BLOCK // TK
    units_a = tuple(range(ATT_HPG))
    units_b = tuple(range(ATT_HPG, 2 * ATT_HPG))

    def load(j, n_tiles):
        row0 = pl.multiple_of(KPAD + TK * j, TK)
        return (k_ref[0, 0, pl.ds(row0, n_tiles * TK), :], vt_ref[0, 0, :, pl.ds(row0, n_tiles * TK)],
                jnp.maximum(j, 0) // tiles_per_chunk)

    def qk(tile, u):
        kt, _, c = tile
        return _dot(kt, qat_ref[c, u])

    def update(tile, u, s, bias):
        if bias is not None:
            s = bias(u % ATT_HPG) + s
        m_prev = m_ref[u]
        m_new = jnp.maximum(m_prev, jnp.max(s, axis=0, keepdims=True))
        alpha = jnp.exp2(m_prev - m_new)
        p = jnp.exp2(s - m_new).astype(BF16)
        acc_ref[u] = alpha * acc_ref[u] + _dot(tile[1], p)
        m_ref[u] = m_new

    def step(work):
        todo = [(tile, u, bias) for tile, units, bias in work for u in units]
        scores = [qk(tile, u) for tile, u, _ in todo[:SEL_AHEAD]]
        for i, (tile, u, bias) in enumerate(todo):
            if i + SEL_AHEAD < len(todo):
                scores.append(qk(todo[i + SEL_AHEAD][0], todo[i + SEL_AHEAD][1]))
            s = scores[i]
            scores[i] = None
            update(tile, u, s, bias)

    def pipelined_step(tile, slot, next_tile):
        for i, u in enumerate(all_units):
            if next_tile is not None:
                for k in range(2 * i, min(2 * i + 2, len(all_units))):
                    sc_refs[1 - slot][k] = qk(next_tile, all_units[k])
            update(tile, u, sc_refs[slot][i], None)

    all_units = units_a + units_b
    one_trip = jnp.minimum(pair + 1, 1)

    def near(i, carry):
        rows = pl.ds(pl.multiple_of(TK * (1 - i), TK), TK)
        bias = lambda r: nb_ref[r, rows, :]
        step([(load(m_a - i, 1), units_a, bias), (load(m_a + 1 - i, 1), units_b, bias)])
        return carry

    lax.fori_loop(0, 2, near, 0)

    def below(i, carry):
        step([(load(m_a - 2, 1), units_a, None), (load(m_a - 2, 2), units_b, None)])
        return carry

    lax.fori_loop(0, one_trip, below, 0)

    n_big = jnp.maximum(m_a - 2, 0) // SEL_FAR_TILES
    far_tile = lambda i: load(SEL_FAR_TILES * i, SEL_FAR_TILES)

    @pl.when(n_big > 0)
    def _():
        first = far_tile(0)
        for k, u in enumerate(all_units):
            sc_refs[0][k] = qk(first, u)
        n_pairs = (n_big - 1) // 2

        def far_pair(i, carry):
            pipelined_step(far_tile(2 * i), 0, far_tile(2 * i + 1))
            pipelined_step(far_tile(2 * i + 1), 1, far_tile(2 * i + 2))
            return carry

        lax.fori_loop(0, n_pairs, far_pair, 0)

        @pl.when(n_big - 2 * n_pairs == 1)
        def _():
            pipelined_step(far_tile(n_big - 1), 0, None)

        @pl.when(n_big - 2 * n_pairs == 2)
        def _():
            pipelined_step(far_tile(n_big - 2), 0, far_tile(n_big - 1))
            pipelined_step(far_tile(n_big - 1), 1, None)

    for sub in range(SEL_SUBTILES):
        outs = []
        for r in range(ATT_HPG):
            acc = acc_ref[ATT_HPG * sub + r]
            gate = _gate_row(gt_ref, r, 1, slice(sub * TQ, (sub + 1) * TQ))
            outs.append((acc / acc[HEAD_DIM:HEAD_DIM + 1, :] * gate).T[:, :HEAD_DIM])
        o_ref[0, sub * TQ:(sub + 1) * TQ, :] = jnp.concatenate(outs, axis=1)


def _sel_attention(qp, selneg, ksa, vst, nb_s, gates_t):
    bn, _, t, _ = qp.shape
    tp = ksa.shape[2]
    n_slc = selneg.shape[2]
    n_chunks = max(n_slc // SEL_CHUNK_BLOCKS, 1)
    tq = SEL_SUBTILES * TQ
    n_units = SEL_SUBTILES * ATT_HPG
    return pl.pallas_call(
        _sel_attn_body,
        grid=(bn, ATT_KV_GROUPS, t // tq),
        in_specs=[pl.BlockSpec((1, ATT_HPG, tq, LANES), lambda bi, g, m: (bi, g, m, 0)),
                  pl.BlockSpec((1, 1, n_slc, tq), lambda bi, g, m: (bi, g, 0, m)),
                  pl.BlockSpec((1, 1, tp, 2 * LANES), lambda bi, g, m: (bi, g, 0, 0)),
                  pl.BlockSpec((1, 1, LANES, tp), lambda bi, g, m: (bi, g, 0, 0)),
                  pl.BlockSpec((ATT_HPG, 2 * TK, TQ), lambda bi, g, m: (g, 0, 0)),
                  _gate_spec(tq)],
        out_specs=pl.BlockSpec((1, tq, ATT_HPG * HEAD_DIM), lambda bi, g, m: (bi, m, g)),
        out_shape=jax.ShapeDtypeStruct((bn, t, ATT_HEADS * HEAD_DIM), F32),
        scratch_shapes=[pltpu.VMEM((n_chunks, n_units, 2 * LANES, TQ), BF16),
                        pltpu.VMEM((n_units, 1, TQ), F32),
                        pltpu.VMEM((n_units, LANES, TQ), F32),
                        pltpu.VMEM((n_units, SEL_FAR_TILES * TK, TQ), F32),
                        pltpu.VMEM((n_units, SEL_FAR_TILES * TK, TQ), F32)],
        compiler_params=_cparams(3, 56),
        name="nsa_sel_attn",
    )(qp, selneg, ksa, vst, nb_s, gates_t)


def _win_attn_body(q_ref, k_ref, vt_ref, nb_ref, gt_ref, o_ref):
    step = pl.program_id(2)
    keys = TQ + WINDOW
    tiles = []
    for sub in range(WIN_SUBTILES):
        row0 = pl.multiple_of((WIN_SUBTILES * step + sub) * TQ, TQ)
        kt = k_ref[0, 0, pl.ds(row0, keys), :]
        vt = vt_ref[0, 0, :, pl.ds(row0, keys)]
        qs = slice(sub * TQ, (sub + 1) * TQ)
        q_all = jnp.concatenate([q_ref[0, r, qs, :].astype(F32).T.astype(BF16) for r in range(ATT_HPG)], axis=1)
        tiles.append((vt, qs, _dot(kt, q_all)))
    for vt, qs, s_all in tiles:
        ps = []
        for r in range(ATT_HPG):
            s = s_all[:, r * TQ:(r + 1) * TQ] + nb_ref[r]
            ps.append(jnp.exp2(s - jnp.max(s, axis=0, keepdims=True)).astype(BF16))
        res_all = _dot(vt, jnp.concatenate(ps, axis=1))
        outs = []
        for r in range(ATT_HPG):
            res = res_all[:, r * TQ:(r + 1) * TQ]
            gate = _gate_row(gt_ref, r, 2, qs)
            outs.append((res / res[HEAD_DIM:HEAD_DIM + 1, :] * gate).T[:, :HEAD_DIM])
        o_ref[0, qs, :] = jnp.concatenate(outs, axis=1)


def _win_attention(qp, kwp, vwt, nb_w, gates_t):
    bn, _, t, _ = qp.shape
    tp = kwp.shape[2]
    tq = WIN_SUBTILES * TQ
    return pl.pallas_call(
        _win_attn_body,
        grid=(bn, ATT_KV_GROUPS, t // tq),
        in_specs=[pl.BlockSpec((1, ATT_HPG, tq, LANES), lambda bi, g, m: (bi, g, m, 0)),
                  pl.BlockSpec((1, 1, tp, LANES), lambda bi, g, m: (bi, g, 0, 0)),
                  pl.BlockSpec((1, 1, LANES, tp), lambda bi, g, m: (bi, g, 0, 0)),
                  pl.BlockSpec((ATT_HPG, TQ + WINDOW, TQ), lambda bi, g, m: (g, 0, 0)),
                  _gate_spec(tq)],
        out_specs=pl.BlockSpec((1, tq, ATT_HPG * HEAD_DIM), lambda bi, g, m: (bi, m, g)),
        out_shape=jax.ShapeDtypeStruct((bn, t, ATT_HEADS * HEAD_DIM), F32),
        compiler_params=_cparams(3, 48),
        name="nsa_win_attn",
    )(qp, kwp, vwt, nb_w, gates_t)


GMLP_ROWS = 1024


def _gmlp_body(uv_ref, g_ref, b_ref, ws_ref, bs_ref, o_ref):
    act = _gelu_tanh(uv_ref[0])
    u = act[:, :D_GMLP]
    v = _layer_norm(act[:, D_GMLP:], g_ref[...], b_ref[...])
    ti = lax.broadcasted_iota(jnp.int32, (GMLP_CHUNK, GMLP_CHUNK), 0)
    si = lax.broadcasted_iota(jnp.int32, (GMLP_CHUNK, GMLP_CHUNK), 1)
    w = [jnp.where(ti >= si, ws_ref[g], 0.0).astype(BF16) for g in range(GMLP_GROUPS)]
    lane = lax.broadcasted_iota(jnp.int32, (1, LANES), 1)
    lo = lane < HEAD_DIM
    for c in range(uv_ref.shape[1] // GMLP_CHUNK):
        rs = slice(c * GMLP_CHUNK, (c + 1) * GMLP_CHUNK)
        pairs = []
        for pr in range(GMLP_GROUPS // 2):
            vp = v[rs, pr * LANES:(pr + 1) * LANES]
            pairs.append(_dot(w[2 * pr], jnp.where(lo, vp, 0.0).astype(BF16))
                         + _dot(w[2 * pr + 1], jnp.where(lo, 0.0, vp).astype(BF16)))
        o_ref[0, rs, :] = u[rs] * (jnp.concatenate(pairs, axis=1) + bs_ref[...])


def _gmlp(uv, ln_g, ln_b, ws, bs):
    bn, t, _ = uv.shape
    rows = min(GMLP_ROWS, t)
    bs_lanes = jnp.repeat(bs.T, D_GMLP // GMLP_GROUPS, axis=1)
    full = lambda shape: pl.BlockSpec(shape, lambda bi, i: (0,) * len(shape))
    return pl.pallas_call(
        _gmlp_body,
        grid=(bn, t // rows),
        in_specs=[pl.BlockSpec((1, rows, 2 * D_GMLP), lambda bi, i: (bi, i, 0)),
                  full((1, D_GMLP)), full((1, D_GMLP)),
                  full((GMLP_GROUPS, GMLP_CHUNK, GMLP_CHUNK)), full((GMLP_CHUNK, D_GMLP))],
        out_specs=pl.BlockSpec((1, rows, D_GMLP), lambda bi, i: (bi, i, 0)),
        out_shape=jax.ShapeDtypeStruct((bn, t, D_GMLP), F32),
        compiler_params=_cparams(2, 40),
        name="gmlp",
    )(uv, ln_g[None], ln_b[None], ws, bs_lanes)


def _ssd_body(z_ref, xbc_ref, prev_ref, dt_ref, cw_ref, cb_ref, dtb_ref, alog_ref, dsk_ref, ng_ref, e4_ref,
              o_ref, xe_ref, state_ref):
    c = pl.program_id(1)
    L = SSM_CHUNK

    @pl.when(c == 0)
    def _():
        state_ref[...] = jnp.zeros(state_ref.shape, F32)

    xe_ref[0:8] = jnp.where(c > 0, prev_ref[0], 0.0)
    xe_ref[8:8 + L] = xbc_ref[0]
    conv = cb_ref[...]
    for k in range(SSM_CONV):
        conv = conv + cw_ref[k:k + 1, :] * xe_ref[pl.ds(8 - (SSM_CONV - 1) + k, L), :]
    xc = conv * _sigmoid(conv)
    xs, bm, cm = xc[:, :D_SSM], xc[:, D_SSM:D_SSM + 2 * SSM_STATE], xc[:, D_SSM + 2 * SSM_STATE:]

    dtr = dt_ref[0] + dtb_ref[...]
    dt = jnp.maximum(dtr, 0.0) + jnp.log1p(jnp.exp(-jnp.abs(dtr)))
    a = -jnp.exp(alog_ref[...]) * dt
    ti = lax.broadcasted_iota(jnp.int32, (L, L), 0)
    si = lax.broadcasted_iota(jnp.int32, (L, L), 1)
    causal = ti >= si
    cs = _dot_hi(jnp.where(causal, 1.0, 0.0), a)
    cs_t = cs.T
    e4 = e4_ref[...]
    dt_e = _dot_hi(dt, e4)
    cs_e = _dot_hi(cs, e4)
    last_e = cs_e[L - 1:L, :]
    x_dt = xs * dt_e
    x_dec = (x_dt * jnp.exp(last_e - cs_e)).astype(BF16)
    x_dt = x_dt.astype(BF16)
    grow = jnp.exp(cs_e)
    lane = lax.broadcasted_iota(jnp.int32, (1, LANES), 1)
    lo = lane < SSM_HEAD_DIM
    zero = jnp.zeros((), BF16)

    ys = []
    for g in range(2):
        gs = slice(g * LANES, (g + 1) * LANES)
        cg = cm[:, gs].astype(BF16)
        bg = bm[:, gs]
        scores = _dot_nt(cg, bg.astype(BF16))
        xp = x_dt[:, gs]
        y = None
        for hh in range(2):
            h = 2 * g + hh
            seg = jnp.broadcast_to(cs[:, h:h + 1], (L, L)) - jnp.broadcast_to(cs_t[h:h + 1, :], (L, L))
            w = (scores * jnp.where(causal, jnp.exp(seg), 0.0)).astype(BF16)
            xh = jnp.where(lo, xp, zero) if hh == 0 else jnp.where(lo, zero, xp)
            yh = _dot(w, xh)
            y = yh if y is None else y + yh
        state = state_ref[g]
        y = y + _dot(cg, state.astype(BF16)) * grow[:, gs]
        state_ref[g] = jnp.exp(last_e[:, gs]) * state + _dot(bg.T.astype(BF16), x_dec[:, gs])
        ys.append(y)
    y = jnp.concatenate(ys, axis=1) + xs * dsk_ref[...]
    zz = z_ref[0]
    gg = y * (zz * _sigmoid(zz))
    outs = []
    for g in range(2):
        gp = gg[:, g * LANES:(g + 1) * LANES]
        outs.append(gp * lax.rsqrt(jnp.mean(gp * gp, axis=-1, keepdims=True) + LN_EPS))
    o_ref[0] = jnp.concatenate(outs, axis=1) * ng_ref[...]


def _head_lanes(v):
    return jnp.repeat(v, SSM_HEAD_DIM)[None]


def _ssd(z, xbc, dt, conv_w, conv_b, dt_bias, a_log, d_skip, norm_g):
    bn, t, _ = z.shape
    L = SSM_CHUNK
    first_lanes = lambda v: jnp.zeros((1, LANES), F32).at[0, :SSM_HEADS].set(v)
    e4 = np.zeros((LANES, D_SSM), np.float32)
    for h in range(SSM_HEADS):
        e4[h, h * SSM_HEAD_DIM:(h + 1) * SSM_HEAD_DIM] = 1.0
    full = lambda shape: pl.BlockSpec(shape, lambda bi, c: (0,) * len(shape))
    return pl.pallas_call(
        _ssd_body,
        grid=(bn, t // L),
        in_specs=[pl.BlockSpec((1, L, D_SSM), lambda bi, c: (bi, c, 0)),
                  pl.BlockSpec((1, L, D_XBC), lambda bi, c: (bi, c, 0)),
                  pl.BlockSpec((1, 8, D_XBC), lambda bi, c: (bi, jnp.maximum(c * (L // 8) - 1, 0), 0)),
                  pl.BlockSpec((1, L, LANES), lambda bi, c: (bi, c, 0)),
                  full((SSM_CONV, D_XBC)), full((1, D_XBC)), full((1, LANES)), full((1, LANES)),
                  full((1, D_SSM)), full((1, D_SSM)), full((LANES, D_SSM))],
        out_specs=pl.BlockSpec((1, L, D_SSM), lambda bi, c: (bi, c, 0)),
        out_shape=jax.ShapeDtypeStruct((bn, t, D_SSM), F32),
        scratch_shapes=[pltpu.VMEM((L + 8, D_XBC), F32),
                        pltpu.VMEM((2, SSM_STATE, LANES), F32)],
        compiler_params=_cparams(2, 40),
        name="ssd",
    )(z, xbc, xbc, dt, conv_w, conv_b[None], first_lanes(dt_bias), first_lanes(a_log),
      _head_lanes(d_skip), norm_g[None], jnp.asarray(e4))


def _outproj_body(x_ref, mod_ref, oc_ref, os_ref, ow_ref, gm_ref, ssm_ref, w_ref, g_ref, b_ref, o_ref):
    x = x_ref[0]
    m = mod_ref[0]
    gate = m[5:6]
    d_att = ATT_HEADS * HEAD_DIM
    o_att = oc_ref[0] + os_ref[0] + ow_ref[0]
    y = (_dot(o_att.astype(BF16), w_ref[0:d_att, :])
         + _dot(gm_ref[0].astype(BF16), w_ref[d_att:d_att + D_GMLP, :])
         + _dot(ssm_ref[0].astype(BF16), w_ref[d_att + D_GMLP:, :]))
    z = ALPHA * x + (1.0 + gate) * y
    o_ref[0] = _layer_norm(z, g_ref[...], b_ref[...])


def _mixer_outproj(x, mod_l, o_c, o_s, o_w, o_gmlp, o_ssm, w_out, g, b):
    bn, t, d = x.shape
    d_att = ATT_HEADS * HEAD_DIM
    row = lambda width: pl.BlockSpec((1, TM, width), lambda bi, i: (bi, i, 0))
    full = lambda shape: pl.BlockSpec(shape, lambda bi, i: (0,) * len(shape))
    return pl.pallas_call(
        _outproj_body,
        grid=(bn, t // TM),
        in_specs=[row(d), pl.BlockSpec((1, N_SUB * 3, d), lambda bi, i: (bi, 0, 0)),
                  row(d_att), row(d_att), row(d_att),
                  row(D_GMLP), row(D_SSM), full((d, d)), full((1, d)), full((1, d))],
        out_specs=row(d),
        out_shape=jax.ShapeDtypeStruct(x.shape, F32),
        compiler_params=_cparams(2, 48),
        name="mixer_outproj",
    )(x, mod_l, o_c, o_s, o_w, o_gmlp, o_ssm, w_out.astype(BF16), g[None], b[None])


def _nsa(qp, kcv, ksa, vst, kwp, vwt, gates_t, tables, ovt, cmp_pe, cmp_w1, cmp_b1, cmp_w2):
    nb_c, nb_s, nb_w = tables
    kcp, lhs = _compress(kcv, cmp_pe, cmp_w1, cmp_b1, cmp_w2, ovt)
    o_c, imp_t = _cmp_attention(qp, kcp, lhs, nb_c, gates_t)
    selneg_t = _select_blocks(imp_t)
    o_s = _sel_attention(qp, selneg_t, ksa, vst, nb_s, gates_t)
    o_w = _win_attention(qp, kwp, vwt, nb_w, gates_t)
    return o_c, o_s, o_w


def kernel(x, c, rel_bias, ada_w, ada_b, ln_g, ln_b, ffn_w1, ffn_w3, ffn_w2, w_in, w_out, cmp_pe, cmp_w1, cmp_b1,
           cmp_w2, gmlp_ln_g, gmlp_ln_b, gmlp_ws, gmlp_bs, ssm_conv_w, ssm_conv_b, ssm_dt_bias, ssm_a_log, ssm_d,
           ssm_norm_g):
    t = x.shape[1]
    mod = _adaln_mod(c, ada_w, ada_b)
    tables = _bias_tables(rel_bias)
    ovt = _overlap_matrix(t)
    for l in range(ada_w.shape[0]):
        x = _ffn_sublayer(x, mod[l], ffn_w1[l, 0], ffn_w3[l, 0], ffn_w2[l, 0], ln_g[l, 0], ln_b[l, 0], 0)
        qp, kcv, ksa, vst, kwp, vwt, gates_t, gm_in, z, xbc, dt = _mixer_inproj(x, mod[l], _relayout_w_in(w_in[l]))
        o_c, o_s, o_w = _nsa(qp, kcv, ksa, vst, kwp, vwt, gates_t, tables, ovt, cmp_pe[l], cmp_w1[l], cmp_b1[l],
                             cmp_w2[l])
        o_gmlp = _gmlp(gm_in, gmlp_ln_g[l], gmlp_ln_b[l], gmlp_ws[l], gmlp_bs[l])
        o_ssm = _ssd(z, xbc, dt, ssm_conv_w[l], ssm_conv_b[l], ssm_dt_bias[l], ssm_a_log[l], ssm_d[l],
                     ssm_norm_g[l])
        x = _mixer_outproj(x, mod[l], o_c, o_s, o_w, o_gmlp, o_ssm, w_out[l], ln_g[l, 1], ln_b[l, 1])
        x = _ffn_sublayer(x, mod[l], ffn_w1[l, 1], ffn_w3[l, 1], ffn_w2[l, 1], ln_g[l, 2], ln_b[l, 2], 2)
    return x
```

```python
import functools
import math

import numpy as np
import jax
import jax.numpy as jnp
from jax import lax
from jax.experimental import pallas as pl
from jax.experimental.pallas import tpu as pltpu

F32 = jnp.float32
BF16 = jnp.bfloat16
HI = lax.Precision.HIGHEST

D_MODEL = 1024
DEPTH = 2
ATT_HEADS = 8
ATT_KV_GROUPS = 2
ATT_HPG = 4
HEAD_DIM = 64
CMP_LEN = 32
CMP_STRIDE = 16
CMP_HIDDEN = 256
SLC_BLOCK = 64
SLC_TOPN = 16
WINDOW = 512
SEL_FORCE = 1.0e4
REL_BUCKETS = 32
REL_MAX_DIST = 128
GMLP_GROUPS = 4
GMLP_CHUNK = 128
D_GMLP = 256
SSM_HEADS = 4
SSM_HEAD_DIM = 64
D_SSM = 256
SSM_STATE = 128
SSM_CONV = 4
SSM_CHUNK = 256
D_XBC = 768
D_FF = 2816
N_SUB = 3
ALPHA = (2 * DEPTH) ** 0.25
LN_EPS = 1e-5

LANES = 128
LOG2E = math.log2(math.e)
NEG_BIG = -(2.0 ** 100)
KPAD = 512
CMP_KPAD = 16
CMP_CHUNK_LOG2 = 7
CMP_NEAR = 256
CMP_FLAG0 = HEAD_DIM + 1
CMP_SUBTILES = 2
CMP_KEY_SIZES = (384, 512, 640, 768, 896, 1024)
TQ_C = 256
TQ = 256
TK = 256
SEL_FAR_TILES = 2
SEL_SUBTILES = 2
SEL_AHEAD = 6
assert SEL_SUBTILES == 2 and SEL_FAR_TILES == 2
WIN_SUBTILES = 2
GATE_ROWS_PER_GROUP = 3 * ATT_HPG
GATE_ROWS = 16
TM = 512
FF_CHUNK = 256
SEL_CHUNK_BLOCKS = 128

COL_Q = 0
COL_KV = 512
COL_GATE = 1280
COL_GMLP = 1408
COL_Z = 1920
COL_XBC = 2176
COL_DT = 2944
N_COLS = 3072


def _cparams(n_axes, vmem_mb):
    return pltpu.CompilerParams(dimension_semantics=("arbitrary",) * n_axes,
                                vmem_limit_bytes=vmem_mb * 1024 * 1024)


def _sigmoid(x):
    return 1.0 / (1.0 + jnp.exp(-x))


def _gelu_tanh(x):
    return x * (0.5 * (1.0 + jnp.tanh(math.sqrt(2.0 / math.pi) * (x + 0.044715 * (x * x * x)))))


def _layer_norm(z, g, b):
    mu = jnp.mean(z, axis=-1, keepdims=True)
    d = z - mu
    var = jnp.mean(d * d, axis=-1, keepdims=True)
    return d * lax.rsqrt(var + LN_EPS) * g + b


def _dot(a, b):
    return jnp.dot(a, b, preferred_element_type=F32)


def _dot_hi(a, b):
    return jnp.dot(a, b, preferred_element_type=F32, precision=HI)


def _dot_nt(a, b):
    return lax.dot_general(a, b, (((1,), (1,)), ((), ())), preferred_element_type=F32)


def _mod_body(c_ref, w_ref, b_ref, o_ref):
    c = c_ref[...]
    o_ref[0] = _dot_hi(c * _sigmoid(c), w_ref[0]) + b_ref[0]


def _adaln_mod(c, ada_w, ada_b):
    n_layers, d, n = ada_w.shape
    bn = c.shape[0]
    tn = 1536
    cp = jnp.zeros((8, d), F32).at[:bn].set(c)
    out = pl.pallas_call(
        _mod_body,
        grid=(n_layers, n // tn),
        in_specs=[pl.BlockSpec((8, d), lambda l, j: (0, 0)),
                  pl.BlockSpec((1, d, tn), lambda l, j: (l, 0, j)),
                  pl.BlockSpec((1, 1, tn), lambda l, j: (l, 0, j))],
        out_specs=pl.BlockSpec((1, 8, tn), lambda l, j: (l, 0, j)),
        out_shape=jax.ShapeDtypeStruct((n_layers, 8, n), F32),
        compiler_params=_cparams(2, 32),
        name="adaln_mod",
    )(cp, ada_w, ada_b.reshape(n_layers, 1, n))
    return out[:, :bn].reshape(n_layers, bn, N_SUB * 3, d)


def _ffn_body(x_ref, mod_ref, w1_ref, w3_ref, w2_ref, g_ref, b_ref, o_ref, *, sub):
    x = x_ref[0]
    m = mod_ref[0]
    shift, scale, gate = m[3 * sub:3 * sub + 1], m[3 * sub + 1:3 * sub + 2], m[3 * sub + 2:3 * sub + 3]
    h = (x * (1.0 + scale) + shift).astype(BF16)
    acc = jnp.zeros(x.shape, F32)
    for j in range(D_FF // FF_CHUNK):
        sl = slice(j * FF_CHUNK, (j + 1) * FF_CHUNK)
        a = _dot(h, w1_ref[:, sl])
        b = _dot(h, w3_ref[:, sl])
        acc = acc + _dot((a * _sigmoid(a) * b).astype(BF16), w2_ref[sl, :])
    z = ALPHA * x + (0.5 * (1.0 + gate)) * acc
    o_ref[0] = _layer_norm(z, g_ref[...], b_ref[...])


def _ffn_sublayer(x, mod_l, w1, w3, w2, g, b, sub):
    bn, t, d = x.shape
    full = lambda shape: pl.BlockSpec(shape, lambda bi, i: (0,) * len(shape))
    return pl.pallas_call(
        functools.partial(_ffn_body, sub=sub),
        grid=(bn, t // TM),
        in_specs=[pl.BlockSpec((1, TM, d), lambda bi, i: (bi, i, 0)),
                  pl.BlockSpec((1, N_SUB * 3, d), lambda bi, i: (bi, 0, 0)),
                  full((d, D_FF)), full((d, D_FF)), full((D_FF, d)),
                  full((1, d)), full((1, d))],
        out_specs=pl.BlockSpec((1, TM, d), lambda bi, i: (bi, i, 0)),
        out_shape=jax.ShapeDtypeStruct(x.shape, F32),
        compiler_params=_cparams(2, 56),
        name="ffn",
    )(x, mod_l, w1.astype(BF16), w3.astype(BF16), w2.astype(BF16), g[None], b[None])


def _pad_cols(w, width):
    return jnp.pad(w, ((0, 0), (0, width - w.shape[1])))


def _relayout_w_in(w_in):
    cols = [w_in[:, 0:1280]]
    cols.append(_pad_cols(w_in[:, 1280:1304], LANES))
    cols.append(w_in[:, 1304:1816])
    cols.append(w_in[:, 1816:2072])
    cols.append(w_in[:, 2072:2840])
    cols.append(_pad_cols(w_in[:, 2840:2844], LANES))
    return jnp.concatenate(cols, axis=1).astype(BF16)


def _inproj_body(x_ref, mod_ref, w_ref, qp_ref, kcv_ref, ksa_ref, vst_ref, kwp_ref, vwt_ref,
                 gt_ref, gm_ref, z_ref, xbc_ref, dt_ref):
    i = pl.program_id(1)
    x = x_ref[0]
    m = mod_ref[0]
    h = (x * (1.0 + m[4:5]) + m[3:4]).astype(BF16)

    def proj(c0, width):
        return _dot(h, w_ref[:, c0:c0 + width])

    lane = lax.broadcasted_iota(jnp.int32, (1, LANES), 1)
    low = lane < HEAD_DIM
    q_flag = jnp.where(lane == HEAD_DIM, NEG_BIG, 0.0)

    def halves(slab):
        return jnp.where(low, slab, 0.0), jnp.where(low, pltpu.roll(slab, HEAD_DIM, axis=1), 0.0)

    def halves_t(slab):
        slab_t = slab.T
        pad = jnp.zeros((LANES - HEAD_DIM, slab.shape[0]), F32)
        ones_row = lax.broadcasted_iota(jnp.int32, (LANES, 1), 0) == HEAD_DIM
        return [jnp.where(ones_row, 1.0, jnp.concatenate([slab_t[HEAD_DIM * g:HEAD_DIM * (g + 1)], pad], axis=0))
                for g in range(ATT_KV_GROUPS)]

    for pr in range(ATT_HEADS // 2):
        for hh, q in enumerate(halves(proj(COL_Q + LANES * pr, LANES) * (HEAD_DIM ** -0.5 * LOG2E))):
            qp_ref[0, 2 * pr + hh] = (q + q_flag).astype(BF16)
    for j in range(2):
        kcv_ref[0, j] = proj(COL_KV + LANES * j, LANES).astype(BF16)
    gates_t = _sigmoid(proj(COL_GATE, LANES)).T
    for g in range(ATT_KV_GROUPS):
        gt_ref[0, g] = gates_t[GATE_ROWS_PER_GROUP * g:GATE_ROWS_PER_GROUP * g + GATE_ROWS, :]
    gm_ref[0] = proj(COL_GMLP, 512)
    z_ref[0] = proj(COL_Z, D_SSM)
    xbc_ref[0] = proj(COL_XBC, D_XBC)
    dt_ref[0] = proj(COL_DT, LANES)

    @pl.when(i == 0)
    def _():
        dummy = jnp.broadcast_to(jnp.where(lane == HEAD_DIM, 1.0, 0.0), (TM, LANES)).astype(BF16)
        zeros = jnp.zeros((TM, LANES), BF16)
        for g in range(ATT_KV_GROUPS):
            ksa_ref[0, g, :, 0:LANES] = dummy
            ksa_ref[0, g, :, LANES:2 * LANES] = zeros
            kwp_ref[0, g] = dummy
            vst_ref[0, g] = jnp.zeros((LANES, TM), BF16)
            vwt_ref[0, g] = jnp.zeros((LANES, TM), BF16)

    @pl.when(i > 0)
    def _():
        pos = (i - 1) * TM + lax.broadcasted_iota(jnp.int32, (TM, LANES), 0)
        blk_lane = (pos >> 6) & (SEL_CHUNK_BLOCKS - 1)
        onehot = jnp.where(blk_lane == lax.broadcasted_iota(jnp.int32, (TM, LANES), 1), 1.0, 0.0).astype(BF16)
        ks = halves(proj(COL_KV + LANES * 2, LANES))
        kw = halves(proj(COL_KV + LANES * 4, LANES))
        vs_t = halves_t(proj(COL_KV + LANES * 3, LANES))
        vw_t = halves_t(proj(COL_KV + LANES * 5, LANES))
        for g in range(ATT_KV_GROUPS):
            ksa_ref[0, g, :, 0:LANES] = ks[g].astype(BF16)
            ksa_ref[0, g, :, LANES:2 * LANES] = onehot
            vst_ref[0, g] = vs_t[g].astype(BF16)
            kwp_ref[0, g] = kw[g].astype(BF16)
            vwt_ref[0, g] = vw_t[g].astype(BF16)


def _mixer_inproj(x, mod_l, w_in):
    bn, t, d = x.shape
    tp = t + KPAD
    row = lambda width: pl.BlockSpec((1, TM, width), lambda bi, i: (bi, jnp.maximum(i - 1, 0), 0))
    sds = jax.ShapeDtypeStruct
    return pl.pallas_call(
        _inproj_body,
        grid=(bn, t // TM + 1),
        in_specs=[row(d),
                  pl.BlockSpec((1, N_SUB * 3, d), lambda bi, i: (bi, 0, 0)),
                  pl.BlockSpec((d, N_COLS), lambda bi, i: (0, 0))],
        out_specs=[pl.BlockSpec((1, ATT_HEADS, TM, LANES), lambda bi, i: (bi, 0, jnp.maximum(i - 1, 0), 0)),
                   pl.BlockSpec((1, 2, TM, LANES), lambda bi, i: (bi, 0, jnp.maximum(i - 1, 0), 0)),
                   pl.BlockSpec((1, ATT_KV_GROUPS, TM, 2 * LANES), lambda bi, i: (bi, 0, i, 0)),
                   pl.BlockSpec((1, ATT_KV_GROUPS, LANES, TM), lambda bi, i: (bi, 0, 0, i)),
                   pl.BlockSpec((1, ATT_KV_GROUPS, TM, LANES), lambda bi, i: (bi, 0, i, 0)),
                   pl.BlockSpec((1, ATT_KV_GROUPS, LANES, TM), lambda bi, i: (bi, 0, 0, i)),
                   pl.BlockSpec((1, ATT_KV_GROUPS, GATE_ROWS, TM), lambda bi, i: (bi, 0, 0, jnp.maximum(i - 1, 0))),
                   row(512), row(D_SSM), row(D_XBC), row(LANES)],
        out_shape=[sds((bn, ATT_HEADS, t, LANES), BF16),
                   sds((bn, 2, t, LANES), BF16),
                   sds((bn, ATT_KV_GROUPS, tp, 2 * LANES), BF16),
                   sds((bn, ATT_KV_GROUPS, LANES, tp), BF16),
                   sds((bn, ATT_KV_GROUPS, tp, LANES), BF16),
                   sds((bn, ATT_KV_GROUPS, LANES, tp), BF16),
                   sds((bn, ATT_KV_GROUPS, GATE_ROWS, t), F32), sds((bn, t, 512), F32), sds((bn, t, D_SSM), F32),
                   sds((bn, t, D_XBC), F32), sds((bn, t, LANES), F32)],
        compiler_params=_cparams(2, 48),
        name="mixer_inproj",
    )(x, mod_l, w_in)


def _cmp_rows(t):
    return -(-(t // CMP_STRIDE + 120) // LANES) * LANES


def _compress_body(ck_ref, cv_ref, w1_ref, pe_ref, b1_ref, w2_ref, ovt_ref, k_ref, lhs_ref):
    n_rows = ck_ref.shape[2]
    rows = k_ref.shape[2]
    n_cmp = n_rows - CMP_LEN // CMP_STRIDE + 1
    half = CMP_STRIDE * HEAD_DIM
    row = lax.broadcasted_iota(jnp.int32, (rows, LANES), 0)
    lane = lax.broadcasted_iota(jnp.int32, (rows, LANES), 1)
    real = (row >= CMP_KPAD) & (row < CMP_KPAD + n_cmp)

    def mlp(c, j):
        w1 = w1_ref[j]
        h1 = _dot(c, w1[:half])
        h2 = _dot(c, w1[half:])
        cvec = _dot(pe_ref[j], w1)[0:1] + b1_ref[j]
        hid = h1 + pltpu.roll(h2, n_rows - 1, 0) + cvec
        out = _dot(_gelu_tanh(hid).astype(BF16), w2_ref[j])
        tail = rows - CMP_KPAD - n_rows
        full = jnp.concatenate([jnp.zeros((CMP_KPAD, LANES), F32), out, jnp.zeros((tail, LANES), F32)], axis=0)
        return jnp.where(real, full, 0.0)

    flags = jnp.where(lane == HEAD_DIM, jnp.where(real, 0.0, 1.0),
                      jnp.where(lane - CMP_FLAG0 == (row >> CMP_CHUNK_LOG2), 1.0, 0.0))
    k_ref[0, 0] = (mlp(ck_ref[0, 0], 0) + flags).astype(BF16)
    ones_row = lax.broadcasted_iota(jnp.int32, (LANES, 1), 0) == HEAD_DIM
    lhs_ref[0, 0, 0:LANES, :] = jnp.where(ones_row, 1.0, mlp(cv_ref[0, 0], 1).T).astype(BF16)
    lhs_ref[0, 0, LANES:, :] = ovt_ref[...].astype(BF16)


def _compress(kcv, cmp_pe, cmp_w1, cmp_b1, cmp_w2, ovt):
    bn, _, t, _ = kcv.shape
    n_rows = t // CMP_STRIDE
    width = CMP_STRIDE * HEAD_DIM
    c = kcv.reshape(bn, 2, n_rows, CMP_STRIDE, ATT_KV_GROUPS, HEAD_DIM).transpose(0, 1, 4, 2, 3, 5)
    c = c.reshape(bn, 2 * ATT_KV_GROUPS, n_rows, width)
    pe = jnp.zeros((2, 8, CMP_LEN * HEAD_DIM), F32).at[:, 0].set(cmp_pe.reshape(2, -1)).astype(BF16)
    w2 = jnp.pad(cmp_w2, ((0, 0), (0, 0), (0, LANES - HEAD_DIM))).astype(BF16)
    n_slc, rows = ovt.shape
    full = lambda shape: pl.BlockSpec(shape, lambda bi, g: (0,) * len(shape))
    return pl.pallas_call(
        _compress_body,
        grid=(bn, ATT_KV_GROUPS),
        in_specs=[pl.BlockSpec((1, 1, n_rows, width), lambda bi, g: (bi, g, 0, 0)),
                  pl.BlockSpec((1, 1, n_rows, width), lambda bi, g: (bi, ATT_KV_GROUPS + g, 0, 0)),
                  full((2, CMP_LEN * HEAD_DIM, CMP_HIDDEN)), full((2, 8, CMP_LEN * HEAD_DIM)),
                  full((2, 1, CMP_HIDDEN)), full((2, CMP_HIDDEN, LANES)), full((n_slc, rows))],
        out_specs=[pl.BlockSpec((1, 1, rows, LANES), lambda bi, g: (bi, g, 0, 0)),
                   pl.BlockSpec((1, 1, LANES + n_slc, rows), lambda bi, g: (bi, g, 0, 0))],
        out_shape=[jax.ShapeDtypeStruct((bn, ATT_KV_GROUPS, rows, LANES), BF16),
                   jax.ShapeDtypeStruct((bn, ATT_KV_GROUPS, LANES + n_slc, rows), BF16)],
        compiler_params=_cparams(2, 40),
        name="nsa_compress",
    )(c, c, cmp_w1.astype(BF16), pe, cmp_b1[:, None, :], w2, ovt)


def _rel_bucket(dist):
    n = jnp.maximum(dist, 0)
    max_exact = REL_BUCKETS // 2
    nf = jnp.maximum(n, 1).astype(F32)
    large = max_exact + (jnp.log(nf / max_exact) / math.log(REL_MAX_DIST / max_exact)
                         * (REL_BUCKETS - max_exact)).astype(jnp.int32)
    large = jnp.minimum(large, REL_BUCKETS - 1)
    return jnp.where(n < max_exact, n, large)


def _bias_tables(rel_bias):
    far = rel_bias[REL_BUCKETS - 1]

    def table(dist, valid, shift):
        onehot = (_rel_bucket(dist)[..., None] == jnp.arange(REL_BUCKETS)).astype(F32)
        b = jnp.einsum("qkb,bh->hqk", onehot, rel_bias, precision=HI)
        if shift:
            b = b - far[:, None, None]
        return jnp.where(valid[None], b * LOG2E, -jnp.inf)

    n_var = (1 << CMP_CHUNK_LOG2) * CMP_STRIDE // TQ_C
    d_c = jnp.asarray(TQ_C * np.arange(n_var)[:, None, None] + np.arange(TQ_C)[None, None, :]
                      - CMP_STRIDE * np.arange(CMP_NEAR)[None, :, None]
                      + (CMP_STRIDE * CMP_KPAD - CMP_LEN + 1))
    nb_c = jnp.stack([table(d_c[v], d_c[v] >= 0, True) for v in range(n_var)])
    nb_c = jnp.maximum(nb_c, NEG_BIG)
    ki = np.arange(2 * TK)[:, None]
    d_s = jnp.asarray(np.arange(TQ)[None, :] - ki + TK)
    nb_s = table(d_s, d_s >= 0, True)
    d_w = jnp.asarray(np.arange(TQ)[None, :] - np.arange(TQ + WINDOW)[:, None] + WINDOW)
    nb_w = table(d_w, (d_w >= 0) & (d_w < WINDOW), False)
    return nb_c, nb_s, nb_w


def _overlap_matrix(t):
    n_cmp = (t - CMP_LEN) // CMP_STRIDE + 1
    n_slc = t // SLC_BLOCK
    cs = np.arange(n_cmp) * CMP_STRIDE
    ss = np.arange(n_slc) * SLC_BLOCK
    ov = np.clip(np.minimum(cs[:, None] + CMP_LEN, ss[None, :] + SLC_BLOCK)
                 - np.maximum(cs[:, None], ss[None, :]), 0, None) / CMP_LEN
    full = np.zeros((n_slc, _cmp_rows(t)), np.float32)
    full[:, CMP_KPAD:CMP_KPAD + n_cmp] = ov.T
    return jnp.asarray(full)


def _gate_row(gt_ref, r, branch, qs):
    row = 3 * r + branch
    return gt_ref[0, 0, row:row + 1, qs]


def _cmp_attn_body(q_ref, k_ref, lhs_ref, nba_ref, nbb_ref, gt_ref, o_ref, imp_ref, s_ref):
    step = pl.program_id(2)
    rows = k_ref.shape[2]
    chunk = 1 << CMP_CHUNK_LOG2
    nb_refs = (nba_ref, nbb_ref)
    rowi = lax.broadcasted_iota(jnp.int32, (LANES, TQ_C), 0)

    def near_chunk_of(sub):
        return ((CMP_SUBTILES * step + sub) * (TQ_C // CMP_STRIDE)) >> CMP_CHUNK_LOG2

    def body(n_keys):
        kf = k_ref[0, 0, 0:n_keys, :]
        lhs = lhs_ref[0, 0, :, 0:n_keys]
        for sub in range(CMP_SUBTILES):
            near_chunk = near_chunk_of(sub)
            chunk_flag = jnp.where((rowi >= CMP_FLAG0) & (rowi - CMP_FLAG0 > near_chunk + 1), NEG_BIG, 0.0)
            qs = slice(sub * TQ_C, (sub + 1) * TQ_C)
            q_all = jnp.concatenate([(q_ref[0, r, qs, :].astype(F32).T + chunk_flag).astype(BF16)
                                     for r in range(ATT_HPG)], axis=1)
            s_ref[sub, 0:n_keys, :] = _dot(kf, q_all)
        for sub in range(CMP_SUBTILES):
            qs = slice(sub * TQ_C, (sub + 1) * TQ_C)
            a0 = pl.multiple_of(near_chunk_of(sub) << CMP_CHUNK_LOG2, chunk)
            ps = []
            for r in range(ATT_HPG):
                hs = slice(r * TQ_C, (r + 1) * TQ_C)
                s_ref[sub, pl.ds(a0, CMP_NEAR), hs] += nb_refs[sub][0, r]
                s = s_ref[sub, 0:n_keys, hs]
                mx = jnp.max(s, axis=0, keepdims=True)
                mx = jnp.where(mx < 0.5 * NEG_BIG, 0.0, mx)
                ps.append(jnp.exp2(s - mx).astype(BF16))
            res_all = _dot(lhs, jnp.concatenate(ps, axis=1))
            imp = jnp.zeros((imp_ref.shape[2], TQ_C), F32)
            outs = []
            for r in range(ATT_HPG):
                res = res_all[:, r * TQ_C:(r + 1) * TQ_C]
                inv = 1.0 / jnp.maximum(res[HEAD_DIM:HEAD_DIM + 1, :], 1e-30)
                outs.append((res[0:LANES, :] * inv * _gate_row(gt_ref, r, 0, qs)).T[:, :HEAD_DIM])
                imp = imp + res[LANES:, :] * inv
            o_ref[0, qs, :] = jnp.concatenate(outs, axis=1)
            imp_ref[0, 0, :, qs] = imp

    need = (near_chunk_of(CMP_SUBTILES - 1) + 2) * chunk
    sizes = sorted({min(rows, s) for s in CMP_KEY_SIZES} | {rows})
    lo = 0
    for n_keys in sizes:
        @pl.when((need > lo) & (need <= n_keys))
        def _(n_keys=n_keys):
            body(n_keys)
        lo = n_keys


def _gate_spec(tq):
    return pl.BlockSpec((1, 1, GATE_ROWS, tq), lambda bi, g, m: (bi, g, 0, m))


def _cmp_attention(qp, kcp, lhs, nb_c, gates_t):
    bn, _, t, _ = qp.shape
    rows = kcp.shape[2]
    n_slc = lhs.shape[2] - LANES
    n_var = nb_c.shape[0]
    tq = CMP_SUBTILES * TQ_C
    table = lambda sub: pl.BlockSpec((1, ATT_HPG, CMP_NEAR, TQ_C),
                                     lambda bi, g, m: ((CMP_SUBTILES * m + sub) % n_var, g, 0, 0))
    return pl.pallas_call(
        _cmp_attn_body,
        grid=(bn, ATT_KV_GROUPS, t // tq),
        in_specs=[pl.BlockSpec((1, ATT_HPG, tq, LANES), lambda bi, g, m: (bi, g, m, 0)),
                  pl.BlockSpec((1, 1, rows, LANES), lambda bi, g, m: (bi, g, 0, 0)),
                  pl.BlockSpec((1, 1, LANES + n_slc, rows), lambda bi, g, m: (bi, g, 0, 0)),
                  table(0), table(1), _gate_spec(tq)],
        out_specs=[pl.BlockSpec((1, tq, ATT_HPG * HEAD_DIM), lambda bi, g, m: (bi, m, g)),
                   pl.BlockSpec((1, 1, n_slc, tq), lambda bi, g, m: (bi, g, 0, m))],
        out_shape=[jax.ShapeDtypeStruct((bn, t, ATT_HEADS * HEAD_DIM), F32),
                   jax.ShapeDtypeStruct((bn, ATT_KV_GROUPS, n_slc, t), F32)],
        scratch_shapes=[pltpu.VMEM((CMP_SUBTILES, rows, ATT_HPG * TQ_C), F32)],
        compiler_params=_cparams(3, 48),
        name="nsa_cmp_attn",
    )(qp, kcp, lhs, nb_c, nb_c, gates_t)


TOPK_COLS = 1024
TOPK_ROW_SIZES = (64, 128, 192)


def _topk_body(imp_ref, o_ref):
    i = pl.program_id(2)
    n_slc, cols = imp_ref.shape[2:]
    t = i * cols + lax.broadcasted_iota(jnp.int32, (1, cols), 1)
    cur = t >> 6

    def body(n_rows):
        blk = lax.broadcasted_iota(jnp.int32, (n_rows, cols), 0)
        forced = (blk == 0) | (blk == cur) | (blk == cur - 1)
        vals = jnp.where(forced, SEL_FORCE, jnp.where(blk <= cur, imp_ref[0, 0, 0:n_rows, :], -SEL_FORCE))
        blkf = blk.astype(F32)
        for _ in range(min(SLC_TOPN, n_rows)):
            mx = jnp.max(vals, axis=0, keepdims=True)
            first = jnp.min(jnp.where(vals == mx, blkf, float(n_slc)), axis=0, keepdims=True)
            vals = jnp.where(blkf == first, -jnp.inf, vals)
        o_ref[0, 0, 0:n_rows, :] = jnp.where(vals == -jnp.inf, 0.0, NEG_BIG).astype(BF16)
        if n_rows < n_slc:
            o_ref[0, 0, n_rows:, :] = jnp.full((n_slc - n_rows, cols), NEG_BIG, BF16)

    need = ((i + 1) * cols) >> 6
    sizes = sorted({min(n_slc, s) for s in TOPK_ROW_SIZES} | {n_slc})
    lo = 0
    for n_rows in sizes:
        @pl.when((need > lo) & (need <= n_rows))
        def _(n_rows=n_rows):
            body(n_rows)
        lo = n_rows


def _select_blocks(imp_t):
    bn, g, n_slc, t = imp_t.shape
    cols = min(TOPK_COLS, t)
    spec = pl.BlockSpec((1, 1, n_slc, cols), lambda bi, gi, i: (bi, gi, 0, i))
    return pl.pallas_call(
        _topk_body,
        grid=(bn, g, t // cols),
        in_specs=[spec],
        out_specs=spec,
        out_shape=jax.ShapeDtypeStruct(imp_t.shape, BF16),
        compiler_params=_cparams(3, 40),
        name="nsa_topk",
    )(imp_t)


def _sel_attn_body(q_ref, sel_ref, k_ref, vt_ref, nb_ref, gt_ref, o_ref, qat_ref, m_ref, acc_ref, sc0_ref,
                   sc1_ref):
    sc_refs = (sc0_ref, sc1_ref)
    pair = pl.program_id(2)
    n_chunks = qat_ref.shape[0]
    m_a = SEL_SUBTILES * pair
    for sub in range(SEL_SUBTILES):
        qs = slice(sub * TQ, (sub + 1) * TQ)
        for r in range(ATT_HPG):
            q_t = q_ref[0, r, qs, :].astype(F32).T.astype(BF16)
            for c in range(n_chunks):
                qat_ref[c, ATT_HPG * sub + r, 0:LANES, :] = q_t
        for c in range(n_chunks):
            sel_t = sel_ref[0, 0, c * LANES:(c + 1) * LANES, qs]
            for r in range(ATT_HPG):
                qat_ref[c, ATT_HPG * sub + r, LANES:2 * LANES, :] = sel_t
    m_ref[...] = jnp.full(m_ref.shape, -jnp.inf, F32)
    acc_ref[...] = jnp.zeros(acc_ref.shape, F32)

    tiles_per_chunk = SEL_CHUNK_BLOCKS * SLC_BLOCK // TK
    units_a = tuple(range(ATT_HPG))
    units_b = tuple(range(ATT_HPG, 2 * ATT_HPG))

    def load(j, n_tiles):
        row0 = pl.multiple_of(KPAD + TK * j, TK)
        return (k_ref[0, 0, pl.ds(row0, n_tiles * TK), :], vt_ref[0, 0, :, pl.ds(row0, n_tiles * TK)],
                jnp.maximum(j, 0) // tiles_per_chunk)

    def qk(tile, u):
        kt, _, c = tile
        return _dot(kt, qat_ref[c, u])

    def update(tile, u, s, bias):
        if bias is not None:
            s = bias(u % ATT_HPG) + s
        m_prev = m_ref[u]
        m_new = jnp.maximum(m_prev, jnp.max(s, axis=0, keepdims=True))
        alpha = jnp.exp2(m_prev - m_new)
        p = jnp.exp2(s - m_new).astype(BF16)
        acc_ref[u] = alpha * acc_ref[u] + _dot(tile[1], p)
        m_ref[u] = m_new

    def step(work):
        todo = [(tile, u, bias) for tile, units, bias in work for u in units]
        scores = [qk(tile, u) for tile, u, _ in todo[:SEL_AHEAD]]
        for i, (tile, u, bias) in enumerate(todo):
            if i + SEL_AHEAD < len(todo):
                scores.append(qk(todo[i + SEL_AHEAD][0], todo[i + SEL_AHEAD][1]))
            s = scores[i]
            scores[i] = None
            update(tile, u, s, bias)

    def pipelined_step(tile, slot, next_tile):
        for i, u in enumerate(all_units):
            if next_tile is not None:
                for k in range(2 * i, min(2 * i + 2, len(all_units))):
                    sc_refs[1 - slot][k] = qk(next_tile, all_units[k])
            update(tile, u, sc_refs[slot][i], None)

    all_units = units_a + units_b
    one_trip = jnp.minimum(pair + 1, 1)

    def near(i, carry):
        rows = pl.ds(pl.multiple_of(TK * (1 - i), TK), TK)
        bias = lambda r: nb_ref[r, rows, :]
        step([(load(m_a - i, 1), units_a, bias), (load(m_a + 1 - i, 1), units_b, bias)])
        return carry

    lax.fori_loop(0, 2, near, 0)

    def below(i, carry):
        step([(load(m_a - 2, 1), units_a, None), (load(m_a - 2, 2), units_b, None)])
        return carry

    lax.fori_loop(0, one_trip, below, 0)

    n_big = jnp.maximum(m_a - 2, 0) // SEL_FAR_TILES
    far_tile = lambda i: load(SEL_FAR_TILES * i, SEL_FAR_TILES)

    @pl.when(n_big > 0)
    def _():
        first = far_tile(0)
        for k, u in enumerate(all_units):
            sc_refs[0][k] = qk(first, u)
        n_pairs = (n_big - 1) // 2

        def far_pair(i, carry):
            pipelined_step(far_tile(2 * i), 0, far_tile(2 * i + 1))
            pipelined_step(far_tile(2 * i + 1), 1, far_tile(2 * i + 2))
            return carry

        lax.fori_loop(0, n_pairs, far_pair, 0)

        @pl.when(n_big - 2 * n_pairs == 1)
        def _():
            pipelined_step(far_tile(n_big - 1), 0, None)

        @pl.when(n_big - 2 * n_pairs == 2)
        def _():
            pipelined_step(far_tile(n_big - 2), 0, far_tile(n_big - 1))
            pipelined_step(far_tile(n_big - 1), 1, None)

    for sub in range(SEL_SUBTILES):
        outs = []
        for r in range(ATT_HPG):
            acc = acc_ref[ATT_HPG * sub + r]
            gate = _gate_row(gt_ref, r, 1, slice(sub * TQ, (sub + 1) * TQ))
            outs.append((acc / acc[HEAD_DIM:HEAD_DIM + 1, :] * gate).T[:, :HEAD_DIM])
        o_ref[0, sub * TQ:(sub + 1) * TQ, :] = jnp.concatenate(outs, axis=1)


def _sel_attention(qp, selneg, ksa, vst, nb_s, gates_t):
    bn, _, t, _ = qp.shape
    tp = ksa.shape[2]
    n_slc = selneg.shape[2]
    n_chunks = max(n_slc // SEL_CHUNK_BLOCKS, 1)
    tq = SEL_SUBTILES * TQ
    n_units = SEL_SUBTILES * ATT_HPG
    return pl.pallas_call(
        _sel_attn_body,
        grid=(bn, ATT_KV_GROUPS, t // tq),
        in_specs=[pl.BlockSpec((1, ATT_HPG, tq, LANES), lambda bi, g, m: (bi, g, m, 0)),
                  pl.BlockSpec((1, 1, n_slc, tq), lambda bi, g, m: (bi, g, 0, m)),
                  pl.BlockSpec((1, 1, tp, 2 * LANES), lambda bi, g, m: (bi, g, 0, 0)),
                  pl.BlockSpec((1, 1, LANES, tp), lambda bi, g, m: (bi, g, 0, 0)),
                  pl.BlockSpec((ATT_HPG, 2 * TK, TQ), lambda bi, g, m: (g, 0, 0)),
                  _gate_spec(tq)],
        out_specs=pl.BlockSpec((1, tq, ATT_HPG * HEAD_DIM), lambda bi, g, m: (bi, m, g)),
        out_shape=jax.ShapeDtypeStruct((bn, t, ATT_HEADS * HEAD_DIM), F32),
        scratch_shapes=[pltpu.VMEM((n_chunks, n_units, 2 * LANES, TQ), BF16),
                        pltpu.VMEM((n_units, 1, TQ), F32),
                        pltpu.VMEM((n_units, LANES, TQ), F32),
                        pltpu.VMEM((n_units, SEL_FAR_TILES * TK, TQ), F32),
                        pltpu.VMEM((n_units, SEL_FAR_TILES * TK, TQ), F32)],
        compiler_params=_cparams(3, 56),
        name="nsa_sel_attn",
    )(qp, selneg, ksa, vst, nb_s, gates_t)


def _win_attn_body(q_ref, k_ref, vt_ref, nb_ref, gt_ref, o_ref):
    step = pl.program_id(2)
    keys = TQ + WINDOW
    tiles = []
    for sub in range(WIN_SUBTILES):
        row0 = pl.multiple_of((WIN_SUBTILES * step + sub) * TQ, TQ)
        kt = k_ref[0, 0, pl.ds(row0, keys), :]
        vt = vt_ref[0, 0, :, pl.ds(row0, keys)]
        qs = slice(sub * TQ, (sub + 1) * TQ)
        q_all = jnp.concatenate([q_ref[0, r, qs, :].astype(F32).T.astype(BF16) for r in range(ATT_HPG)], axis=1)
        tiles.append((vt, qs, _dot(kt, q_all)))
    for vt, qs, s_all in tiles:
        ps = []
        for r in range(ATT_HPG):
            s = s_all[:, r * TQ:(r + 1) * TQ] + nb_ref[r]
            ps.append(jnp.exp2(s - jnp.max(s, axis=0, keepdims=True)).astype(BF16))
        res_all = _dot(vt, jnp.concatenate(ps, axis=1))
        outs = []
        for r in range(ATT_HPG):
            res = res_all[:, r * TQ:(r + 1) * TQ]
            gate = _gate_row(gt_ref, r, 2, qs)
            outs.append((res / res[HEAD_DIM:HEAD_DIM + 1, :] * gate).T[:, :HEAD_DIM])
        o_ref[0, qs, :] = jnp.concatenate(outs, axis=1)


def _win_attention(qp, kwp, vwt, nb_w, gates_t):
    bn, _, t, _ = qp.shape
    tp = kwp.shape[2]
    tq = WIN_SUBTILES * TQ
    return pl.pallas_call(
        _win_attn_body,
        grid=(bn, ATT_KV_GROUPS, t // tq),
        in_specs=[pl.BlockSpec((1, ATT_HPG, tq, LANES), lambda bi, g, m: (bi, g, m, 0)),
                  pl.BlockSpec((1, 1, tp, LANES), lambda bi, g, m: (bi, g, 0, 0)),
                  pl.BlockSpec((1, 1, LANES, tp), lambda bi, g, m: (bi, g, 0, 0)),
                  pl.BlockSpec((ATT_HPG, TQ + WINDOW, TQ), lambda bi, g, m: (g, 0, 0)),
                  _gate_spec(tq)],
        out_specs=pl.BlockSpec((1, tq, ATT_HPG * HEAD_DIM), lambda bi, g, m: (bi, m, g)),
        out_shape=jax.ShapeDtypeStruct((bn, t, ATT_HEADS * HEAD_DIM), F32),
        compiler_params=_cparams(3, 48),
        name="nsa_win_attn",
    )(qp, kwp, vwt, nb_w, gates_t)


GMLP_ROWS = 1024


def _gmlp_body(uv_ref, g_ref, b_ref, ws_ref, bs_ref, o_ref):
    act = _gelu_tanh(uv_ref[0])
    u = act[:, :D_GMLP]
    v = _layer_norm(act[:, D_GMLP:], g_ref[...], b_ref[...])
    ti = lax.broadcasted_iota(jnp.int32, (GMLP_CHUNK, GMLP_CHUNK), 0)
    si = lax.broadcasted_iota(jnp.int32, (GMLP_CHUNK, GMLP_CHUNK), 1)
    w = [jnp.where(ti >= si, ws_ref[g], 0.0).astype(BF16) for g in range(GMLP_GROUPS)]
    lane = lax.broadcasted_iota(jnp.int32, (1, LANES), 1)
    lo = lane < HEAD_DIM
    for c in range(uv_ref.shape[1] // GMLP_CHUNK):
        rs = slice(c * GMLP_CHUNK, (c + 1) * GMLP_CHUNK)
        pairs = []
        for pr in range(GMLP_GROUPS // 2):
            vp = v[rs, pr * LANES:(pr + 1) * LANES]
            pairs.append(_dot(w[2 * pr], jnp.where(lo, vp, 0.0).astype(BF16))
                         + _dot(w[2 * pr + 1], jnp.where(lo, 0.0, vp).astype(BF16)))
        o_ref[0, rs, :] = u[rs] * (jnp.concatenate(pairs, axis=1) + bs_ref[...])


def _gmlp(uv, ln_g, ln_b, ws, bs):
    bn, t, _ = uv.shape
    rows = min(GMLP_ROWS, t)
    bs_lanes = jnp.repeat(bs.T, D_GMLP // GMLP_GROUPS, axis=1)
    full = lambda shape: pl.BlockSpec(shape, lambda bi, i: (0,) * len(shape))
    return pl.pallas_call(
        _gmlp_body,
        grid=(bn, t // rows),
        in_specs=[pl.BlockSpec((1, rows, 2 * D_GMLP), lambda bi, i: (bi, i, 0)),
                  full((1, D_GMLP)), full((1, D_GMLP)),
                  full((GMLP_GROUPS, GMLP_CHUNK, GMLP_CHUNK)), full((GMLP_CHUNK, D_GMLP))],
        out_specs=pl.BlockSpec((1, rows, D_GMLP), lambda bi, i: (bi, i, 0)),
        out_shape=jax.ShapeDtypeStruct((bn, t, D_GMLP), F32),
        compiler_params=_cparams(2, 40),
        name="gmlp",
    )(uv, ln_g[None], ln_b[None], ws, bs_lanes)


def _ssd_body(z_ref, xbc_ref, prev_ref, dt_ref, cw_ref, cb_ref, dtb_ref, alog_ref, dsk_ref, ng_ref, e4_ref,
              o_ref, xe_ref, state_ref):
    c = pl.program_id(1)
    L = SSM_CHUNK

    @pl.when(c == 0)
    def _():
        state_ref[...] = jnp.zeros(state_ref.shape, F32)

    xe_ref[0:8] = jnp.where(c > 0, prev_ref[0], 0.0)
    xe_ref[8:8 + L] = xbc_ref[0]
    conv = cb_ref[...]
    for k in range(SSM_CONV):
        conv = conv + cw_ref[k:k + 1, :] * xe_ref[pl.ds(8 - (SSM_CONV - 1) + k, L), :]
    xc = conv * _sigmoid(conv)
    xs, bm, cm = xc[:, :D_SSM], xc[:, D_SSM:D_SSM + 2 * SSM_STATE], xc[:, D_SSM + 2 * SSM_STATE:]

    dtr = dt_ref[0] + dtb_ref[...]
    dt = jnp.maximum(dtr, 0.0) + jnp.log1p(jnp.exp(-jnp.abs(dtr)))
    a = -jnp.exp(alog_ref[...]) * dt
    ti = lax.broadcasted_iota(jnp.int32, (L, L), 0)
    si = lax.broadcasted_iota(jnp.int32, (L, L), 1)
    causal = ti >= si
    cs = _dot_hi(jnp.where(causal, 1.0, 0.0), a)
    cs_t = cs.T
    e4 = e4_ref[...]
    dt_e = _dot_hi(dt, e4)
    cs_e = _dot_hi(cs, e4)
    last_e = cs_e[L - 1:L, :]
    x_dt = xs * dt_e
    x_dec = (x_dt * jnp.exp(last_e - cs_e)).astype(BF16)
    x_dt = x_dt.astype(BF16)
    grow = jnp.exp(cs_e)
    lane = lax.broadcasted_iota(jnp.int32, (1, LANES), 1)
    lo = lane < SSM_HEAD_DIM
    zero = jnp.zeros((), BF16)

    ys = []
    for g in range(2):
        gs = slice(g * LANES, (g + 1) * LANES)
        cg = cm[:, gs].astype(BF16)
        bg = bm[:, gs]
        scores = _dot_nt(cg, bg.astype(BF16))
        xp = x_dt[:, gs]
        y = None
        for hh in range(2):
            h = 2 * g + hh
            seg = jnp.broadcast_to(cs[:, h:h + 1], (L, L)) - jnp.broadcast_to(cs_t[h:h + 1, :], (L, L))
            w = (scores * jnp.where(causal, jnp.exp(seg), 0.0)).astype(BF16)
            xh = jnp.where(lo, xp, zero) if hh == 0 else jnp.where(lo, zero, xp)
            yh = _dot(w, xh)
            y = yh if y is None else y + yh
        state = state_ref[g]
        y = y + _dot(cg, state.astype(BF16)) * grow[:, gs]
        state_ref[g] = jnp.exp(last_e[:, gs]) * state + _dot(bg.T.astype(BF16), x_dec[:, gs])
        ys.append(y)
    y = jnp.concatenate(ys, axis=1) + xs * dsk_ref[...]
    zz = z_ref[0]
    gg = y * (zz * _sigmoid(zz))
    outs = []
    for g in range(2):
        gp = gg[:, g * LANES:(g + 1) * LANES]
        outs.append(gp * lax.rsqrt(jnp.mean(gp * gp, axis=-1, keepdims=True) + LN_EPS))
    o_ref[0] = jnp.concatenate(outs, axis=1) * ng_ref[...]


def _head_lanes(v):
    return jnp.repeat(v, SSM_HEAD_DIM)[None]


def _ssd(z, xbc, dt, conv_w, conv_b, dt_bias, a_log, d_skip, norm_g):
    bn, t, _ = z.shape
    L = SSM_CHUNK
    first_lanes = lambda v: jnp.zeros((1, LANES), F32).at[0, :SSM_HEADS].set(v)
    e4 = np.zeros((LANES, D_SSM), np.float32)
    for h in range(SSM_HEADS):
        e4[h, h * SSM_HEAD_DIM:(h + 1) * SSM_HEAD_DIM] = 1.0
    full = lambda shape: pl.BlockSpec(shape, lambda bi, c: (0,) * len(shape))
    return pl.pallas_call(
        _ssd_body,
        grid=(bn, t // L),
        in_specs=[pl.BlockSpec((1, L, D_SSM), lambda bi, c: (bi, c, 0)),
                  pl.BlockSpec((1, L, D_XBC), lambda bi, c: (bi, c, 0)),
                  pl.BlockSpec((1, 8, D_XBC), lambda bi, c: (bi, jnp.maximum(c * (L // 8) - 1, 0), 0)),
                  pl.BlockSpec((1, L, LANES), lambda bi, c: (bi, c, 0)),
                  full((SSM_CONV, D_XBC)), full((1, D_XBC)), full((1, LANES)), full((1, LANES)),
                  full((1, D_SSM)), full((1, D_SSM)), full((LANES, D_SSM))],
        out_specs=pl.BlockSpec((1, L, D_SSM), lambda bi, c: (bi, c, 0)),
        out_shape=jax.ShapeDtypeStruct((bn, t, D_SSM), F32),
        scratch_shapes=[pltpu.VMEM((L + 8, D_XBC), F32),
                        pltpu.VMEM((2, SSM_STATE, LANES), F32)],
        compiler_params=_cparams(2, 40),
        name="ssd",
    )(z, xbc, xbc, dt, conv_w, conv_b[None], first_lanes(dt_bias), first_lanes(a_log),
      _head_lanes(d_skip), norm_g[None], jnp.asarray(e4))


def _outproj_body(x_ref, mod_ref, oc_ref, os_ref, ow_ref, gm_ref, ssm_ref, w_ref, g_ref, b_ref, o_ref):
    x = x_ref[0]
    m = mod_ref[0]
    gate = m[5:6]
    d_att = ATT_HEADS * HEAD_DIM
    o_att = oc_ref[0] + os_ref[0] + ow_ref[0]
    y = (_dot(o_att.astype(BF16), w_ref[0:d_att, :])
         + _dot(gm_ref[0].astype(BF16), w_ref[d_att:d_att + D_GMLP, :])
         + _dot(ssm_ref[0].astype(BF16), w_ref[d_att + D_GMLP:, :]))
    z = ALPHA * x + (1.0 + gate) * y
    o_ref[0] = _layer_norm(z, g_ref[...], b_ref[...])


def _mixer_outproj(x, mod_l, o_c, o_s, o_w, o_gmlp, o_ssm, w_out, g, b):
    bn, t, d = x.shape
    d_att = ATT_HEADS * HEAD_DIM
    row = lambda width: pl.BlockSpec((1, TM, width), lambda bi, i: (bi, i, 0))
    full = lambda shape: pl.BlockSpec(shape, lambda bi, i: (0,) * len(shape))
    return pl.pallas_call(
        _outproj_body,
        grid=(bn, t // TM),
        in_specs=[row(d), pl.BlockSpec((1, N_SUB * 3, d), lambda bi, i: (bi, 0, 0)),
                  row(d_att), row(d_att), row(d_att),
                  row(D_GMLP), row(D_SSM), full((d, d)), full((1, d)), full((1, d))],
        out_specs=row(d),
        out_shape=jax.ShapeDtypeStruct(x.shape, F32),
        compiler_params=_cparams(2, 48),
        name="mixer_outproj",
    )(x, mod_l, o_c, o_s, o_w, o_gmlp, o_ssm, w_out.astype(BF16), g[None], b[None])


def _nsa(qp, kcv, ksa, vst, kwp, vwt, gates_t, tables, ovt, cmp_pe, cmp_w1, cmp_b1, cmp_w2):
    nb_c, nb_s, nb_w = tables
    kcp, lhs = _compress(kcv, cmp_pe, cmp_w1, cmp_b1, cmp_w2, ovt)
    o_c, imp_t = _cmp_attention(qp, kcp, lhs, nb_c, gates_t)
    selneg_t = _select_blocks(imp_t)
    o_s = _sel_attention(qp, selneg_t, ksa, vst, nb_s, gates_t)
    o_w = _win_attention(qp, kwp, vwt, nb_w, gates_t)
    return o_c, o_s, o_w


def kernel(x, c, rel_bias, ada_w, ada_b, ln_g, ln_b, ffn_w1, ffn_w3, ffn_w2, w_in, w_out, cmp_pe, cmp_w1, cmp_b1,
           cmp_w2, gmlp_ln_g, gmlp_ln_b, gmlp_ws, gmlp_bs, ssm_conv_w, ssm_conv_b, ssm_dt_bias, ssm_a_log, ssm_d,
           ssm_norm_g):
    t = x.shape[1]
    mod = _adaln_mod(c, ada_w, ada_b)
    tables = _bias_tables(rel_bias)
    ovt = _overlap_matrix(t)
    for l in range(ada_w.shape[0]):
        x = _ffn_sublayer(x, mod[l], ffn_w1[l, 0], ffn_w3[l, 0], ffn_w2[l, 0], ln_g[l, 0], ln_b[l, 0], 0)
        qp, kcv, ksa, vst, kwp, vwt, gates_t, gm_in, z, xbc, dt = _mixer_inproj(x, mod[l], _relayout_w_in(w_in[l]))
        o_c, o_s, o_w = _nsa(qp, kcv, ksa, vst, kwp, vwt, gates_t, tables, ovt, cmp_pe[l], cmp_w1[l], cmp_b1[l],
                             cmp_w2[l])
        o_gmlp = _gmlp(gm_in, gmlp_ln_g[l], gmlp_ln_b[l], gmlp_ws[l], gmlp_bs[l])
        o_ssm = _ssd(z, xbc, dt, ssm_conv_w[l], ssm_conv_b[l], ssm_dt_bias[l], ssm_a_log[l], ssm_d[l],
                     ssm_norm_g[l])
        x = _mixer_outproj(x, mod[l], o_c, o_s, o_w, o_gmlp, o_ssm, w_out[l], ln_g[l, 1], ln_b[l, 1])
        x = _ffn_sublayer(x, mod[l], ffn_w1[l, 1], ffn_w3[l, 1], ffn_w2[l, 1], ln_g[l, 2], ln_b[l, 2], 2)
    return x
```

```python
import functools
import math

import numpy as np
import jax
import jax.numpy as jnp
from jax import lax
from jax.experimental import pallas as pl
from jax.experimental.pallas import tpu as pltpu

F32 = jnp.float32
BF16 = jnp.bfloat16
HI = lax.Precision.HIGHEST

D_MODEL = 1024
DEPTH = 2
ATT_HEADS = 8
ATT_KV_GROUPS = 2
ATT_HPG = 4
HEAD_DIM = 64
CMP_LEN = 32
CMP_STRIDE = 16
CMP_HIDDEN = 256
SLC_BLOCK = 64
SLC_TOPN = 16
WINDOW = 512
SEL_FORCE = 1.0e4
REL_BUCKETS = 32
REL_MAX_DIST = 128
GMLP_GROUPS = 4
GMLP_CHUNK = 128
D_GMLP = 256
SSM_HEADS = 4
SSM_HEAD_DIM = 64
D_SSM = 256
SSM_STATE = 128
SSM_CONV = 4
SSM_CHUNK = 256
D_XBC = 768
D_FF = 2816
N_SUB = 3
ALPHA = (2 * DEPTH) ** 0.25
LN_EPS = 1e-5

LANES = 128
LOG2E = math.log2(math.e)
NEG_BIG = -(2.0 ** 100)
KPAD = 512
CMP_KPAD = 16
CMP_CHUNK_LOG2 = 7
CMP_NEAR = 256
CMP_FLAG0 = HEAD_DIM + 1
CMP_SUBTILES = 2
CMP_KEY_SIZES = (384, 512, 640, 768, 896, 1024)
TQ_C = 256
TQ = 256
TK = 256
SEL_FAR_TILES = 2
SEL_SUBTILES = 2
SEL_AHEAD = 6
assert SEL_SUBTILES == 2 and SEL_FAR_TILES == 2
WIN_SUBTILES = 2
GATE_ROWS_PER_GROUP = 3 * ATT_HPG
GATE_ROWS = 16
TM = 512
FF_CHUNK = 256
SEL_CHUNK_BLOCKS = 128

COL_Q = 0
COL_KV = 512
COL_GATE = 1280
COL_GMLP = 1408
COL_Z = 1920
COL_XBC = 2176
COL_DT = 2944
N_COLS = 3072


def _cparams(n_axes, vmem_mb):
    return pltpu.CompilerParams(dimension_semantics=("arbitrary",) * n_axes,
                                vmem_limit_bytes=vmem_mb * 1024 * 1024)


def _sigmoid(x):
    return 1.0 / (1.0 + jnp.exp(-x))


def _gelu_tanh(x):
    return x * (0.5 * (1.0 + jnp.tanh(math.sqrt(2.0 / math.pi) * (x + 0.044715 * (x * x * x)))))


def _layer_norm(z, g, b):
    mu = jnp.mean(z, axis=-1, keepdims=True)
    d = z - mu
    var = jnp.mean(d * d, axis=-1, keepdims=True)
    return d * lax.rsqrt(var + LN_EPS) * g + b


def _dot(a, b):
    return jnp.dot(a, b, preferred_element_type=F32)


def _dot_hi(a, b):
    return jnp.dot(a, b, preferred_element_type=F32, precision=HI)


def _dot_nt(a, b):
    return lax.dot_general(a, b, (((1,), (1,)), ((), ())), preferred_element_type=F32)


def _mod_body(c_ref, w_ref, b_ref, o_ref):
    c = c_ref[...]
    o_ref[0] = _dot_hi(c * _sigmoid(c), w_ref[0]) + b_ref[0]


def _adaln_mod(c, ada_w, ada_b):
    n_layers, d, n = ada_w.shape
    bn = c.shape[0]
    tn = 1536
    cp = jnp.zeros((8, d), F32).at[:bn].set(c)
    out = pl.pallas_call(
        _mod_body,
        grid=(n_layers, n // tn),
        in_specs=[pl.BlockSpec((8, d), lambda l, j: (0, 0)),
                  pl.BlockSpec((1, d, tn), lambda l, j: (l, 0, j)),
                  pl.BlockSpec((1, 1, tn), lambda l, j: (l, 0, j))],
        out_specs=pl.BlockSpec((1, 8, tn), lambda l, j: (l, 0, j)),
        out_shape=jax.ShapeDtypeStruct((n_layers, 8, n), F32),
        compiler_params=_cparams(2, 32),
        name="adaln_mod",
    )(cp, ada_w, ada_b.reshape(n_layers, 1, n))
    return out[:, :bn].reshape(n_layers, bn, N_SUB * 3, d)


def _ffn_body(x_ref, mod_ref, w1_ref, w3_ref, w2_ref, g_ref, b_ref, o_ref, *, sub):
    x = x_ref[0]
    m = mod_ref[0]
    shift, scale, gate = m[3 * sub:3 * sub + 1], m[3 * sub + 1:3 * sub + 2], m[3 * sub + 2:3 * sub + 3]
    h = (x * (1.0 + scale) + shift).astype(BF16)
    acc = jnp.zeros(x.shape, F32)
    for j in range(D_FF // FF_CHUNK):
        sl = slice(j * FF_CHUNK, (j + 1) * FF_CHUNK)
        a = _dot(h, w1_ref[:, sl])
        b = _dot(h, w3_ref[:, sl])
        acc = acc + _dot((a * _sigmoid(a) * b).astype(BF16), w2_ref[sl, :])
    z = ALPHA * x + (0.5 * (1.0 + gate)) * acc
    o_ref[0] = _layer_norm(z, g_ref[...], b_ref[...])


def _ffn_sublayer(x, mod_l, w1, w3, w2, g, b, sub):
    bn, t, d = x.shape
    full = lambda shape: pl.BlockSpec(shape, lambda bi, i: (0,) * len(shape))
    return pl.pallas_call(
        functools.partial(_ffn_body, sub=sub),
        grid=(bn, t // TM),
        in_specs=[pl.BlockSpec((1, TM, d), lambda bi, i: (bi, i, 0)),
                  pl.BlockSpec((1, N_SUB * 3, d), lambda bi, i: (bi, 0, 0)),
                  full((d, D_FF)), full((d, D_FF)), full((D_FF, d)),
                  full((1, d)), full((1, d))],
        out_specs=pl.BlockSpec((1, TM, d), lambda bi, i: (bi, i, 0)),
        out_shape=jax.ShapeDtypeStruct(x.shape, F32),
        compiler_params=_cparams(2, 56),
        name="ffn",
    )(x, mod_l, w1.astype(BF16), w3.astype(BF16), w2.astype(BF16), g[None], b[None])


def _pad_cols(w, width):
    return jnp.pad(w, ((0, 0), (0, width - w.shape[1])))


def _relayout_w_in(w_in):
    cols = [w_in[:, 0:1280]]
    cols.append(_pad_cols(w_in[:, 1280:1304], LANES))
    cols.append(w_in[:, 1304:1816])
    cols.append(w_in[:, 1816:2072])
    cols.append(w_in[:, 2072:2840])
    cols.append(_pad_cols(w_in[:, 2840:2844], LANES))
    return jnp.concatenate(cols, axis=1).astype(BF16)


def _inproj_body(x_ref, mod_ref, w_ref, qp_ref, kcv_ref, ksa_ref, vst_ref, kwp_ref, vwt_ref,
                 gt_ref, gm_ref, z_ref, xbc_ref, dt_ref):
    i = pl.program_id(1)
    x = x_ref[0]
    m = mod_ref[0]
    h = (x * (1.0 + m[4:5]) + m[3:4]).astype(BF16)

    def proj(c0, width):
        return _dot(h, w_ref[:, c0:c0 + width])

    lane = lax.broadcasted_iota(jnp.int32, (1, LANES), 1)
    low = lane < HEAD_DIM
    q_flag = jnp.where(lane == HEAD_DIM, NEG_BIG, 0.0)

    def halves(slab):
        return jnp.where(low, slab, 0.0), jnp.where(low, pltpu.roll(slab, HEAD_DIM, axis=1), 0.0)

    def halves_t(slab):
        slab_t = slab.T
        pad = jnp.zeros((LANES - HEAD_DIM, slab.shape[0]), F32)
        ones_row = lax.broadcasted_iota(jnp.int32, (LANES, 1), 0) == HEAD_DIM
        return [jnp.where(ones_row, 1.0, jnp.concatenate([slab_t[HEAD_DIM * g:HEAD_DIM * (g + 1)], pad], axis=0))
                for g in range(ATT_KV_GROUPS)]

    for pr in range(ATT_HEADS // 2):
        for hh, q in enumerate(halves(proj(COL_Q + LANES * pr, LANES) * (HEAD_DIM ** -0.5 * LOG2E))):
            qp_ref[0, 2 * pr + hh] = (q + q_flag).astype(BF16)
    for j in range(2):
        for g, part in enumerate(halves(proj(COL_KV + LANES * j, LANES))):
            kcv_ref[0, ATT_KV_GROUPS * j + g] = part.astype(BF16)
    gates_t = _sigmoid(proj(COL_GATE, LANES)).T
    for g in range(ATT_KV_GROUPS):
        gt_ref[0, g] = gates_t[GATE_ROWS_PER_GROUP * g:GATE_ROWS_PER_GROUP * g + GATE_ROWS, :]
    gm_ref[0] = proj(COL_GMLP, 512)
    z_ref[0] = proj(COL_Z, D_SSM)
    xbc_ref[0] = proj(COL_XBC, D_XBC)
    dt_ref[0] = proj(COL_DT, LANES)

    @pl.when(i == 0)
    def _():
        dummy = jnp.broadcast_to(jnp.where(lane == HEAD_DIM, 1.0, 0.0), (TM, LANES)).astype(BF16)
        zeros = jnp.zeros((TM, LANES), BF16)
        for g in range(ATT_KV_GROUPS):
            ksa_ref[0, g, :, 0:LANES] = dummy
            ksa_ref[0, g, :, LANES:2 * LANES] = zeros
            kwp_ref[0, g] = dummy
            vst_ref[0, g] = jnp.zeros((LANES, TM), BF16)
            vwt_ref[0, g] = jnp.zeros((LANES, TM), BF16)

    @pl.when(i > 0)
    def _():
        pos = (i - 1) * TM + lax.broadcasted_iota(jnp.int32, (TM, LANES), 0)
        blk_lane = (pos >> 6) & (SEL_CHUNK_BLOCKS - 1)
        onehot = jnp.where(blk_lane == lax.broadcasted_iota(jnp.int32, (TM, LANES), 1), 1.0, 0.0).astype(BF16)
        ks = halves(proj(COL_KV + LANES * 2, LANES))
        kw = halves(proj(COL_KV + LANES * 4, LANES))
        vs_t = halves_t(proj(COL_KV + LANES * 3, LANES))
        vw_t = halves_t(proj(COL_KV + LANES * 5, LANES))
        for g in range(ATT_KV_GROUPS):
            ksa_ref[0, g, :, 0:LANES] = ks[g].astype(BF16)
            ksa_ref[0, g, :, LANES:2 * LANES] = onehot
            vst_ref[0, g] = vs_t[g].astype(BF16)
            kwp_ref[0, g] = kw[g].astype(BF16)
            vwt_ref[0, g] = vw_t[g].astype(BF16)


def _mixer_inproj(x, mod_l, w_in):
    bn, t, d = x.shape
    tp = t + KPAD
    row = lambda width: pl.BlockSpec((1, TM, width), lambda bi, i: (bi, jnp.maximum(i - 1, 0), 0))
    sds = jax.ShapeDtypeStruct
    return pl.pallas_call(
        _inproj_body,
        grid=(bn, t // TM + 1),
        in_specs=[row(d),
                  pl.BlockSpec((1, N_SUB * 3, d), lambda bi, i: (bi, 0, 0)),
                  pl.BlockSpec((d, N_COLS), lambda bi, i: (0, 0))],
        out_specs=[pl.BlockSpec((1, ATT_HEADS, TM, LANES), lambda bi, i: (bi, 0, jnp.maximum(i - 1, 0), 0)),
                   pl.BlockSpec((1, 4, TM, LANES), lambda bi, i: (bi, 0, jnp.maximum(i - 1, 0), 0)),
                   pl.BlockSpec((1, ATT_KV_GROUPS, TM, 2 * LANES), lambda bi, i: (bi, 0, i, 0)),
                   pl.BlockSpec((1, ATT_KV_GROUPS, LANES, TM), lambda bi, i: (bi, 0, 0, i)),
                   pl.BlockSpec((1, ATT_KV_GROUPS, TM, LANES), lambda bi, i: (bi, 0, i, 0)),
                   pl.BlockSpec((1, ATT_KV_GROUPS, LANES, TM), lambda bi, i: (bi, 0, 0, i)),
                   pl.BlockSpec((1, ATT_KV_GROUPS, GATE_ROWS, TM), lambda bi, i: (bi, 0, 0, jnp.maximum(i - 1, 0))),
                   row(512), row(D_SSM), row(D_XBC), row(LANES)],
        out_shape=[sds((bn, ATT_HEADS, t, LANES), BF16),
                   sds((bn, 4, t, LANES), BF16),
                   sds((bn, ATT_KV_GROUPS, tp, 2 * LANES), BF16),
                   sds((bn, ATT_KV_GROUPS, LANES, tp), BF16),
                   sds((bn, ATT_KV_GROUPS, tp, LANES), BF16),
                   sds((bn, ATT_KV_GROUPS, LANES, tp), BF16),
                   sds((bn, ATT_KV_GROUPS, GATE_ROWS, t), F32), sds((bn, t, 512), F32), sds((bn, t, D_SSM), F32),
                   sds((bn, t, D_XBC), F32), sds((bn, t, LANES), F32)],
        compiler_params=_cparams(2, 48),
        name="mixer_inproj",
    )(x, mod_l, w_in)


def _cmp_rows(t):
    return -(-(t // CMP_STRIDE + 120) // LANES) * LANES


def _compress_body(ck_ref, cv_ref, w1_ref, pe_ref, b1_ref, w2_ref, ovt_ref, k_ref, lhs_ref):
    n_rows = ck_ref.shape[2]
    rows = k_ref.shape[2]
    n_cmp = n_rows - CMP_LEN // CMP_STRIDE + 1
    half = CMP_STRIDE * HEAD_DIM
    row = lax.broadcasted_iota(jnp.int32, (rows, LANES), 0)
    lane = lax.broadcasted_iota(jnp.int32, (rows, LANES), 1)
    real = (row >= CMP_KPAD) & (row < CMP_KPAD + n_cmp)

    def mlp(c, j):
        w1 = w1_ref[j]
        h1 = _dot(c, w1[:half])
        h2 = _dot(c, w1[half:])
        cvec = _dot(pe_ref[j], w1)[0:1] + b1_ref[j]
        hid = h1 + pltpu.roll(h2, n_rows - 1, 0) + cvec
        out = _dot(_gelu_tanh(hid).astype(BF16), w2_ref[j])
        tail = rows - CMP_KPAD - n_rows
        full = jnp.concatenate([jnp.zeros((CMP_KPAD, LANES), F32), out, jnp.zeros((tail, LANES), F32)], axis=0)
        return jnp.where(real, full, 0.0)

    flags = jnp.where(lane == HEAD_DIM, jnp.where(real, 0.0, 1.0),
                      jnp.where(lane - CMP_FLAG0 == (row >> CMP_CHUNK_LOG2), 1.0, 0.0))
    k_ref[0, 0] = (mlp(ck_ref[0, 0], 0) + flags).astype(BF16)
    ones_row = lax.broadcasted_iota(jnp.int32, (LANES, 1), 0) == HEAD_DIM
    lhs_ref[0, 0, 0:LANES, :] = jnp.where(ones_row, 1.0, mlp(cv_ref[0, 0], 1).T).astype(BF16)
    lhs_ref[0, 0, LANES:, :] = ovt_ref[...].astype(BF16)


def _compress(kcv, cmp_pe, cmp_w1, cmp_b1, cmp_w2, ovt):
    bn, _, t, _ = kcv.shape
    n_rows = t // CMP_STRIDE
    width = CMP_STRIDE * HEAD_DIM
    c = kcv[..., :HEAD_DIM].reshape(bn, 4, n_rows, width)
    pe = jnp.zeros((2, 8, CMP_LEN * HEAD_DIM), F32).at[:, 0].set(cmp_pe.reshape(2, -1)).astype(BF16)
    w2 = jnp.pad(cmp_w2, ((0, 0), (0, 0), (0, LANES - HEAD_DIM))).astype(BF16)
    n_slc, rows = ovt.shape
    full = lambda shape: pl.BlockSpec(shape, lambda bi, g: (0,) * len(shape))
    return pl.pallas_call(
        _compress_body,
        grid=(bn, ATT_KV_GROUPS),
        in_specs=[pl.BlockSpec((1, 1, n_rows, width), lambda bi, g: (bi, g, 0, 0)),
                  pl.BlockSpec((1, 1, n_rows, width), lambda bi, g: (bi, ATT_KV_GROUPS + g, 0, 0)),
                  full((2, CMP_LEN * HEAD_DIM, CMP_HIDDEN)), full((2, 8, CMP_LEN * HEAD_DIM)),
                  full((2, 1, CMP_HIDDEN)), full((2, CMP_HIDDEN, LANES)), full((n_slc, rows))],
        out_specs=[pl.BlockSpec((1, 1, rows, LANES), lambda bi, g: (bi, g, 0, 0)),
                   pl.BlockSpec((1, 1, LANES + n_slc, rows), lambda bi, g: (bi, g, 0, 0))],
        out_shape=[jax.ShapeDtypeStruct((bn, ATT_KV_GROUPS, rows, LANES), BF16),
                   jax.ShapeDtypeStruct((bn, ATT_KV_GROUPS, LANES + n_slc, rows), BF16)],
        compiler_params=_cparams(2, 40),
        name="nsa_compress",
    )(c, c, cmp_w1.astype(BF16), pe, cmp_b1[:, None, :], w2, ovt)


def _rel_bucket(dist):
    n = jnp.maximum(dist, 0)
    max_exact = REL_BUCKETS // 2
    nf = jnp.maximum(n, 1).astype(F32)
    large = max_exact + (jnp.log(nf / max_exact) / math.log(REL_MAX_DIST / max_exact)
                         * (REL_BUCKETS - max_exact)).astype(jnp.int32)
    large = jnp.minimum(large, REL_BUCKETS - 1)
    return jnp.where(n < max_exact, n, large)


def _bias_tables(rel_bias):
    far = rel_bias[REL_BUCKETS - 1]

    def table(dist, valid, shift):
        onehot = (_rel_bucket(dist)[..., None] == jnp.arange(REL_BUCKETS)).astype(F32)
        b = jnp.einsum("qkb,bh->hqk", onehot, rel_bias, precision=HI)
        if shift:
            b = b - far[:, None, None]
        return jnp.where(valid[None], b * LOG2E, -jnp.inf)

    n_var = (1 << CMP_CHUNK_LOG2) * CMP_STRIDE // TQ_C
    d_c = jnp.asarray(TQ_C * np.arange(n_var)[:, None, None] + np.arange(TQ_C)[None, None, :]
                      - CMP_STRIDE * np.arange(CMP_NEAR)[None, :, None]
                      + (CMP_STRIDE * CMP_KPAD - CMP_LEN + 1))
    nb_c = jnp.stack([table(d_c[v], d_c[v] >= 0, True) for v in range(n_var)])
    nb_c = jnp.maximum(nb_c, NEG_BIG)
    ki = np.arange(2 * TK)[:, None]
    d_s = jnp.asarray(np.arange(TQ)[None, :] - ki + TK)
    nb_s = table(d_s, d_s >= 0, True)
    d_w = jnp.asarray(np.arange(TQ)[None, :] - np.arange(TQ + WINDOW)[:, None] + WINDOW)
    nb_w = table(d_w, (d_w >= 0) & (d_w < WINDOW), False)
    return nb_c, nb_s, nb_w


def _overlap_matrix(t):
    n_cmp = (t - CMP_LEN) // CMP_STRIDE + 1
    n_slc = t // SLC_BLOCK
    cs = np.arange(n_cmp) * CMP_STRIDE
    ss = np.arange(n_slc) * SLC_BLOCK
    ov = np.clip(np.minimum(cs[:, None] + CMP_LEN, ss[None, :] + SLC_BLOCK)
                 - np.maximum(cs[:, None], ss[None, :]), 0, None) / CMP_LEN
    full = np.zeros((n_slc, _cmp_rows(t)), np.float32)
    full[:, CMP_KPAD:CMP_KPAD + n_cmp] = ov.T
    return jnp.asarray(full)


def _gate_row(gt_ref, r, branch, qs):
    row = 3 * r + branch
    return gt_ref[0, 0, row:row + 1, qs]


def _cmp_attn_body(q_ref, k_ref, lhs_ref, nba_ref, nbb_ref, gt_ref, o_ref, imp_ref, s_ref):
    step = pl.program_id(2)
    rows = k_ref.shape[2]
    chunk = 1 << CMP_CHUNK_LOG2
    nb_refs = (nba_ref, nbb_ref)
    rowi = lax.broadcasted_iota(jnp.int32, (LANES, TQ_C), 0)

    def near_chunk_of(sub):
        return ((CMP_SUBTILES * step + sub) * (TQ_C // CMP_STRIDE)) >> CMP_CHUNK_LOG2

    def body(n_keys):
        kf = k_ref[0, 0, 0:n_keys, :]
        lhs = lhs_ref[0, 0, :, 0:n_keys]
        for sub in range(CMP_SUBTILES):
            near_chunk = near_chunk_of(sub)
            chunk_flag = jnp.where((rowi >= CMP_FLAG0) & (rowi - CMP_FLAG0 > near_chunk + 1), NEG_BIG, 0.0)
            qs = slice(sub * TQ_C, (sub + 1) * TQ_C)
            q_all = jnp.concatenate([(q_ref[0, r, qs, :].astype(F32).T + chunk_flag).astype(BF16)
                                     for r in range(ATT_HPG)], axis=1)
            s_ref[sub, 0:n_keys, :] = _dot(kf, q_all)
        for sub in range(CMP_SUBTILES):
            qs = slice(sub * TQ_C, (sub + 1) * TQ_C)
            a0 = pl.multiple_of(near_chunk_of(sub) << CMP_CHUNK_LOG2, chunk)
            ps = []
            for r in range(ATT_HPG):
                hs = slice(r * TQ_C, (r + 1) * TQ_C)
                s_ref[sub, pl.ds(a0, CMP_NEAR), hs] += nb_refs[sub][0, r]
                s = s_ref[sub, 0:n_keys, hs]
                mx = jnp.max(s, axis=0, keepdims=True)
                mx = jnp.where(mx < 0.5 * NEG_BIG, 0.0, mx)
                ps.append(jnp.exp2(s - mx).astype(BF16))
            res_all = _dot(lhs, jnp.concatenate(ps, axis=1))
            imp = jnp.zeros((imp_ref.shape[2], TQ_C), F32)
            outs = []
            for r in range(ATT_HPG):
                res = res_all[:, r * TQ_C:(r + 1) * TQ_C]
                inv = 1.0 / jnp.maximum(res[HEAD_DIM:HEAD_DIM + 1, :], 1e-30)
                outs.append((res[0:LANES, :] * inv * _gate_row(gt_ref, r, 0, qs)).T[:, :HEAD_DIM])
                imp = imp + res[LANES:, :] * inv
            o_ref[0, qs, :] = jnp.concatenate(outs, axis=1)
            imp_ref[0, 0, :, qs] = imp

    need = (near_chunk_of(CMP_SUBTILES - 1) + 2) * chunk
    sizes = sorted({min(rows, s) for s in CMP_KEY_SIZES} | {rows})
    lo = 0
    for n_keys in sizes:
        @pl.when((need > lo) & (need <= n_keys))
        def _(n_keys=n_keys):
            body(n_keys)
        lo = n_keys


def _gate_spec(tq):
    return pl.BlockSpec((1, 1, GATE_ROWS, tq), lambda bi, g, m: (bi, g, 0, m))


def _cmp_attention(qp, kcp, lhs, nb_c, gates_t):
    bn, _, t, _ = qp.shape
    rows = kcp.shape[2]
    n_slc = lhs.shape[2] - LANES
    n_var = nb_c.shape[0]
    tq = CMP_SUBTILES * TQ_C
    table = lambda sub: pl.BlockSpec((1, ATT_HPG, CMP_NEAR, TQ_C),
                                     lambda bi, g, m: ((CMP_SUBTILES * m + sub) % n_var, g, 0, 0))
    return pl.pallas_call(
        _cmp_attn_body,
        grid=(bn, ATT_KV_GROUPS, t // tq),
        in_specs=[pl.BlockSpec((1, ATT_HPG, tq, LANES), lambda bi, g, m: (bi, g, m, 0)),
                  pl.BlockSpec((1, 1, rows, LANES), lambda bi, g, m: (bi, g, 0, 0)),
                  pl.BlockSpec((1, 1, LANES + n_slc, rows), lambda bi, g, m: (bi, g, 0, 0)),
                  table(0), table(1), _gate_spec(tq)],
        out_specs=[pl.BlockSpec((1, tq, ATT_HPG * HEAD_DIM), lambda bi, g, m: (bi, m, g)),
                   pl.BlockSpec((1, 1, n_slc, tq), lambda bi, g, m: (bi, g, 0, m))],
        out_shape=[jax.ShapeDtypeStruct((bn, t, ATT_HEADS * HEAD_DIM), F32),
                   jax.ShapeDtypeStruct((bn, ATT_KV_GROUPS, n_slc, t), F32)],
        scratch_shapes=[pltpu.VMEM((CMP_SUBTILES, rows, ATT_HPG * TQ_C), F32)],
        compiler_params=_cparams(3, 48),
        name="nsa_cmp_attn",
    )(qp, kcp, lhs, nb_c, nb_c, gates_t)


TOPK_COLS = 1024
TOPK_ROW_SIZES = (64, 128, 192)


def _topk_body(imp_ref, o_ref):
    i = pl.program_id(2)
    n_slc, cols = imp_ref.shape[2:]
    t = i * cols + lax.broadcasted_iota(jnp.int32, (1, cols), 1)
    cur = t >> 6

    def body(n_rows):
        blk = lax.broadcasted_iota(jnp.int32, (n_rows, cols), 0)
        forced = (blk == 0) | (blk == cur) | (blk == cur - 1)
        vals = jnp.where(forced, SEL_FORCE, jnp.where(blk <= cur, imp_ref[0, 0, 0:n_rows, :], -SEL_FORCE))
        blkf = blk.astype(F32)
        for _ in range(min(SLC_TOPN, n_rows)):
            mx = jnp.max(vals, axis=0, keepdims=True)
            first = jnp.min(jnp.where(vals == mx, blkf, float(n_slc)), axis=0, keepdims=True)
            vals = jnp.where(blkf == first, -jnp.inf, vals)
        o_ref[0, 0, 0:n_rows, :] = jnp.where(vals == -jnp.inf, 0.0, NEG_BIG).astype(BF16)
        if n_rows < n_slc:
            o_ref[0, 0, n_rows:, :] = jnp.full((n_slc - n_rows, cols), NEG_BIG, BF16)

    need = ((i + 1) * cols) >> 6
    sizes = sorted({min(n_slc, s) for s in TOPK_ROW_SIZES} | {n_slc})
    lo = 0
    for n_rows in sizes:
        @pl.when((need > lo) & (need <= n_rows))
        def _(n_rows=n_rows):
            body(n_rows)
        lo = n_rows


def _select_blocks(imp_t):
    bn, g, n_slc, t = imp_t.shape
    cols = min(TOPK_COLS, t)
    spec = pl.BlockSpec((1, 1, n_slc, cols), lambda bi, gi, i: (bi, gi, 0, i))
    return pl.pallas_call(
        _topk_body,
        grid=(bn, g, t // cols),
        in_specs=[spec],
        out_specs=spec,
        out_shape=jax.ShapeDtypeStruct(imp_t.shape, BF16),
        compiler_params=_cparams(3, 40),
        name="nsa_topk",
    )(imp_t)


def _sel_attn_body(q_ref, sel_ref, k_ref, vt_ref, nb_ref, gt_ref, o_ref, qat_ref, m_ref, acc_ref, sc0_ref,
                   sc1_ref):
    sc_refs = (sc0_ref, sc1_ref)
    pair = pl.program_id(2)
    n_chunks = qat_ref.shape[0]
    m_a = SEL_SUBTILES * pair
    for sub in range(SEL_SUBTILES):
        qs = slice(sub * TQ, (sub + 1) * TQ)
        for r in range(ATT_HPG):
            q_t = q_ref[0, r, qs, :].astype(F32).T.astype(BF16)
            for c in range(n_chunks):
                qat_ref[c, ATT_HPG * sub + r, 0:LANES, :] = q_t
        for c in range(n_chunks):
            sel_t = sel_ref[0, 0, c * LANES:(c + 1) * LANES, qs]
            for r in range(ATT_HPG):
                qat_ref[c, ATT_HPG * sub + r, LANES:2 * LANES, :] = sel_t
    m_ref[...] = jnp.full(m_ref.shape, -jnp.inf, F32)
    acc_ref[...] = jnp.zeros(acc_ref.shape, F32)

    tiles_per_chunk = SEL_CHUNK_BLOCKS * SLC_BLOCK // TK
    units_a = tuple(range(ATT_HPG))
    units_b = tuple(range(ATT_HPG, 2 * ATT_HPG))

    def load(j, n_tiles):
        row0 = pl.multiple_of(KPAD + TK * j, TK)
        return (k_ref[0, 0, pl.ds(row0, n_tiles * TK), :], vt_ref[0, 0, :, pl.ds(row0, n_tiles * TK)],
                jnp.maximum(j, 0) // tiles_per_chunk)

    def qk(tile, u):
        kt, _, c = tile
        return _dot(kt, qat_ref[c, u])

    def update(tile, u, s, bias):
        if bias is not None:
            s = bias(u % ATT_HPG) + s
        m_prev = m_ref[u]
        m_new = jnp.maximum(m_prev, jnp.max(s, axis=0, keepdims=True))
        alpha = jnp.exp2(m_prev - m_new)
        p = jnp.exp2(s - m_new).astype(BF16)
        acc_ref[u] = alpha * acc_ref[u] + _dot(tile[1], p)
        m_ref[u] = m_new

    def step(work):
        todo = [(tile, u, bias) for tile, units, bias in work for u in units]
        scores = [qk(tile, u) for tile, u, _ in todo[:SEL_AHEAD]]
        for i, (tile, u, bias) in enumerate(todo):
            if i + SEL_AHEAD < len(todo):
                scores.append(qk(todo[i + SEL_AHEAD][0], todo[i + SEL_AHEAD][1]))
            s = scores[i]
            scores[i] = None
            update(tile, u, s, bias)

    def pipelined_step(tile, slot, next_tile):
        for i, u in enumerate(all_units):
            if next_tile is not None:
                for k in range(2 * i, min(2 * i + 2, len(all_units))):
                    sc_refs[1 - slot][k] = qk(next_tile, all_units[k])
            update(tile, u, sc_refs[slot][i], None)

    all_units = units_a + units_b
    one_trip = jnp.minimum(pair + 1, 1)

    def near(i, carry):
        rows = pl.ds(pl.multiple_of(TK * (1 - i), TK), TK)
        bias = lambda r: nb_ref[r, rows, :]
        step([(load(m_a - i, 1), units_a, bias), (load(m_a + 1 - i, 1), units_b, bias)])
        return carry

    lax.fori_loop(0, 2, near, 0)

    def below(i, carry):
        step([(load(m_a - 2, 1), units_a, None), (load(m_a - 2, 2), units_b, None)])
        return carry

    lax.fori_loop(0, one_trip, below, 0)

    n_big = jnp.maximum(m_a - 2, 0) // SEL_FAR_TILES
    far_tile = lambda i: load(SEL_FAR_TILES * i, SEL_FAR_TILES)

    @pl.when(n_big > 0)
    def _():
        first = far_tile(0)
        for k, u in enumerate(all_units):
            sc_refs[0][k] = qk(first, u)
        n_pairs = (n_big - 1) // 2

        def far_pair(i, carry):
            pipelined_step(far_tile(2 * i), 0, far_tile(2 * i + 1))
            pipelined_step(far_tile(2 * i + 1), 1, far_tile(2 * i + 2))
            return carry

        lax.fori_loop(0, n_pairs, far_pair, 0)

        @pl.when(n_big - 2 * n_pairs == 1)
        def _():
            pipelined_step(far_tile(n_big - 1), 0, None)

        @pl.when(n_big - 2 * n_pairs == 2)
        def _():
            pipelined_step(far_tile(n_big - 2), 0, far_tile(n_big - 1))
            pipelined_step(far_tile(n_big - 1), 1, None)

    for sub in range(SEL_SUBTILES):
        outs = []
        for r in range(ATT_HPG):
            acc = acc_ref[ATT_HPG * sub + r]
            gate = _gate_row(gt_ref, r, 1, slice(sub * TQ, (sub + 1) * TQ))
            outs.append((acc / acc[HEAD_DIM:HEAD_DIM + 1, :] * gate).T[:, :HEAD_DIM])
        o_ref[0, sub * TQ:(sub + 1) * TQ, :] = jnp.concatenate(outs, axis=1)


def _sel_attention(qp, selneg, ksa, vst, nb_s, gates_t):
    bn, _, t, _ = qp.shape
    tp = ksa.shape[2]
    n_slc = selneg.shape[2]
    n_chunks = max(n_slc // SEL_CHUNK_BLOCKS, 1)
    tq = SEL_SUBTILES * TQ
    n_units = SEL_SUBTILES * ATT_HPG
    return pl.pallas_call(
        _sel_attn_body,
        grid=(bn, ATT_KV_GROUPS, t // tq),
        in_specs=[pl.BlockSpec((1, ATT_HPG, tq, LANES), lambda bi, g, m: (bi, g, m, 0)),
                  pl.BlockSpec((1, 1, n_slc, tq), lambda bi, g, m: (bi, g, 0, m)),
                  pl.BlockSpec((1, 1, tp, 2 * LANES), lambda bi, g, m: (bi, g, 0, 0)),
                  pl.BlockSpec((1, 1, LANES, tp), lambda bi, g, m: (bi, g, 0, 0)),
                  pl.BlockSpec((ATT_HPG, 2 * TK, TQ), lambda bi, g, m: (g, 0, 0)),
                  _gate_spec(tq)],
        out_specs=pl.BlockSpec((1, tq, ATT_HPG * HEAD_DIM), lambda bi, g, m: (bi, m, g)),
        out_shape=jax.ShapeDtypeStruct((bn, t, ATT_HEADS * HEAD_DIM), F32),
        scratch_shapes=[pltpu.VMEM((n_chunks, n_units, 2 * LANES, TQ), BF16),
                        pltpu.VMEM((n_units, 1, TQ), F32),
                        pltpu.VMEM((n_units, LANES, TQ), F32),
                        pltpu.VMEM((n_units, SEL_FAR_TILES * TK, TQ), F32),
                        pltpu.VMEM((n_units, SEL_FAR_TILES * TK, TQ), F32)],
        compiler_params=_cparams(3, 56),
        name="nsa_sel_attn",
    )(qp, selneg, ksa, vst, nb_s, gates_t)


def _win_attn_body(q_ref, k_ref, vt_ref, nb_ref, gt_ref, o_ref):
    step = pl.program_id(2)
    keys = TQ + WINDOW
    tiles = []
    for sub in range(WIN_SUBTILES):
        row0 = pl.multiple_of((WIN_SUBTILES * step + sub) * TQ, TQ)
        kt = k_ref[0, 0, pl.ds(row0, keys), :]
        vt = vt_ref[0, 0, :, pl.ds(row0, keys)]
        qs = slice(sub * TQ, (sub + 1) * TQ)
        q_all = jnp.concatenate([q_ref[0, r, qs, :].astype(F32).T.astype(BF16) for r in range(ATT_HPG)], axis=1)
        tiles.append((vt, qs, _dot(kt, q_all)))
    for vt, qs, s_all in tiles:
        ps = []
        for r in range(ATT_HPG):
            s = s_all[:, r * TQ:(r + 1) * TQ] + nb_ref[r]
            ps.append(jnp.exp2(s - jnp.max(s, axis=0, keepdims=True)).astype(BF16))
        res_all = _dot(vt, jnp.concatenate(ps, axis=1))
        outs = []
        for r in range(ATT_HPG):
            res = res_all[:, r * TQ:(r + 1) * TQ]
            gate = _gate_row(gt_ref, r, 2, qs)
            outs.append((res / res[HEAD_DIM:HEAD_DIM + 1, :] * gate).T[:, :HEAD_DIM])
        o_ref[0, qs, :] = jnp.concatenate(outs, axis=1)


def _win_attention(qp, kwp, vwt, nb_w, gates_t):
    bn, _, t, _ = qp.shape
    tp = kwp.shape[2]
    tq = WIN_SUBTILES * TQ
    return pl.pallas_call(
        _win_attn_body,
        grid=(bn, ATT_KV_GROUPS, t // tq),
        in_specs=[pl.BlockSpec((1, ATT_HPG, tq, LANES), lambda bi, g, m: (bi, g, m, 0)),
                  pl.BlockSpec((1, 1, tp, LANES), lambda bi, g, m: (bi, g, 0, 0)),
                  pl.BlockSpec((1, 1, LANES, tp), lambda bi, g, m: (bi, g, 0, 0)),
                  pl.BlockSpec((ATT_HPG, TQ + WINDOW, TQ), lambda bi, g, m: (g, 0, 0)),
                  _gate_spec(tq)],
        out_specs=pl.BlockSpec((1, tq, ATT_HPG * HEAD_DIM), lambda bi, g, m: (bi, m, g)),
        out_shape=jax.ShapeDtypeStruct((bn, t, ATT_HEADS * HEAD_DIM), F32),
        compiler_params=_cparams(3, 48),
        name="nsa_win_attn",
    )(qp, kwp, vwt, nb_w, gates_t)


GMLP_ROWS = 1024


def _gmlp_body(uv_ref, g_ref, b_ref, ws_ref, bs_ref, o_ref):
    act = _gelu_tanh(uv_ref[0])
    u = act[:, :D_GMLP]
    v = _layer_norm(act[:, D_GMLP:], g_ref[...], b_ref[...])
    ti = lax.broadcasted_iota(jnp.int32, (GMLP_CHUNK, GMLP_CHUNK), 0)
    si = lax.broadcasted_iota(jnp.int32, (GMLP_CHUNK, GMLP_CHUNK), 1)
    w = [jnp.where(ti >= si, ws_ref[g], 0.0).astype(BF16) for g in range(GMLP_GROUPS)]
    lane = lax.broadcasted_iota(jnp.int32, (1, LANES), 1)
    lo = lane < HEAD_DIM
    for c in range(uv_ref.shape[1] // GMLP_CHUNK):
        rs = slice(c * GMLP_CHUNK, (c + 1) * GMLP_CHUNK)
        pairs = []
        for pr in range(GMLP_GROUPS // 2):
            vp = v[rs, pr * LANES:(pr + 1) * LANES]
            pairs.append(_dot(w[2 * pr], jnp.where(lo, vp, 0.0).astype(BF16))
                         + _dot(w[2 * pr + 1], jnp.where(lo, 0.0, vp).astype(BF16)))
        o_ref[0, rs, :] = u[rs] * (jnp.concatenate(pairs, axis=1) + bs_ref[...])


def _gmlp(uv, ln_g, ln_b, ws, bs):
    bn, t, _ = uv.shape
    rows = min(GMLP_ROWS, t)
    bs_lanes = jnp.repeat(bs.T, D_GMLP // GMLP_GROUPS, axis=1)
    full = lambda shape: pl.BlockSpec(shape, lambda bi, i: (0,) * len(shape))
    return pl.pallas_call(
        _gmlp_body,
        grid=(bn, t // rows),
        in_specs=[pl.BlockSpec((1, rows, 2 * D_GMLP), lambda bi, i: (bi, i, 0)),
                  full((1, D_GMLP)), full((1, D_GMLP)),
                  full((GMLP_GROUPS, GMLP_CHUNK, GMLP_CHUNK)), full((GMLP_CHUNK, D_GMLP))],
        out_specs=pl.BlockSpec((1, rows, D_GMLP), lambda bi, i: (bi, i, 0)),
        out_shape=jax.ShapeDtypeStruct((bn, t, D_GMLP), F32),
        compiler_params=_cparams(2, 40),
        name="gmlp",
    )(uv, ln_g[None], ln_b[None], ws, bs_lanes)


def _ssd_body(z_ref, xbc_ref, prev_ref, dt_ref, cw_ref, cb_ref, dtb_ref, alog_ref, dsk_ref, ng_ref, e4_ref,
              o_ref, xe_ref, state_ref):
    c = pl.program_id(1)
    L = SSM_CHUNK

    @pl.when(c == 0)
    def _():
        state_ref[...] = jnp.zeros(state_ref.shape, F32)

    xe_ref[0:8] = jnp.where(c > 0, prev_ref[0], 0.0)
    xe_ref[8:8 + L] = xbc_ref[0]
    conv = cb_ref[...]
    for k in range(SSM_CONV):
        conv = conv + cw_ref[k:k + 1, :] * xe_ref[pl.ds(8 - (SSM_CONV - 1) + k, L), :]
    xc = conv * _sigmoid(conv)
    xs, bm, cm = xc[:, :D_SSM], xc[:, D_SSM:D_SSM + 2 * SSM_STATE], xc[:, D_SSM + 2 * SSM_STATE:]

    dtr = dt_ref[0] + dtb_ref[...]
    dt = jnp.maximum(dtr, 0.0) + jnp.log1p(jnp.exp(-jnp.abs(dtr)))
    a = -jnp.exp(alog_ref[...]) * dt
    ti = lax.broadcasted_iota(jnp.int32, (L, L), 0)
    si = lax.broadcasted_iota(jnp.int32, (L, L), 1)
    causal = ti >= si
    cs = _dot_hi(jnp.where(causal, 1.0, 0.0), a)
    cs_t = cs.T
    e4 = e4_ref[...]
    dt_e = _dot_hi(dt, e4)
    cs_e = _dot_hi(cs, e4)
    last_e = cs_e[L - 1:L, :]
    x_dt = xs * dt_e
    x_dec = (x_dt * jnp.exp(last_e - cs_e)).astype(BF16)
    x_dt = x_dt.astype(BF16)
    grow = jnp.exp(cs_e)
    lane = lax.broadcasted_iota(jnp.int32, (1, LANES), 1)
    lo = lane < SSM_HEAD_DIM
    zero = jnp.zeros((), BF16)

    ys = []
    for g in range(2):
        gs = slice(g * LANES, (g + 1) * LANES)
        cg = cm[:, gs].astype(BF16)
        bg = bm[:, gs]
        scores = _dot_nt(cg, bg.astype(BF16))
        xp = x_dt[:, gs]
        y = None
        for hh in range(2):
            h = 2 * g + hh
            seg = jnp.broadcast_to(cs[:, h:h + 1], (L, L)) - jnp.broadcast_to(cs_t[h:h + 1, :], (L, L))
            w = (scores * jnp.where(causal, jnp.exp(seg), 0.0)).astype(BF16)
            xh = jnp.where(lo, xp, zero) if hh == 0 else jnp.where(lo, zero, xp)
            yh = _dot(w, xh)
            y = yh if y is None else y + yh
        state = state_ref[g]
        y = y + _dot(cg, state.astype(BF16)) * grow[:, gs]
        state_ref[g] = jnp.exp(last_e[:, gs]) * state + _dot(bg.T.astype(BF16), x_dec[:, gs])
        ys.append(y)
    y = jnp.concatenate(ys, axis=1) + xs * dsk_ref[...]
    zz = z_ref[0]
    gg = y * (zz * _sigmoid(zz))
    outs = []
    for g in range(2):
        gp = gg[:, g * LANES:(g + 1) * LANES]
        outs.append(gp * lax.rsqrt(jnp.mean(gp * gp, axis=-1, keepdims=True) + LN_EPS))
    o_ref[0] = jnp.concatenate(outs, axis=1) * ng_ref[...]


def _head_lanes(v):
    return jnp.repeat(v, SSM_HEAD_DIM)[None]


def _ssd(z, xbc, dt, conv_w, conv_b, dt_bias, a_log, d_skip, norm_g):
    bn, t, _ = z.shape
    L = SSM_CHUNK
    first_lanes = lambda v: jnp.zeros((1, LANES), F32).at[0, :SSM_HEADS].set(v)
    e4 = np.zeros((LANES, D_SSM), np.float32)
    for h in range(SSM_HEADS):
        e4[h, h * SSM_HEAD_DIM:(h + 1) * SSM_HEAD_DIM] = 1.0
    full = lambda shape: pl.BlockSpec(shape, lambda bi, c: (0,) * len(shape))
    return pl.pallas_call(
        _ssd_body,
        grid=(bn, t // L),
        in_specs=[pl.BlockSpec((1, L, D_SSM), lambda bi, c: (bi, c, 0)),
                  pl.BlockSpec((1, L, D_XBC), lambda bi, c: (bi, c, 0)),
                  pl.BlockSpec((1, 8, D_XBC), lambda bi, c: (bi, jnp.maximum(c * (L // 8) - 1, 0), 0)),
                  pl.BlockSpec((1, L, LANES), lambda bi, c: (bi, c, 0)),
                  full((SSM_CONV, D_XBC)), full((1, D_XBC)), full((1, LANES)), full((1, LANES)),
                  full((1, D_SSM)), full((1, D_SSM)), full((LANES, D_SSM))],
        out_specs=pl.BlockSpec((1, L, D_SSM), lambda bi, c: (bi, c, 0)),
        out_shape=jax.ShapeDtypeStruct((bn, t, D_SSM), F32),
        scratch_shapes=[pltpu.VMEM((L + 8, D_XBC), F32),
                        pltpu.VMEM((2, SSM_STATE, LANES), F32)],
        compiler_params=_cparams(2, 40),
        name="ssd",
    )(z, xbc, xbc, dt, conv_w, conv_b[None], first_lanes(dt_bias), first_lanes(a_log),
      _head_lanes(d_skip), norm_g[None], jnp.asarray(e4))


def _outproj_body(x_ref, mod_ref, oc_ref, os_ref, ow_ref, gm_ref, ssm_ref, w_ref, g_ref, b_ref, o_ref):
    x = x_ref[0]
    m = mod_ref[0]
    gate = m[5:6]
    d_att = ATT_HEADS * HEAD_DIM
    o_att = oc_ref[0] + os_ref[0] + ow_ref[0]
    y = (_dot(o_att.astype(BF16), w_ref[0:d_att, :])
         + _dot(gm_ref[0].astype(BF16), w_ref[d_att:d_att + D_GMLP, :])
         + _dot(ssm_ref[0].astype(BF16), w_ref[d_att + D_GMLP:, :]))
    z = ALPHA * x + (1.0 + gate) * y
    o_ref[0] = _layer_norm(z, g_ref[...], b_ref[...])


def _mixer_outproj(x, mod_l, o_c, o_s, o_w, o_gmlp, o_ssm, w_out, g, b):
    bn, t, d = x.shape
    d_att = ATT_HEADS * HEAD_DIM
    row = lambda width: pl.BlockSpec((1, TM, width), lambda bi, i: (bi, i, 0))
    full = lambda shape: pl.BlockSpec(shape, lambda bi, i: (0,) * len(shape))
    return pl.pallas_call(
        _outproj_body,
        grid=(bn, t // TM),
        in_specs=[row(d), pl.BlockSpec((1, N_SUB * 3, d), lambda bi, i: (bi, 0, 0)),
                  row(d_att), row(d_att), row(d_att),
                  row(D_GMLP), row(D_SSM), full((d, d)), full((1, d)), full((1, d))],
        out_specs=row(d),
        out_shape=jax.ShapeDtypeStruct(x.shape, F32),
        compiler_params=_cparams(2, 48),
        name="mixer_outproj",
    )(x, mod_l, o_c, o_s, o_w, o_gmlp, o_ssm, w_out.astype(BF16), g[None], b[None])


def _nsa(qp, kcv, ksa, vst, kwp, vwt, gates_t, tables, ovt, cmp_pe, cmp_w1, cmp_b1, cmp_w2):
    nb_c, nb_s, nb_w = tables
    kcp, lhs = _compress(kcv, cmp_pe, cmp_w1, cmp_b1, cmp_w2, ovt)
    o_c, imp_t = _cmp_attention(qp, kcp, lhs, nb_c, gates_t)
    selneg_t = _select_blocks(imp_t)
    o_s = _sel_attention(qp, selneg_t, ksa, vst, nb_s, gates_t)
    o_w = _win_attention(qp, kwp, vwt, nb_w, gates_t)
    return o_c, o_s, o_w


def kernel(x, c, rel_bias, ada_w, ada_b, ln_g, ln_b, ffn_w1, ffn_w3, ffn_w2, w_in, w_out, cmp_pe, cmp_w1, cmp_b1,
           cmp_w2, gmlp_ln_g, gmlp_ln_b, gmlp_ws, gmlp_bs, ssm_conv_w, ssm_conv_b, ssm_dt_bias, ssm_a_log, ssm_d,
           ssm_norm_g):
    t = x.shape[1]
    mod = _adaln_mod(c, ada_w, ada_b)
    tables = _bias_tables(rel_bias)
    ovt = _overlap_matrix(t)
    for l in range(ada_w.shape[0]):
        x = _ffn_sublayer(x, mod[l], ffn_w1[l, 0], ffn_w3[l, 0], ffn_w2[l, 0], ln_g[l, 0], ln_b[l, 0], 0)
        qp, kcv, ksa, vst, kwp, vwt, gates_t, gm_in, z, xbc, dt = _mixer_inproj(x, mod[l], _relayout_w_in(w_in[l]))
        o_c, o_s, o_w = _nsa(qp, kcv, ksa, vst, kwp, vwt, gates_t, tables, ovt, cmp_pe[l], cmp_w1[l], cmp_b1[l],
                             cmp_w2[l])
        o_gmlp = _gmlp(gm_in, gmlp_ln_g[l], gmlp_ln_b[l], gmlp_ws[l], gmlp_bs[l])
        o_ssm = _ssd(z, xbc, dt, ssm_conv_w[l], ssm_conv_b[l], ssm_dt_bias[l], ssm_a_log[l], ssm_d[l],
                     ssm_norm_g[l])
        x = _mixer_outproj(x, mod[l], o_c, o_s, o_w, o_gmlp, o_ssm, w_out[l], ln_g[l, 1], ln_b[l, 1])
        x = _ffn_sublayer(x, mod[l], ffn_w1[l, 1], ffn_w3[l, 1], ffn_w2[l, 1], ln_g[l, 2], ln_b[l, 2], 2)
    return x
```
